```python
import functools
import jax, jax.numpy as jnp
from jax import lax
import numpy as np

D_MODEL = 2048
BATCH = 4
SEQ = 2048
DEPTH = 2
DEC_BATCH = 128
DEC_SEQ = 8
PAST_LEN = 16384
PAGE_SIZE = 128

N_MIXERS = 2
N_CONV_LAYERS = (DEPTH + 1) // 2
N_RWKV_LAYERS = DEPTH // 2
D_CONV = D_MODEL
CONV_WIDTH = 3
HEAD_SIZE = 64
N_HEADS = D_MODEL // HEAD_SIZE
D_DECAY_LORA = max(32, int(round(1.8 * D_MODEL ** 0.5 / 32)) * 32)
D_AAA_LORA = max(32, int(round(1.8 * D_MODEL ** 0.5 / 32)) * 32)
D_GATE_LORA = max(32, int(round(0.6 * D_MODEL ** 0.8 / 32)) * 32)
D_FF = ((8 * D_MODEL // 3 + 255) // 256) * 256
N_SUBLAYERS = 3
N_SHIFT_MIX = 6
HALF_STEP = 0.5
RMS_EPS = 1e-6
GN_EPS = 64e-5
NORM_EPS = 1e-12

kernel_name = 'hybrid_shortconv_rwkv7_macaron_adaln_step'


def rms_norm(x, g):
    xf = x.astype(jnp.float32)
    y = xf * lax.rsqrt(jnp.mean(xf * xf, axis=-1, keepdims=True) + RMS_EPS)
    return (y * g.astype(jnp.float32)).astype(x.dtype)


def modulated_pre(x, g, shift, scale):
    return rms_norm(x, g) * (1 + scale) + shift


def gated_post(x, y, g, gate, res_w):
    return x + res_w * gate * rms_norm(y, g)


def swiglu(h, w_in, w_out):
    gt, up = jnp.split(h @ w_in, 2, axis=-1)
    return (jax.nn.silu(gt) * up) @ w_out


def short_conv_mixer(h, conv_state, w_in, conv_w, w_out):
    t_len = h.shape[1]
    b_gate, c_gate, xin = jnp.split(h @ w_in, 3, axis=-1)
    u = c_gate * xin
    u_ext = jnp.concatenate([conv_state.astype(u.dtype), u], axis=1)
    z = sum(conv_w[j] * u_ext[:, j:j + t_len] for j in range(CONV_WIDTH))
    y = (b_gate * z) @ w_out
    return y, u_ext[:, -(CONV_WIDTH - 1):]


def rwkv7_mixer(h, shift_state, wkv_state, mix, w0, w1, w2, a0, a1, a2, g1, g2,
                k_k, k_a, r_k, w_r, w_k, w_v, w_o, ln_w, ln_b):
    nb, t_len, d = h.shape
    h_prev = jnp.concatenate([shift_state[:, None].astype(h.dtype), h[:, :-1]], axis=1)
    xx = h_prev - h
    xs = h[:, :, None, :] + xx[:, :, None, :] * mix
    xr, xw, xk, xv, xa, xg = (xs[:, :, i] for i in range(N_SHIFT_MIX))
    r = xr @ w_r
    k = xk @ w_k
    v = xv @ w_v
    w_log = -jax.nn.softplus(-(w0 + jnp.tanh(xw @ w1) @ w2)) - 0.5
    a = jax.nn.sigmoid(a0 + (xa @ a1) @ a2)
    g = jax.nn.sigmoid(xg @ g1) @ g2

    def heads(t):
        return t.reshape(nb, t_len, N_HEADS, HEAD_SIZE).astype(jnp.float32)

    r, k, v, a = heads(r), heads(k), heads(v), heads(a)
    decay = jnp.exp(-jnp.exp(heads(w_log)))
    kk = k * k_k.reshape(N_HEADS, HEAD_SIZE).astype(jnp.float32)
    kk = kk / jnp.maximum(jnp.linalg.norm(kk, axis=-1, keepdims=True), NORM_EPS)
    k = k * (1 + (a - 1) * k_a.reshape(N_HEADS, HEAD_SIZE).astype(jnp.float32))

    def step(s, inp):
        r_t, w_t, k_t, v_t, a_t, b_t = inp
        sa = jnp.einsum('bhij,bhj->bhi', s, a_t)
        s = s * w_t[:, :, None, :] + sa[..., None] * b_t[:, :, None, :] + v_t[..., None] * k_t[:, :, None, :]
        return s, jnp.einsum('bhij,bhj->bhi', s, r_t)

    seq_first = lambda t: jnp.moveaxis(t, 1, 0)
    s_fin, y = lax.scan(step, wkv_state.astype(jnp.float32),
                        (seq_first(r), seq_first(decay), seq_first(k), seq_first(v),
                         seq_first(-kk), seq_first(kk * a)))
    y = jnp.moveaxis(y, 0, 1)
    mu = jnp.mean(y, axis=-1, keepdims=True)
    var = jnp.mean(jnp.square(y - mu), axis=-1, keepdims=True)
    y = ((y - mu) * lax.rsqrt(var + GN_EPS)).reshape(nb, t_len, d)
    y = y * ln_w.astype(jnp.float32) + ln_b.astype(jnp.float32)
    bonus = jnp.sum(r * k * r_k.astype(jnp.float32), axis=-1, keepdims=True) * v
    y = y + bonus.reshape(nb, t_len, d)
    out = (y.astype(h.dtype) * g) @ w_o
    return out, h[:, -1], s_fin.astype(wkv_state.dtype)


def trunk(x, c, conv_states, shift_states, wkv_states, *, mod_w, mod_b, norm_pre, norm_post,
          ffn_w_in, ffn_w_out, conv_w_in, conv_w, conv_w_out, rw_mix, rw_w0, rw_w1, rw_w2,
          rw_a0, rw_a1, rw_a2, rw_g1, rw_g2, rw_kk, rw_ka, rw_rk, rw_wr, rw_wk, rw_wv, rw_wo,
          rw_lnw, rw_lnb):
    nb = x.shape[0]
    new_conv, new_shift, new_wkv = [], [], []
    for l in range(DEPTH):
        mod = (jax.nn.silu(c) @ mod_w[l] + mod_b[l]).reshape(nb, 1, N_SUBLAYERS, 3, D_MODEL)
        shift, scale, gate = mod[:, :, :, 0], mod[:, :, :, 1], mod[:, :, :, 2]
        h = modulated_pre(x, norm_pre[l, 0], shift[:, :, 0], scale[:, :, 0])
        x = gated_post(x, swiglu(h, ffn_w_in[l, 0], ffn_w_out[l, 0]), norm_post[l, 0], gate[:, :, 0], HALF_STEP)
        h = modulated_pre(x, norm_pre[l, 1], shift[:, :, 1], scale[:, :, 1])
        j = l // N_MIXERS
        if l % N_MIXERS == 0:
            y, cs = short_conv_mixer(h, conv_states[j], conv_w_in[j], conv_w[j], conv_w_out[j])
            new_conv.append(cs)
        else:
            y, ss, ws = rwkv7_mixer(h, shift_states[j], wkv_states[j], rw_mix[j], rw_w0[j], rw_w1[j],
                                    rw_w2[j], rw_a0[j], rw_a1[j], rw_a2[j], rw_g1[j], rw_g2[j],
                                    rw_kk[j], rw_ka[j], rw_rk[j], rw_wr[j], rw_wk[j], rw_wv[j],
                                    rw_wo[j], rw_lnw[j], rw_lnb[j])
            new_shift.append(ss)
            new_wkv.append(ws)
        x = gated_post(x, y, norm_post[l, 1], gate[:, :, 1], 1.0)
        h = modulated_pre(x, norm_pre[l, 2], shift[:, :, 2], scale[:, :, 2])
        x = gated_post(x, swiglu(h, ffn_w_in[l, 1], ffn_w_out[l, 1]), norm_post[l, 2], gate[:, :, 2], HALF_STEP)
    return x, jnp.stack(new_conv), jnp.stack(new_shift), jnp.stack(new_wkv)


def setup_inputs(seed: int = 0) -> dict:
    key = jax.random.key(seed)
    ks = iter(jax.random.split(key, 48))

    def nrm(shape, std):
        return jax.random.normal(next(ks), shape, jnp.float32) * std

    def uni(shape, lo, hi):
        return jax.random.uniform(next(ks), shape, jnp.float32, lo, hi)

    d = D_MODEL
    nc, nr = N_CONV_LAYERS, N_RWKV_LAYERS
    return {
        'x_prompt': nrm((BATCH, SEQ, d), 1.0),
        'x_sample': nrm((DEC_BATCH, DEC_SEQ, d), 1.0),
        'state_conv': nrm((nc, DEC_BATCH, CONV_WIDTH - 1, D_CONV), 1.0),
        'state_shift': nrm((nr, DEC_BATCH, d), 1.0),
        'state_wkv': nrm((nr, DEC_BATCH, N_HEADS, HEAD_SIZE, HEAD_SIZE), 0.5),
        'c_prompt': nrm((BATCH, d), 1.0),
        'c_sample': nrm((DEC_BATCH, d), 1.0),
        'mod_w': nrm((DEPTH, d, N_SUBLAYERS * 3 * d), 0.5 * d ** -0.5),
        'mod_b': nrm((DEPTH, N_SUBLAYERS * 3 * d), 0.02),
        'norm_pre': 1.0 + nrm((DEPTH, N_SUBLAYERS, d), 0.1),
        'norm_post': 1.0 + nrm((DEPTH, N_SUBLAYERS, d), 0.1),
        'ffn_w_in': nrm((DEPTH, 2, d, 2 * D_FF), d ** -0.5),
        'ffn_w_out': nrm((DEPTH, 2, D_FF, d), D_FF ** -0.5),
        'conv_w_in': nrm((nc, d, 3 * D_CONV), d ** -0.5),
        'conv_w': nrm((nc, CONV_WIDTH, D_CONV), CONV_WIDTH ** -0.5),
        'conv_w_out': nrm((nc, D_CONV, d), D_CONV ** -0.5),
        'rw_mix': uni((nr, N_SHIFT_MIX, d), 0.0, 1.0),
        'rw_w0': uni((nr, d), -6.0, 1.0),
        'rw_w1': nrm((nr, d, D_DECAY_LORA), d ** -0.5),
        'rw_w2': nrm((nr, D_DECAY_LORA, d), 0.1 * D_DECAY_LORA ** -0.5),
        'rw_a0': nrm((nr, d), 0.1),
        'rw_a1': nrm((nr, d, D_AAA_LORA), d ** -0.5),
        'rw_a2': nrm((nr, D_AAA_LORA, d), 0.1 * D_AAA_LORA ** -0.5),
        'rw_g1': nrm((nr, d, D_GATE_LORA), d ** -0.5),
        'rw_g2': nrm((nr, D_GATE_LORA, d), D_GATE_LORA ** -0.5),
        'rw_kk': 0.85 + nrm((nr, d), 0.1),
        'rw_ka': 1.0 + nrm((nr, d), 0.1),
        'rw_rk': nrm((nr, N_HEADS, HEAD_SIZE), 0.1),
        'rw_wr': nrm((nr, d, d), d ** -0.5),
        'rw_wk': nrm((nr, d, d), d ** -0.5),
        'rw_wv': nrm((nr, d, d), d ** -0.5),
        'rw_wo': nrm((nr, d, d), d ** -0.5),
        'rw_lnw': 1.0 + nrm((nr, d), 0.1),
        'rw_lnb': nrm((nr, d), 0.02),
    }


def reference(x_prompt, x_sample, state_conv, state_shift, state_wkv, c_prompt, c_sample,
              mod_w, mod_b, norm_pre, norm_post, ffn_w_in, ffn_w_out, conv_w_in, conv_w, conv_w_out,
              rw_mix, rw_w0, rw_w1, rw_w2, rw_a0, rw_a1, rw_a2, rw_g1, rw_g2, rw_kk, rw_ka, rw_rk,
              rw_wr, rw_wk, rw_wv, rw_wo, rw_lnw, rw_lnb):
    run = functools.partial(
        trunk, mod_w=mod_w, mod_b=mod_b, norm_pre=norm_pre, norm_post=norm_post,
        ffn_w_in=ffn_w_in, ffn_w_out=ffn_w_out, conv_w_in=conv_w_in, conv_w=conv_w,
        conv_w_out=conv_w_out, rw_mix=rw_mix, rw_w0=rw_w0, rw_w1=rw_w1, rw_w2=rw_w2,
        rw_a0=rw_a0, rw_a1=rw_a1, rw_a2=rw_a2, rw_g1=rw_g1, rw_g2=rw_g2, rw_kk=rw_kk,
        rw_ka=rw_ka, rw_rk=rw_rk, rw_wr=rw_wr, rw_wk=rw_wk, rw_wv=rw_wv, rw_wo=rw_wo,
        rw_lnw=rw_lnw, rw_lnb=rw_lnb)
    nb = x_prompt.shape[0]
    conv0 = jnp.zeros((N_CONV_LAYERS, nb, CONV_WIDTH - 1, D_CONV), state_conv.dtype)
    shift0 = jnp.zeros((N_RWKV_LAYERS, nb, D_MODEL), state_shift.dtype)
    wkv0 = jnp.zeros((N_RWKV_LAYERS, nb, N_HEADS, HEAD_SIZE, HEAD_SIZE), state_wkv.dtype)
    y_prompt, conv_p, shift_p, wkv_p = run(x_prompt, c_prompt, conv0, shift0, wkv0)
    y_sample, conv_s, shift_s, wkv_s = run(x_sample, c_sample, state_conv, state_shift, state_wkv)
    return (y_prompt, y_sample, conv_p, shift_p, wkv_p, conv_s, shift_s, wkv_s)
```

```python
import functools
import math

import jax
import jax.numpy as jnp
from jax import lax
from jax.experimental import pallas as pl
from jax.experimental.pallas import tpu as pltpu

F32 = jnp.float32
BF16 = jnp.bfloat16

RMS_EPS = 1e-6
GN_EPS = 64e-5
NORM_EPS = 1e-12
HALF_STEP = 0.5
HEAD_SIZE = 64
N_MOD = 9

MXU_WIDTH_V7X = 256
GROUP_LANES = MXU_WIDTH_V7X
HEADS_PER_GROUP = GROUP_LANES // HEAD_SIZE
VMEM_LIMIT_V7X = 60 * 2**20
ROW_PIECE = 256

NN = (((1,), (0,)), ((), ()))
NT = (((1,), (1,)), ((), ()))
TN = (((0,), (0,)), ((), ()))


def _mm(a, b, dims=NN):
    return lax.dot_general(a.astype(BF16), b.astype(BF16), dims, preferred_element_type=F32)


def _sigmoid(x):
    return 1.0 / (1.0 + jnp.exp(-x))


def _params(n_axes):
    return pltpu.CompilerParams(dimension_semantics=("arbitrary",) * n_axes,
                                vmem_limit_bytes=VMEM_LIMIT_V7X)


def _rms(x, g):
    return x * lax.rsqrt(jnp.mean(x * x, axis=-1, keepdims=True) + RMS_EPS) * g


def _modulated_pre(x, m_ref, g):
    return _rms(x, g) * (1.0 + m_ref[1]) + m_ref[0]


def _gated_post(x, y, m_ref, g, res_w):
    return x + (res_w * m_ref[2]) * _rms(y, g)


def _pieces(lead, rows):
    if lead == 1:
        n = min(ROW_PIECE, rows)
        return [(slice(0, 1), slice(r0, r0 + n)) for r0 in range(0, rows, n)]
    n = min(lead, max(1, ROW_PIECE // rows))
    return [(slice(a0, a0 + n), slice(0, rows)) for a0 in range(0, lead, n)]


def _piece_rows(x_ref, ls, rs):
    shp = x_ref[ls, rs, :].shape
    return shp, shp[0] * shp[1]


def _pre_to_scratch(x_ref, m_ref, g, h_ref):
    lead, rows, d = x_ref.shape
    r0 = 0
    for ls, rs in _pieces(lead, rows):
        _, n = _piece_rows(x_ref, ls, rs)
        h = _modulated_pre(x_ref[ls, rs, :], m_ref, g)
        h_ref[r0:r0 + n, :] = h.reshape(n, d).astype(BF16)
        r0 += n


def _accumulate(o_ref, lhs, w, j):
    lead, rows, _ = o_ref.shape
    r0 = 0
    for ls, rs in _pieces(lead, rows):
        shp, n = _piece_rows(o_ref, ls, rs)
        y = _mm(lhs[r0:r0 + n, :], w).reshape(shp)
        r0 += n

        @pl.when(j == 0)
        def _():
            o_ref[ls, rs, :] = y

        @pl.when(j > 0)
        def _():
            o_ref[ls, rs, :] += y


def _post_in_place(x_ref, o_ref, m_ref, g, res_w):
    lead, rows, _ = o_ref.shape
    for ls, rs in _pieces(lead, rows):
        o_ref[ls, rs, :] = _gated_post(x_ref[ls, rs, :], o_ref[ls, rs, :], m_ref, g, res_w)


def _shift_time(u, k, fill, time_major):
    lead, rows, n = u.shape
    if time_major:
        return jnp.concatenate([fill, u[:lead - k]], axis=0)
    out = pltpu.roll(u.reshape(rows, n), k, 0)
    row = lax.broadcasted_iota(jnp.int32, (rows, n), 0)
    for t in range(k):
        out = jnp.where(row == t, fill[t:t + 1, :], out)
    return out.reshape(1, rows, n)


class _Layout:
    def __init__(self, shape, time_major, rows_per_tile):
        self.time_major = time_major
        self.shape = shape
        lead, rows, d = shape
        if time_major:
            self.block = (lead, rows_per_tile, d)
            self.tiles_per_seq = 1
            self.n_tiles = rows // rows_per_tile
        else:
            self.block = (1, rows_per_tile, d)
            self.tiles_per_seq = rows // rows_per_tile
            self.n_tiles = lead * self.tiles_per_seq
        self.tile_rows = self.block[0] * self.block[1]

    def xmap(self, i, *_):
        if self.time_major:
            return (0, i, 0)
        return (i // self.tiles_per_seq, i % self.tiles_per_seq, 0)

    def x_spec(self, **kw):
        return pl.BlockSpec(self.block, self.xmap, **kw)

    def mod_spec(self, mod, l, sub):
        d = self.shape[2]
        if self.time_major:
            return pl.BlockSpec((None, 3, self.block[1], d), lambda i, *_: (l, sub, i, 0))
        tps = self.tiles_per_seq
        return pl.BlockSpec((None, 3, None, 1, d), lambda i, *_: (l, sub, i // tps, 0, 0))


def _gain_spec(gains, l, sub):
    return pl.BlockSpec((None, None, 1, gains.shape[3]), lambda i, *_: (l, sub, 0, 0))


def _mod_kernel(c_ref, w_ref, b_ref, o_ref):
    c = c_ref[...]
    o_ref[...] = _mm(c * _sigmoid(c), w_ref[...]) + b_ref[...]


def _mod_call(c_all, mod_w, mod_b):
    depth, d, n = mod_w.shape
    nbp = c_all.shape[0]
    tn = math.gcd(n, 1024)
    return pl.pallas_call(
        _mod_kernel,
        grid=(depth, n // tn),
        in_specs=[
            pl.BlockSpec((nbp, d), lambda l, j: (0, 0)),
            pl.BlockSpec((None, d, tn), lambda l, j: (l, 0, j)),
            pl.BlockSpec((None, 1, tn), lambda l, j: (l, 0, j)),
        ],
        out_specs=pl.BlockSpec((None, nbp, tn), lambda l, j: (l, 0, j)),
        out_shape=jax.ShapeDtypeStruct((depth, nbp, n), F32),
        compiler_params=_params(2),
    )(c_all, mod_w, mod_b.reshape(depth, 1, n))


def _ffn_kernel(x_ref, m_ref, gpre_ref, gpost_ref, wg_ref, wu_ref, wo_ref, o_ref, h_ref):
    j = pl.program_id(1)
    lead, rows, _ = x_ref.shape

    @pl.when(j == 0)
    def _():
        _pre_to_scratch(x_ref, m_ref, gpre_ref[...], h_ref)

    wg = wg_ref[...].astype(BF16)
    wu = wu_ref[...].astype(BF16)
    wo = wo_ref[...].astype(BF16)
    r0 = 0
    for ls, rs in _pieces(lead, rows):
        shp, n = _piece_rows(o_ref, ls, rs)
        h = h_ref[r0:r0 + n, :]
        gt = _mm(h, wg)
        up = _mm(h, wu)
        y = _mm(gt * _sigmoid(gt) * up, wo).reshape(shp)
        r0 += n

        @pl.when(j == 0)
        def _():
            o_ref[ls, rs, :] = y

        @pl.when(j > 0)
        def _():
            o_ref[ls, rs, :] += y

    @pl.when(j == pl.num_programs(1) - 1)
    def _():
        _post_in_place(x_ref, o_ref, m_ref, gpost_ref[...], HALF_STEP)


def _ffn_call(x3, lay, mod, norm_pre, norm_post, w_in, w_out, l, s, sub, tf):
    d = x3.shape[2]
    nj = w_out.shape[2] // tf
    return pl.pallas_call(
        _ffn_kernel,
        grid=(lay.n_tiles, nj),
        in_specs=[
            lay.x_spec(pipeline_mode=pl.Buffered(1)),
            lay.mod_spec(mod, l, sub),
            _gain_spec(norm_pre, l, sub),
            _gain_spec(norm_post, l, sub),
            pl.BlockSpec((None, None, d, tf), lambda i, j: (l, s, 0, j)),
            pl.BlockSpec((None, None, d, tf), lambda i, j: (l, s, 0, nj + j)),
            pl.BlockSpec((None, None, tf, d), lambda i, j: (l, s, j, 0)),
        ],
        out_specs=lay.x_spec(),
        out_shape=jax.ShapeDtypeStruct(x3.shape, F32),
        scratch_shapes=[pltpu.VMEM((lay.tile_rows, d), BF16)],
        compiler_params=_params(2),
    )(x3, mod, norm_pre, norm_post, w_in, w_in, w_out)


def _conv_kernel(*refs, time_major, tiles_per_seq):
    if time_major:
        (x_ref, m_ref, gpre_ref, gpost_ref, wb_ref, wc_ref, wx_ref, cw_ref, wo_ref, st_ref,
         o_ref, so_ref, h_ref) = refs
    else:
        (x_ref, m_ref, gpre_ref, gpost_ref, wb_ref, wc_ref, wx_ref, cw_ref, wo_ref,
         o_ref, so_ref, h_ref, carry_ref) = refs
    i = pl.program_id(0)
    j = pl.program_id(1)
    lead, rows, _ = x_ref.shape
    tm = lead * rows

    @pl.when(j == 0)
    def _():
        _pre_to_scratch(x_ref, m_ref, gpre_ref[...], h_ref)

    h = h_ref[...]
    bg = _mm(h, wb_ref[...])
    u2 = _mm(h, wc_ref[...]) * _mm(h, wx_ref[...])
    tn = u2.shape[1]
    u = u2.reshape(lead, rows, tn)
    if time_major:
        st = st_ref[...]
        fill1, fill2 = st[1:2], st
        so_ref[...] = u[lead - 2:lead]
    else:
        @pl.when(i % tiles_per_seq == 0)
        def _():
            carry_ref[j] = jnp.zeros((8, tn), F32)
        prev = carry_ref[j]
        fill1, fill2 = prev[7:8, :], prev[6:8, :]
        carry_ref[j] = u2[tm - 8:tm, :]
        so_ref[...] = u[:, rows - 2:rows, :]
    cw = cw_ref[...]
    z = (cw[0:1, :] * _shift_time(u, 2, fill2, time_major)
         + cw[1:2, :] * _shift_time(u, 1, fill1, time_major) + cw[2:3, :] * u)
    bz = (bg * z.reshape(tm, tn)).astype(BF16)
    _accumulate(o_ref, bz, wo_ref[...].astype(BF16), j)

    @pl.when(j == pl.num_programs(1) - 1)
    def _():
        _post_in_place(x_ref, o_ref, m_ref, gpost_ref[...], 1.0)


def _conv_call(x3, lay, mod, norm_pre, norm_post, w_in, cw, w_out, state, l, jl, sub, tn):
    d = x3.shape[2]
    dc = w_out.shape[1]
    nj = dc // tn
    in_specs = [
        lay.x_spec(pipeline_mode=pl.Buffered(1)),
        lay.mod_spec(mod, l, sub),
        _gain_spec(norm_pre, l, sub),
        _gain_spec(norm_post, l, sub),
        pl.BlockSpec((None, d, tn), lambda i, j: (jl, 0, j)),
        pl.BlockSpec((None, d, tn), lambda i, j: (jl, 0, nj + j)),
        pl.BlockSpec((None, d, tn), lambda i, j: (jl, 0, 2 * nj + j)),
        pl.BlockSpec((None, cw.shape[1], tn), lambda i, j: (jl, 0, j)),
        pl.BlockSpec((None, tn, d), lambda i, j: (jl, j, 0)),
    ]
    args = [x3, mod, norm_pre, norm_post, w_in, w_in, w_in, cw, w_out]
    scratch = [pltpu.VMEM((lay.tile_rows, d), BF16)]
    if lay.time_major:
        nb = lay.block[1]
        in_specs.append(pl.BlockSpec((2, nb, tn), lambda i, j: (0, i, j)))
        args.append(state)
        so_spec = pl.BlockSpec((2, nb, tn), lambda i, j: (0, i, j))
        so_shape = jax.ShapeDtypeStruct((2, x3.shape[1], dc), F32)
    else:
        scratch.append(pltpu.VMEM((nj, 8, tn), F32))
        so_spec = pl.BlockSpec((1, 2, tn), lambda i, j: (i, 0, j))
        so_shape = jax.ShapeDtypeStruct((lay.n_tiles, 2, dc), F32)
    return pl.pallas_call(
        functools.partial(_conv_kernel, time_major=lay.time_major,
                          tiles_per_seq=lay.tiles_per_seq),
        grid=(lay.n_tiles, nj),
        in_specs=in_specs,
        out_specs=[lay.x_spec(), so_spec],
        out_shape=[jax.ShapeDtypeStruct(x3.shape, F32), so_shape],
        scratch_shapes=scratch,
        compiler_params=_params(2),
    )(*args)


def _rw_prep_kernel(*refs, time_major, tiles_per_seq):
    if time_major:
        (x_ref, m_ref, gpre_ref, mix_ref, w1_ref, a1_ref, g1_ref, st_ref,
         xr_ref, xk_ref, xv_ref, tw_ref, ta_ref, tg_ref, so_ref) = refs
    else:
        (x_ref, m_ref, gpre_ref, mix_ref, w1_ref, a1_ref, g1_ref,
         xr_ref, xk_ref, xv_ref, tw_ref, ta_ref, tg_ref, so_ref, carry_ref) = refs
    i = pl.program_id(0)
    lead, rows, d = x_ref.shape
    n = lead * rows
    h = _modulated_pre(x_ref[...], m_ref, gpre_ref[...])
    if time_major:
        fill = st_ref[...][None]
        so_ref[...] = h[lead - 1]
    else:
        @pl.when(i % tiles_per_seq == 0)
        def _():
            carry_ref[...] = jnp.zeros((8, d), F32)
        fill = carry_ref[7:8, :]
        carry_ref[...] = h[0, rows - 8:rows, :]
        so_ref[...] = h[:, rows - 1:rows, :]
    xx = _shift_time(h, 1, fill, time_major) - h
    mix = mix_ref[...]

    def mixed(k):
        return h + xx * mix[k:k + 1, :]

    def low_rank(k, w_ref):
        return _mm(mixed(k).reshape(n, d), w_ref[...]).reshape(lead, rows, w_ref.shape[1])

    xr_ref[...] = mixed(0).astype(BF16)
    tw_ref[...] = jnp.tanh(low_rank(1, w1_ref))
    xk_ref[...] = mixed(2).astype(BF16)
    xv_ref[...] = mixed(3).astype(BF16)
    ta_ref[...] = low_rank(4, a1_ref)
    tg_ref[...] = _sigmoid(low_rank(5, g1_ref))


def _rw_prep_call(x3, lay, mod, norm_pre, mix, w1, a1, g1, state, l, jl, sub):
    lead, rows, d = x3.shape
    dl, dg = w1.shape[2], g1.shape[2]
    in_specs = [
        lay.x_spec(),
        lay.mod_spec(mod, l, sub),
        _gain_spec(norm_pre, l, sub),
        pl.BlockSpec((None, mix.shape[1], d), lambda i: (jl, 0, 0)),
        pl.BlockSpec((None, d, dl), lambda i: (jl, 0, 0)),
        pl.BlockSpec((None, d, dl), lambda i: (jl, 0, 0)),
        pl.BlockSpec((None, d, dg), lambda i: (jl, 0, 0)),
    ]
    args = [x3, mod, norm_pre, mix, w1, a1, g1]
    scratch = []
    blk = lay.block
    if lay.time_major:
        in_specs.append(pl.BlockSpec((blk[1], d), lambda i: (i, 0)))
        args.append(state)
        so_spec = pl.BlockSpec((blk[1], d), lambda i: (i, 0))
        so_shape = jax.ShapeDtypeStruct((rows, d), F32)
    else:
        scratch.append(pltpu.VMEM((8, d), F32))
        tps = lay.tiles_per_seq
        so_spec = pl.BlockSpec((1, 1, d), lambda i: (i // tps, 0, 0))
        so_shape = jax.ShapeDtypeStruct((lead, 1, d), F32)

    def ospec(width):
        return pl.BlockSpec((blk[0], blk[1], width), lay.xmap)

    return pl.pallas_call(
        functools.partial(_rw_prep_kernel, time_major=lay.time_major,
                          tiles_per_seq=lay.tiles_per_seq),
        grid=(lay.n_tiles,),
        in_specs=in_specs,
        out_specs=[ospec(d)] * 3 + [ospec(dl), ospec(dl), ospec(dg), so_spec],
        out_shape=[jax.ShapeDtypeStruct((lead, rows, d), BF16)] * 3
        + [jax.ShapeDtypeStruct((lead, rows, dl), F32)] * 2
        + [jax.ShapeDtypeStruct((lead, rows, dg), F32), so_shape],
        scratch_shapes=scratch,
        compiler_params=_params(1),
    )(*args)


def _rw_proj_kernel(xr_ref, xk_ref, xv_ref, tw_ref, ta_ref, tg_ref, wr_ref, wk_ref, wv_ref,
                    w2_ref, a2_ref, g2_ref, w0_ref, a0_ref,
                    r_ref, k_ref, v_ref, w_ref, a_ref, g_ref):
    r_ref[...] = _mm(xr_ref[...], wr_ref[...])
    k_ref[...] = _mm(xk_ref[...], wk_ref[...])
    v_ref[...] = _mm(xv_ref[...], wv_ref[...])
    w_ref[...] = w0_ref[...] + _mm(tw_ref[...], w2_ref[...])
    a_ref[...] = _sigmoid(a0_ref[...] + _mm(ta_ref[...], a2_ref[...]))
    g_ref[...] = _mm(tg_ref[...], g2_ref[...])


def _rw_proj_call(xr, xk, xv, tw, ta, tg, wr, wk, wv, w2, a2, g2, w0, a0, jl, tm, tn):
    m, d = xr.shape
    dl, dg = tw.shape[1], tg.shape[1]
    nr = w0.shape[0]
    xspec = pl.BlockSpec((tm, d), lambda i, j: (i, 0))
    wspec = pl.BlockSpec((None, d, tn), lambda i, j: (jl, 0, j))
    vspec = pl.BlockSpec((None, 1, tn), lambda i, j: (jl, 0, j))
    ospec = pl.BlockSpec((tm, tn), lambda i, j: (i, j))
    return pl.pallas_call(
        _rw_proj_kernel,
        grid=(m // tm, d // tn),
        in_specs=[xspec, xspec, xspec,
                  pl.BlockSpec((tm, dl), lambda i, j: (i, 0)),
                  pl.BlockSpec((tm, dl), lambda i, j: (i, 0)),
                  pl.BlockSpec((tm, dg), lambda i, j: (i, 0)),
                  wspec, wspec, wspec,
                  pl.BlockSpec((None, dl, tn), lambda i, j: (jl, 0, j)),
                  pl.BlockSpec((None, dl, tn), lambda i, j: (jl, 0, j)),
                  pl.BlockSpec((None, dg, tn), lambda i, j: (jl, 0, j)),
                  vspec, vspec],
        out_specs=[ospec] * 6,
        out_shape=[jax.ShapeDtypeStruct((m, d), F32)] * 6,
        compiler_params=_params(2),
    )(xr, xk, xv, tw, ta, tg, wr, wk, wv, w2, a2, g2,
      w0.reshape(nr, 1, d), a0.reshape(nr, 1, d))


def _iota2(shape, axis):
    return lax.broadcasted_iota(jnp.int32, shape, axis)


def _same_block(shape, row_block, lane_block):
    r = lax.shift_right_logical(_iota2(shape, 0), int(math.log2(row_block)))
    c = lax.shift_right_logical(_iota2(shape, 1), int(math.log2(lane_block)))
    return r == c


def _bd_rows(x, mask):
    return jnp.where(mask, jnp.concatenate([x] * HEADS_PER_GROUP, axis=0), 0.0)


def _split_mm(x, ones):
    hi = x.astype(BF16)
    lo = (x - hi.astype(F32)).astype(BF16)
    return (lax.dot_general(hi, ones, NN, preferred_element_type=F32)
            + lax.dot_general(lo, ones, NN, preferred_element_type=F32))


def _wkv_chunk(r, k, v, a_, b_, lw, cum, s_bd, c):
    hs = HEAD_SIZE
    tlanes = HEADS_PER_GROUP * c
    m_ch = _same_block((tlanes, GROUP_LANES), c, hs)
    m_tt = _same_block((tlanes, tlanes), c, c)
    m_ss = _same_block((GROUP_LANES, GROUP_LANES), hs, hs)
    t_row = _iota2((c, tlanes), 0)
    s_lane = jnp.bitwise_and(_iota2((c, tlanes), 1), c - 1)
    strict = s_lane < t_row
    incl = s_lane <= t_row

    cl = cum[c - 1:c, :]
    e_in = jnp.exp(cum)
    e_ex = jnp.exp(cum - lw)
    e_neg = jnp.exp(-cum)
    e_rem = jnp.exp(cl - cum)
    at = a_ * e_ex
    rt = r * e_in
    bt = b_ * e_neg
    kt = k * e_neg
    bh = b_ * e_rem
    kh = k * e_rem

    lhs = jnp.concatenate([at, rt], axis=0)
    ob = _mm(lhs, _bd_rows(bt, m_ch), NT)
    ok = _mm(lhs, _bd_rows(kt, m_ch), NT)
    a_ab = jnp.where(strict, ob[:c], 0.0)
    a_rb = jnp.where(incl, ob[c:], 0.0)
    a_ak = jnp.where(strict, ok[:c], 0.0)
    a_rk = jnp.where(incl, ok[c:], 0.0)

    tinv = jnp.where(s_lane == t_row, 1.0, 0.0) + a_ab
    npow = a_ab
    for _ in range(int(math.log2(c)) - 1):
        npow = _mm(npow, _bd_rows(npow, m_tt))
        tinv = tinv + _mm(tinv, _bd_rows(npow, m_tt))

    av = _mm(a_ak, _bd_rows(v, m_ch))
    wt = _mm(tinv, _bd_rows(at, m_ch))
    w = _mm(tinv, _bd_rows(av, m_ch))
    qt = rt + _mm(a_rb, _bd_rows(wt, m_ch))
    yi = _mm(a_rb, _bd_rows(w, m_ch)) + _mm(a_rk, _bd_rows(v, m_ch))
    g_off = jnp.where(m_ss, _mm(wt, bh, TN), 0.0)
    h_t = jnp.where(m_ss, _mm(w, bh, TN) + _mm(v, kh, TN), 0.0)

    y = _mm(qt, s_bd, NT) + yi
    s_new = s_bd * jnp.exp(cl) + _mm(s_bd, g_off) + h_t
    return y, s_new


def _scan_kernel(*refs, c, has_state, n_groups):
    if has_state:
        (r_ref, k_ref, v_ref, w_ref, a_ref, g_ref, p_ref, s0_ref, y_ref, so_ref, sbd_ref) = refs
    else:
        (r_ref, k_ref, v_ref, w_ref, a_ref, g_ref, p_ref, y_ref, so_ref, sbd_ref) = refs
    ci = pl.program_id(1)
    hs = HEAD_SIZE
    m_ss = _same_block((GROUP_LANES, GROUP_LANES), hs, hs)
    ones_bd = jnp.where(m_ss, 1.0, 0.0).astype(BF16)

    @pl.when(ci == 0)
    def _():
        for g in range(n_groups):
            if has_state:
                sbd_ref[g] = _bd_rows(s0_ref[0, g], m_ss)
            else:
                sbd_ref[g] = jnp.zeros((GROUP_LANES, GROUP_LANES), F32)

    z = -w_ref[...]
    softplus = jnp.maximum(z, 0.0) + jnp.log(1.0 + jnp.exp(-jnp.abs(z)))
    lw = -jnp.exp(-softplus - 0.5)
    tri = jnp.where(_iota2((c, c), 0) >= _iota2((c, c), 1), 1.0, 0.0)
    cum = lax.dot_general(tri, lw, NN, precision=lax.Precision.HIGHEST,
                          preferred_element_type=F32)

    for g in range(n_groups):
        sl = slice(g * GROUP_LANES, (g + 1) * GROUP_LANES)
        r = r_ref[:, sl]
        k = k_ref[:, sl]
        v = v_ref[:, sl]
        a_sig = a_ref[:, sl]
        kk = k * p_ref[0:1, sl]
        norm = jnp.sqrt(_split_mm(kk * kk, ones_bd))
        kk = kk / jnp.maximum(norm, NORM_EPS)
        k = k * (1.0 + (a_sig - 1.0) * p_ref[1:2, sl])
        y, s_new = _wkv_chunk(r, k, v, -kk, kk * a_sig, lw[:, sl], cum[:, sl], sbd_ref[g], c)
        sbd_ref[g] = s_new

        mu = _split_mm(y, ones_bd) * (1.0 / hs)
        dy = y - mu
        var = _split_mm(dy * dy, ones_bd) * (1.0 / hs)
        yn = dy * lax.rsqrt(var + GN_EPS) * p_ref[3:4, sl] + p_ref[4:5, sl]
        bonus = _split_mm(r * k * p_ref[2:3, sl], ones_bd) * v
        y_ref[:, sl] = (yn + bonus) * g_ref[:, sl]

        @pl.when(ci == pl.num_programs(1) - 1)
        def _():
            so_ref[0, g] = (s_new[0:hs] + s_new[hs:2 * hs]
                            + s_new[2 * hs:3 * hs] + s_new[3 * hs:4 * hs])


def _scan_call(r, k, v, w, a, g, p, s0, nb, t, c, time_major):
    m, d = r.shape
    nc = t // c
    ng = d // GROUP_LANES
    has_state = s0 is not None
    if time_major:
        r, k, v, w, a, g = (z.reshape(t, nb * d) for z in (r, k, v, w, a, g))
        xspec = pl.BlockSpec((c, d), lambda b, ci: (ci, b))
    else:
        xspec = pl.BlockSpec((c, d), lambda b, ci: (b * nc + ci, 0))
    sspec = pl.BlockSpec((1, ng, HEAD_SIZE, GROUP_LANES), lambda b, ci: (b, 0, 0, 0))
    in_specs = [xspec] * 6 + [pl.BlockSpec(p.shape, lambda b, ci: (0, 0))]
    args = [r, k, v, w, a, g, p]
    if has_state:
        in_specs.append(sspec)
        args.append(s0)
    y, s_fin = pl.pallas_call(
        functools.partial(_scan_kernel, c=c, has_state=has_state, n_groups=ng),
        grid=(nb, nc),
        in_specs=in_specs,
        out_specs=[xspec, sspec],
        out_shape=[jax.ShapeDtypeStruct(r.shape, F32),
                   jax.ShapeDtypeStruct((nb, ng, HEAD_SIZE, GROUP_LANES), F32)],
        scratch_shapes=[pltpu.VMEM((ng, GROUP_LANES, GROUP_LANES), F32)],
        compiler_params=_params(2),
    )(*args)
    return y.reshape(m, d), s_fin


def _rw_out_kernel(x_ref, m_ref, gpost_ref, y_ref, wo_ref, o_ref):
    j = pl.program_id(1)
    _accumulate(o_ref, y_ref[...].astype(BF16), wo_ref[...].astype(BF16), j)

    @pl.when(j == pl.num_programs(1) - 1)
    def _():
        _post_in_place(x_ref, o_ref, m_ref, gpost_ref[...], 1.0)


def _rw_out_call(x3, lay, mod, norm_post, yg, w_o, l, jl, sub, tk):
    d = x3.shape[2]
    return pl.pallas_call(
        _rw_out_kernel,
        grid=(lay.n_tiles, d // tk),
        in_specs=[
            lay.x_spec(pipeline_mode=pl.Buffered(1)),
            lay.mod_spec(mod, l, sub),
            _gain_spec(norm_post, l, sub),
            pl.BlockSpec((lay.tile_rows, tk), lambda i, j: (i, j)),
            pl.BlockSpec((None, tk, d), lambda i, j: (jl, j, 0)),
        ],
        out_specs=lay.x_spec(),
        out_shape=jax.ShapeDtypeStruct(x3.shape, F32),
        compiler_params=_params(2),
    )(x3, mod, norm_post, yg, w_o)


def _wkv_to_groups(s):
    nb, h, n, _ = s.shape
    g = h // HEADS_PER_GROUP
    return (s.reshape(nb, g, HEADS_PER_GROUP, n, n).transpose(0, 1, 3, 2, 4)
            .reshape(nb, g, n, HEADS_PER_GROUP * n))


def _wkv_from_groups(s):
    nb, g, n, _ = s.shape
    return (s.reshape(nb, g, n, HEADS_PER_GROUP, n).transpose(0, 1, 3, 2, 4)
            .reshape(nb, g * HEADS_PER_GROUP, n, n))


def _trunk(x3, mod, conv_state, shift_state, wkv_state, wts, *, time_major, tile, prep_tile, chunk):
    d = x3.shape[2]
    nb, t = (x3.shape[1], x3.shape[0]) if time_major else (x3.shape[0], x3.shape[1])
    depth = mod.shape[0]
    lay = _Layout(x3.shape, time_major, tile)
    prep_lay = _Layout(x3.shape, time_major, prep_tile)
    m = nb * t
    new_conv, new_shift, new_wkv = [], [], []
    for l in range(depth):
        jl = l // 2
        x3 = _ffn_call(x3, lay, mod, wts["norm_pre"], wts["norm_post"], wts["ffn_w_in"],
                       wts["ffn_w_out"], l, 0, 0, 256)
        if l % 2 == 0:
            st = conv_state[jl].transpose(1, 0, 2) if time_major else None
            x3, cs = _conv_call(x3, lay, mod, wts["norm_pre"], wts["norm_post"],
                                wts["conv_w_in"], wts["conv_w"], wts["conv_w_out"], st,
                                l, jl, 1, 256)
            if time_major:
                new_conv.append(cs.transpose(1, 0, 2))
            else:
                tps = lay.tiles_per_seq
                new_conv.append(cs[tps - 1::tps])
        else:
            st = shift_state[jl] if time_major else None
            xr, xk, xv, tw, ta, tg, ss = _rw_prep_call(
                x3, prep_lay, mod, wts["norm_pre"], wts["rw_mix"], wts["rw_w1"], wts["rw_a1"],
                wts["rw_g1"], st, l, jl, 1)
            flat = lambda z: z.reshape(m, z.shape[2])
            r, k, v, w, a, g = _rw_proj_call(
                flat(xr), flat(xk), flat(xv), flat(tw), flat(ta), flat(tg),
                wts["rw_wr"], wts["rw_wk"], wts["rw_wv"], wts["rw_w2"], wts["rw_a2"],
                wts["rw_g2"], wts["rw_w0"], wts["rw_a0"], jl, lay.tile_rows, 256)
            p = jnp.concatenate([
                wts["rw_kk"][jl][None], wts["rw_ka"][jl][None], wts["rw_rk"][jl].reshape(1, d),
                wts["rw_lnw"][jl][None], wts["rw_lnb"][jl][None],
                jnp.zeros((3, d), F32)], axis=0)
            s0 = _wkv_to_groups(wkv_state[jl]) if time_major else None
            yg, s_fin = _scan_call(r, k, v, w, a, g, p, s0, nb, t, chunk, time_major)
            x3 = _rw_out_call(x3, lay, mod, wts["norm_post"], yg, wts["rw_wo"], l, jl, 1, 512)
            new_shift.append(ss.reshape(nb, d))
            new_wkv.append(_wkv_from_groups(s_fin))
        x3 = _ffn_call(x3, lay, mod, wts["norm_pre"], wts["norm_post"], wts["ffn_w_in"],
                       wts["ffn_w_out"], l, 1, 2, 256)
    return x3, jnp.stack(new_conv), jnp.stack(new_shift), jnp.stack(new_wkv)


def kernel(x_prompt, x_sample, state_conv, state_shift, state_wkv, c_prompt, c_sample, mod_w, mod_b, norm_pre, norm_post, ffn_w_in, ffn_w_out, conv_w_in, conv_w, conv_w_out, rw_mix, rw_w0, rw_w1, rw_w2, rw_a0, rw_a1, rw_a2, rw_g1, rw_g2, rw_kk, rw_ka, rw_rk, rw_wr, rw_wk, rw_wv, rw_wo, rw_lnw, rw_lnb):
    b, t, d = x_prompt.shape
    sb, st, _ = x_sample.shape
    depth = mod_w.shape[0]
    n_sub = norm_pre.shape[1]
    wts = dict(norm_pre=norm_pre.reshape(depth, n_sub, 1, d),
               norm_post=norm_post.reshape(depth, n_sub, 1, d), ffn_w_in=ffn_w_in, ffn_w_out=ffn_w_out,
               conv_w_in=conv_w_in, conv_w=conv_w, conv_w_out=conv_w_out, rw_mix=rw_mix,
               rw_w0=rw_w0, rw_w1=rw_w1, rw_w2=rw_w2, rw_a0=rw_a0, rw_a1=rw_a1, rw_a2=rw_a2,
               rw_g1=rw_g1, rw_g2=rw_g2, rw_kk=rw_kk, rw_ka=rw_ka, rw_rk=rw_rk, rw_wr=rw_wr,
               rw_wk=rw_wk, rw_wv=rw_wv, rw_wo=rw_wo, rw_lnw=rw_lnw, rw_lnb=rw_lnb)

    n_c = b + sb
    pad = (-n_c) % 8
    c_all = jnp.concatenate([c_prompt, c_sample, jnp.zeros((pad, d), F32)], axis=0)
    mod_all = _mod_call(c_all, mod_w, mod_b)
    mod_p = (mod_all[:, :b].reshape(depth, b, N_MOD, d).transpose(0, 2, 1, 3)
             .reshape(depth, N_MOD, b, 1, d))
    mod_s = mod_all[:, b:n_c].reshape(depth, sb, N_MOD, d).transpose(0, 2, 1, 3)

    y_p, conv_p, shift_p, wkv_p = _trunk(
        x_prompt, mod_p, None, None, None, wts, time_major=False,
        tile=min(t, 1024), prep_tile=min(t, 128), chunk=min(t, 64))
    y_s, conv_s, shift_s, wkv_s = _trunk(
        x_sample.transpose(1, 0, 2), mod_s, state_conv, state_shift, state_wkv, wts,
        time_major=True, tile=sb, prep_tile=min(sb, 16), chunk=st)
    return (y_p, y_s.transpose(1, 0, 2), conv_p, shift_p, wkv_p, conv_s, shift_s, wkv_s)
```

```python
import functools
import math

import jax
import jax.numpy as jnp
from jax import lax
from jax.experimental import pallas as pl
from jax.experimental.pallas import tpu as pltpu

F32 = jnp.float32
BF16 = jnp.bfloat16

RMS_EPS = 1e-6
GN_EPS = 64e-5
NORM_EPS = 1e-12
HALF_STEP = 0.5
HEAD_SIZE = 64
N_MOD = 9

MXU_WIDTH_V7X = 256
GROUP_LANES = MXU_WIDTH_V7X
HEADS_PER_GROUP = GROUP_LANES // HEAD_SIZE
VMEM_LIMIT_V7X = 60 * 2**20
ROW_PIECE = 256

NN = (((1,), (0,)), ((), ()))
NT = (((1,), (1,)), ((), ()))
TN = (((0,), (0,)), ((), ()))


def _mm(a, b, dims=NN):
    return lax.dot_general(a.astype(BF16), b.astype(BF16), dims, preferred_element_type=F32)


def _sigmoid(x):
    return 1.0 / (1.0 + jnp.exp(-x))


def _params(n_axes):
    return pltpu.CompilerParams(dimension_semantics=("arbitrary",) * n_axes,
                                vmem_limit_bytes=VMEM_LIMIT_V7X)


def _rms(x, g):
    return x * lax.rsqrt(jnp.mean(x * x, axis=-1, keepdims=True) + RMS_EPS) * g


def _modulated_pre(x, m_ref, g):
    return _rms(x, g) * (1.0 + m_ref[1]) + m_ref[0]


def _gated_post(x, y, m_ref, g, res_w):
    return x + (res_w * m_ref[2]) * _rms(y, g)


def _pieces(lead, rows):
    if lead == 1:
        n = min(ROW_PIECE, rows)
        return [(slice(0, 1), slice(r0, r0 + n)) for r0 in range(0, rows, n)]
    n = min(lead, max(1, ROW_PIECE // rows))
    return [(slice(a0, a0 + n), slice(0, rows)) for a0 in range(0, lead, n)]


def _piece_rows(x_ref, ls, rs):
    shp = x_ref[ls, rs, :].shape
    return shp, shp[0] * shp[1]


def _pre_to_scratch(x_ref, m_ref, g, h_ref):
    lead, rows, d = x_ref.shape
    r0 = 0
    for ls, rs in _pieces(lead, rows):
        _, n = _piece_rows(x_ref, ls, rs)
        h = _modulated_pre(x_ref[ls, rs, :], m_ref, g)
        h_ref[r0:r0 + n, :] = h.reshape(n, d).astype(BF16)
        r0 += n


def _shift_time(u, k, fill, time_major):
    lead, rows, n = u.shape
    if time_major:
        return jnp.concatenate([fill, u[:lead - k]], axis=0)
    out = pltpu.roll(u.reshape(rows, n), k, 0)
    row = lax.broadcasted_iota(jnp.int32, (rows, n), 0)
    for t in range(k):
        out = jnp.where(row == t, fill[t:t + 1, :], out)
    return out.reshape(1, rows, n)


class _Layout:
    def __init__(self, shape, time_major, rows_per_tile):
        self.time_major = time_major
        self.shape = shape
        lead, rows, d = shape
        if time_major:
            self.block = (lead, rows_per_tile, d)
            self.tiles_per_seq = 1
            self.n_tiles = rows // rows_per_tile
        else:
            self.block = (1, rows_per_tile, d)
            self.tiles_per_seq = rows // rows_per_tile
            self.n_tiles = lead * self.tiles_per_seq
        self.tile_rows = self.block[0] * self.block[1]

    def xmap(self, i, *_):
        if self.time_major:
            return (0, i, 0)
        return (i // self.tiles_per_seq, i % self.tiles_per_seq, 0)

    def x_spec(self, **kw):
        return pl.BlockSpec(self.block, self.xmap, **kw)

    def mod_spec(self, mod, l, sub):
        d = self.shape[2]
        if self.time_major:
            return pl.BlockSpec((None, 3, self.block[1], d), lambda i, *_: (l, sub, i, 0))
        tps = self.tiles_per_seq
        return pl.BlockSpec((None, 3, None, 1, d), lambda i, *_: (l, sub, i // tps, 0, 0))


def _gain_spec(gains, l, sub):
    return pl.BlockSpec((None, None, 1, gains.shape[3]), lambda i, *_: (l, sub, 0, 0))


def _mod_kernel(c_ref, w_ref, b_ref, o_ref):
    c = c_ref[...]
    o_ref[...] = _mm(c * _sigmoid(c), w_ref[...]) + b_ref[...]


def _mod_call(c_all, mod_w, mod_b):
    depth, d, n = mod_w.shape
    nbp = c_all.shape[0]
    tn = math.gcd(n, 1024)
    return pl.pallas_call(
        _mod_kernel,
        grid=(depth, n // tn),
        in_specs=[
            pl.BlockSpec((nbp, d), lambda l, j: (0, 0)),
            pl.BlockSpec((None, d, tn), lambda l, j: (l, 0, j)),
            pl.BlockSpec((None, 1, tn), lambda l, j: (l, 0, j)),
        ],
        out_specs=pl.BlockSpec((None, nbp, tn), lambda l, j: (l, 0, j)),
        out_shape=jax.ShapeDtypeStruct((depth, nbp, n), F32),
        compiler_params=_params(2),
    )(c_all, mod_w, mod_b.reshape(depth, 1, n))


def _out_proj_kernel(x_ref, m_ref, gpost_ref, a_ref, w_ref, o_ref, *, res_w):
    lead, rows, _ = x_ref.shape
    for ls, rs in _pieces(lead, rows):
        shp, n = _piece_rows(x_ref, ls, rs)
        a = a_ref[ls, rs, :]
        y = lax.dot_general(a.reshape(n, a.shape[2]), w_ref[...], NN,
                            preferred_element_type=F32).reshape(shp)
        o_ref[ls, rs, :] = _gated_post(x_ref[ls, rs, :], y, m_ref, gpost_ref[...], res_w)


def _out_proj_call(x3, lay, mod, norm_post, act, w, l, widx, sub, res_w):
    k, d = w.shape[2], w.shape[3]
    return pl.pallas_call(
        functools.partial(_out_proj_kernel, res_w=res_w),
        grid=(lay.n_tiles,),
        in_specs=[
            lay.x_spec(),
            lay.mod_spec(mod, l, sub),
            _gain_spec(norm_post, l, sub),
            pl.BlockSpec((lay.block[0], lay.block[1], k), lay.xmap),
            pl.BlockSpec((None, None, k, d), lambda i: (widx[0], widx[1], 0, 0),
                         pipeline_mode=pl.Buffered(1)),
        ],
        out_specs=lay.x_spec(),
        out_shape=jax.ShapeDtypeStruct(x3.shape, F32),
        compiler_params=_params(1),
    )(x3, mod, norm_post, act, w)


def _ffn_in_kernel(x_ref, m_ref, gpre_ref, wg_ref, wu_ref, a_ref, h_ref):
    lead, rows, _ = x_ref.shape

    @pl.when(pl.program_id(1) == 0)
    def _():
        _pre_to_scratch(x_ref, m_ref, gpre_ref[...], h_ref)

    wg = wg_ref[...].astype(BF16)
    wu = wu_ref[...].astype(BF16)
    r0 = 0
    for ls, rs in _pieces(lead, rows):
        shp, n = _piece_rows(x_ref, ls, rs)
        h = h_ref[r0:r0 + n, :]
        gt = _mm(h, wg)
        up = _mm(h, wu)
        a_ref[ls, rs, :] = (gt * _sigmoid(gt) * up).astype(BF16).reshape(shp[0], shp[1], -1)
        r0 += n


def _ffn_in_call(x3, lay, mod, norm_pre, w_in, l, s, sub, tf):
    lead, rows, d = x3.shape
    f = w_in.shape[3] // 2
    nj = f // tf
    return pl.pallas_call(
        _ffn_in_kernel,
        grid=(lay.n_tiles, nj),
        in_specs=[
            lay.x_spec(pipeline_mode=pl.Buffered(1)),
            lay.mod_spec(mod, l, sub),
            _gain_spec(norm_pre, l, sub),
            pl.BlockSpec((None, None, d, tf), lambda i, j: (l, s, 0, j)),
            pl.BlockSpec((None, None, d, tf), lambda i, j: (l, s, 0, nj + j)),
        ],
        out_specs=pl.BlockSpec((lay.block[0], lay.block[1], tf),
                               lambda i, j: lay.xmap(i)[:2] + (j,)),
        out_shape=jax.ShapeDtypeStruct((lead, rows, f), BF16),
        scratch_shapes=[pltpu.VMEM((lay.tile_rows, d), BF16)],
        compiler_params=_params(2),
    )(x3, mod, norm_pre, w_in, w_in)


def _conv_kernel(*refs, time_major, tiles_per_seq):
    if time_major:
        (x_ref, m_ref, gpre_ref, wb_ref, wc_ref, wx_ref, cw_ref, st_ref,
         o_ref, so_ref, h_ref) = refs
    else:
        (x_ref, m_ref, gpre_ref, wb_ref, wc_ref, wx_ref, cw_ref,
         o_ref, so_ref, h_ref, carry_ref) = refs
    i = pl.program_id(0)
    j = pl.program_id(1)
    lead, rows, _ = x_ref.shape
    tm = lead * rows

    @pl.when(j == 0)
    def _():
        _pre_to_scratch(x_ref, m_ref, gpre_ref[...], h_ref)

    h = h_ref[...]
    bg = _mm(h, wb_ref[...])
    u2 = _mm(h, wc_ref[...]) * _mm(h, wx_ref[...])
    tn = u2.shape[1]
    u = u2.reshape(lead, rows, tn)
    if time_major:
        st = st_ref[...]
        fill1, fill2 = st[1:2], st
        so_ref[...] = u[lead - 2:lead]
    else:
        @pl.when(i % tiles_per_seq == 0)
        def _():
            carry_ref[j] = jnp.zeros((8, tn), F32)
        prev = carry_ref[j]
        fill1, fill2 = prev[7:8, :], prev[6:8, :]
        carry_ref[j] = u2[tm - 8:tm, :]
        so_ref[...] = u[:, rows - 2:rows, :]
    cw = cw_ref[...]
    z = (cw[0:1, :] * _shift_time(u, 2, fill2, time_major)
         + cw[1:2, :] * _shift_time(u, 1, fill1, time_major) + cw[2:3, :] * u)
    o_ref[...] = (bg.reshape(lead, rows, tn) * z).astype(BF16)


def _conv_call(x3, lay, mod, norm_pre, w_in, cw, state, l, jl, sub, tn):
    lead_n, rows_n, d = x3.shape
    dc = cw.shape[2]
    nj = dc // tn
    in_specs = [
        lay.x_spec(pipeline_mode=pl.Buffered(1)),
        lay.mod_spec(mod, l, sub),
        _gain_spec(norm_pre, l, sub),
        pl.BlockSpec((None, d, tn), lambda i, j: (jl, 0, j)),
        pl.BlockSpec((None, d, tn), lambda i, j: (jl, 0, nj + j)),
        pl.BlockSpec((None, d, tn), lambda i, j: (jl, 0, 2 * nj + j)),
        pl.BlockSpec((None, cw.shape[1], tn), lambda i, j: (jl, 0, j)),
    ]
    args = [x3, mod, norm_pre, w_in, w_in, w_in, cw]
    scratch = [pltpu.VMEM((lay.tile_rows, d), BF16)]
    if lay.time_major:
        nb = lay.block[1]
        in_specs.append(pl.BlockSpec((2, nb, tn), lambda i, j: (0, i, j)))
        args.append(state)
        so_spec = pl.BlockSpec((2, nb, tn), lambda i, j: (0, i, j))
        so_shape = jax.ShapeDtypeStruct((2, x3.shape[1], dc), F32)
    else:
        scratch.append(pltpu.VMEM((nj, 8, tn), F32))
        so_spec = pl.BlockSpec((1, 2, tn), lambda i, j: (i, 0, j))
        so_shape = jax.ShapeDtypeStruct((lay.n_tiles, 2, dc), F32)
    return pl.pallas_call(
        functools.partial(_conv_kernel, time_major=lay.time_major,
                          tiles_per_seq=lay.tiles_per_seq),
        grid=(lay.n_tiles, nj),
        in_specs=in_specs,
        out_specs=[pl.BlockSpec((lay.block[0], lay.block[1], tn),
                                lambda i, j: lay.xmap(i)[:2] + (j,)), so_spec],
        out_shape=[jax.ShapeDtypeStruct((lead_n, rows_n, dc), BF16), so_shape],
        scratch_shapes=scratch,
        compiler_params=_params(2),
    )(*args)


def _rw_prep_kernel(*refs, time_major, tiles_per_seq):
    if time_major:
        (x_ref, m_ref, gpre_ref, mix_ref, w1_ref, a1_ref, g1_ref, st_ref,
         xr_ref, xk_ref, xv_ref, tw_ref, ta_ref, tg_ref, so_ref) = refs
    else:
        (x_ref, m_ref, gpre_ref, mix_ref, w1_ref, a1_ref, g1_ref,
         xr_ref, xk_ref, xv_ref, tw_ref, ta_ref, tg_ref, so_ref, carry_ref) = refs
    i = pl.program_id(0)
    lead, rows, d = x_ref.shape
    n = lead * rows
    h = _modulated_pre(x_ref[...], m_ref, gpre_ref[...])
    if time_major:
        fill = st_ref[...][None]
        so_ref[...] = h[lead - 1]
    else:
        @pl.when(i % tiles_per_seq == 0)
        def _():
            carry_ref[...] = jnp.zeros((8, d), F32)
        fill = carry_ref[7:8, :]
        carry_ref[...] = h[0, rows - 8:rows, :]
        so_ref[...] = h[:, rows - 1:rows, :]
    xx = _shift_time(h, 1, fill, time_major) - h
    mix = mix_ref[...]

    def mixed(k):
        return h + xx * mix[k:k + 1, :]

    def low_rank(k, w_ref):
        return _mm(mixed(k).reshape(n, d), w_ref[...]).reshape(lead, rows, w_ref.shape[1])

    xr_ref[...] = mixed(0).astype(BF16)
    tw_ref[...] = jnp.tanh(low_rank(1, w1_ref))
    xk_ref[...] = mixed(2).astype(BF16)
    xv_ref[...] = mixed(3).astype(BF16)
    ta_ref[...] = low_rank(4, a1_ref)
    tg_ref[...] = _sigmoid(low_rank(5, g1_ref))


def _rw_prep_call(x3, lay, mod, norm_pre, mix, w1, a1, g1, state, l, jl, sub):
    lead, rows, d = x3.shape
    dl, dg = w1.shape[2], g1.shape[2]
    in_specs = [
        lay.x_spec(),
        lay.mod_spec(mod, l, sub),
        _gain_spec(norm_pre, l, sub),
        pl.BlockSpec((None, mix.shape[1], d), lambda i: (jl, 0, 0)),
        pl.BlockSpec((None, d, dl), lambda i: (jl, 0, 0)),
        pl.BlockSpec((None, d, dl), lambda i: (jl, 0, 0)),
        pl.BlockSpec((None, d, dg), lambda i: (jl, 0, 0)),
    ]
    args = [x3, mod, norm_pre, mix, w1, a1, g1]
    scratch = []
    blk = lay.block
    if lay.time_major:
        in_specs.append(pl.BlockSpec((blk[1], d), lambda i: (i, 0)))
        args.append(state)
        so_spec = pl.BlockSpec((blk[1], d), lambda i: (i, 0))
        so_shape = jax.ShapeDtypeStruct((rows, d), F32)
    else:
        scratch.append(pltpu.VMEM((8, d), F32))
        tps = lay.tiles_per_seq
        so_spec = pl.BlockSpec((1, 1, d), lambda i: (i // tps, 0, 0))
        so_shape = jax.ShapeDtypeStruct((lead, 1, d), F32)

    def ospec(width):
        return pl.BlockSpec((blk[0], blk[1], width), lay.xmap)

    return pl.pallas_call(
        functools.partial(_rw_prep_kernel, time_major=lay.time_major,
                          tiles_per_seq=lay.tiles_per_seq),
        grid=(lay.n_tiles,),
        in_specs=in_specs,
        out_specs=[ospec(d)] * 3 + [ospec(dl), ospec(dl), ospec(dg), so_spec],
        out_shape=[jax.ShapeDtypeStruct((lead, rows, d), BF16)] * 3
        + [jax.ShapeDtypeStruct((lead, rows, dl), F32)] * 2
        + [jax.ShapeDtypeStruct((lead, rows, dg), F32), so_shape],
        scratch_shapes=scratch,
        compiler_params=_params(1),
    )(*args)


def _rw_proj_kernel(xr_ref, xk_ref, xv_ref, tw_ref, ta_ref, tg_ref, wr_ref, wk_ref, wv_ref,
                    w2_ref, a2_ref, g2_ref, w0_ref, a0_ref,
                    r_ref, k_ref, v_ref, w_ref, a_ref, g_ref):
    r_ref[...] = _mm(xr_ref[...], wr_ref[...])
    k_ref[...] = _mm(xk_ref[...], wk_ref[...])
    v_ref[...] = _mm(xv_ref[...], wv_ref[...])
    w_ref[...] = w0_ref[...] + _mm(tw_ref[...], w2_ref[...])
    a_ref[...] = _sigmoid(a0_ref[...] + _mm(ta_ref[...], a2_ref[...]))
    g_ref[...] = _mm(tg_ref[...], g2_ref[...])


def _rw_proj_call(xr, xk, xv, tw, ta, tg, wr, wk, wv, w2, a2, g2, w0, a0, jl, tm, tn):
    m, d = xr.shape
    dl, dg = tw.shape[1], tg.shape[1]
    nr = w0.shape[0]
    xspec = pl.BlockSpec((tm, d), lambda i, j: (i, 0))
    wspec = pl.BlockSpec((None, d, tn), lambda i, j: (jl, 0, j))
    vspec = pl.BlockSpec((None, 1, tn), lambda i, j: (jl, 0, j))
    ospec = pl.BlockSpec((tm, tn), lambda i, j: (i, j))
    return pl.pallas_call(
        _rw_proj_kernel,
        grid=(m // tm, d // tn),
        in_specs=[xspec, xspec, xspec,
                  pl.BlockSpec((tm, dl), lambda i, j: (i, 0)),
                  pl.BlockSpec((tm, dl), lambda i, j: (i, 0)),
                  pl.BlockSpec((tm, dg), lambda i, j: (i, 0)),
                  wspec, wspec, wspec,
                  pl.BlockSpec((None, dl, tn), lambda i, j: (jl, 0, j)),
                  pl.BlockSpec((None, dl, tn), lambda i, j: (jl, 0, j)),
                  pl.BlockSpec((None, dg, tn), lambda i, j: (jl, 0, j)),
                  vspec, vspec],
        out_specs=[ospec] * 6,
        out_shape=[jax.ShapeDtypeStruct((m, d), F32)] * 6,
        compiler_params=_params(2),
    )(xr, xk, xv, tw, ta, tg, wr, wk, wv, w2, a2, g2,
      w0.reshape(nr, 1, d), a0.reshape(nr, 1, d))


def _iota2(shape, axis):
    return lax.broadcasted_iota(jnp.int32, shape, axis)


def _same_block(shape, row_block, lane_block):
    r = lax.shift_right_logical(_iota2(shape, 0), int(math.log2(row_block)))
    c = lax.shift_right_logical(_iota2(shape, 1), int(math.log2(lane_block)))
    return r == c


def _bd_rows(x, mask):
    return jnp.where(mask, jnp.concatenate([x] * HEADS_PER_GROUP, axis=0), 0.0)


def _split_mm(x, ones):
    hi = x.astype(BF16)
    lo = (x - hi.astype(F32)).astype(BF16)
    return (lax.dot_general(hi, ones, NN, preferred_element_type=F32)
            + lax.dot_general(lo, ones, NN, preferred_element_type=F32))


def _wkv_chunk(r, k, v, a_, b_, lw, cum, s_bd, c):
    hs = HEAD_SIZE
    tlanes = HEADS_PER_GROUP * c
    m_ch = _same_block((tlanes, GROUP_LANES), c, hs)
    m_tt = _same_block((tlanes, tlanes), c, c)
    m_ss = _same_block((GROUP_LANES, GROUP_LANES), hs, hs)
    t_row = _iota2((c, tlanes), 0)
    s_lane = jnp.bitwise_and(_iota2((c, tlanes), 1), c - 1)
    strict = s_lane < t_row
    incl = s_lane <= t_row

    cl = cum[c - 1:c, :]
    e_in = jnp.exp(cum)
    e_ex = jnp.exp(cum - lw)
    e_neg = jnp.exp(-cum)
    e_rem = jnp.exp(cl - cum)
    at = a_ * e_ex
    rt = r * e_in
    bt = b_ * e_neg
    kt = k * e_neg
    bh = b_ * e_rem
    kh = k * e_rem

    lhs = jnp.concatenate([at, rt], axis=0)
    ob = _mm(lhs, _bd_rows(bt, m_ch), NT)
    ok = _mm(lhs, _bd_rows(kt, m_ch), NT)
    a_ab = jnp.where(strict, ob[:c], 0.0)
    a_rb = jnp.where(incl, ob[c:], 0.0)
    a_ak = jnp.where(strict, ok[:c], 0.0)
    a_rk = jnp.where(incl, ok[c:], 0.0)

    tinv = jnp.where(s_lane == t_row, 1.0, 0.0) + a_ab
    npow = a_ab
    for _ in range(int(math.log2(c)) - 1):
        npow = _mm(npow, _bd_rows(npow, m_tt))
        tinv = tinv + _mm(tinv, _bd_rows(npow, m_tt))

    av = _mm(a_ak, _bd_rows(v, m_ch))
    wt = _mm(tinv, _bd_rows(at, m_ch))
    w = _mm(tinv, _bd_rows(av, m_ch))
    qt = rt + _mm(a_rb, _bd_rows(wt, m_ch))
    yi = _mm(a_rb, _bd_rows(w, m_ch)) + _mm(a_rk, _bd_rows(v, m_ch))
    g_off = jnp.where(m_ss, _mm(wt, bh, TN), 0.0)
    h_t = jnp.where(m_ss, _mm(w, bh, TN) + _mm(v, kh, TN), 0.0)

    y = _mm(qt, s_bd, NT) + yi
    s_new = s_bd * jnp.exp(cl) + _mm(s_bd, g_off) + h_t
    return y, s_new


def _scan_kernel(*refs, c, has_state, n_groups):
    if has_state:
        (r_ref, k_ref, v_ref, w_ref, a_ref, g_ref, p_ref, s0_ref, y_ref, so_ref, sbd_ref) = refs
    else:
        (r_ref, k_ref, v_ref, w_ref, a_ref, g_ref, p_ref, y_ref, so_ref, sbd_ref) = refs
    ci = pl.program_id(1)
    hs = HEAD_SIZE
    m_ss = _same_block((GROUP_LANES, GROUP_LANES), hs, hs)
    ones_bd = jnp.where(m_ss, 1.0, 0.0).astype(BF16)

    @pl.when(ci == 0)
    def _():
        for g in range(n_groups):
            if has_state:
                sbd_ref[g] = _bd_rows(s0_ref[0, g], m_ss)
            else:
                sbd_ref[g] = jnp.zeros((GROUP_LANES, GROUP_LANES), F32)

    z = -w_ref[...]
    softplus = jnp.maximum(z, 0.0) + jnp.log(1.0 + jnp.exp(-jnp.abs(z)))
    lw = -jnp.exp(-softplus - 0.5)
    tri = jnp.where(_iota2((c, c), 0) >= _iota2((c, c), 1), 1.0, 0.0)
    cum = lax.dot_general(tri, lw, NN, precision=lax.Precision.HIGHEST,
                          preferred_element_type=F32)

    for g in range(n_groups):
        sl = slice(g * GROUP_LANES, (g + 1) * GROUP_LANES)
        r = r_ref[:, sl]
        k = k_ref[:, sl]
        v = v_ref[:, sl]
        a_sig = a_ref[:, sl]
        kk = k * p_ref[0:1, sl]
        norm = jnp.sqrt(_split_mm(kk * kk, ones_bd))
        kk = kk / jnp.maximum(norm, NORM_EPS)
        k = k * (1.0 + (a_sig - 1.0) * p_ref[1:2, sl])
        y, s_new = _wkv_chunk(r, k, v, -kk, kk * a_sig, lw[:, sl], cum[:, sl], sbd_ref[g], c)
        sbd_ref[g] = s_new

        mu = _split_mm(y, ones_bd) * (1.0 / hs)
        dy = y - mu
        var = _split_mm(dy * dy, ones_bd) * (1.0 / hs)
        yn = dy * lax.rsqrt(var + GN_EPS) * p_ref[3:4, sl] + p_ref[4:5, sl]
        bonus = _split_mm(r * k * p_ref[2:3, sl], ones_bd) * v
        y_ref[:, sl] = ((yn + bonus) * g_ref[:, sl]).astype(BF16)

        @pl.when(ci == pl.num_programs(1) - 1)
        def _():
            so_ref[0, g] = (s_new[0:hs] + s_new[hs:2 * hs]
                            + s_new[2 * hs:3 * hs] + s_new[3 * hs:4 * hs])


def _scan_call(r, k, v, w, a, g, p, s0, nb, t, c, time_major):
    m, d = r.shape
    nc = t // c
    ng = d // GROUP_LANES
    has_state = s0 is not None
    if time_major:
        r, k, v, w, a, g = (z.reshape(t, nb * d) for z in (r, k, v, w, a, g))
        xspec = pl.BlockSpec((c, d), lambda b, ci: (ci, b))
    else:
        xspec = pl.BlockSpec((c, d), lambda b, ci: (b * nc + ci, 0))
    sspec = pl.BlockSpec((1, ng, HEAD_SIZE, GROUP_LANES), lambda b, ci: (b, 0, 0, 0))
    in_specs = [xspec] * 6 + [pl.BlockSpec(p.shape, lambda b, ci: (0, 0))]
    args = [r, k, v, w, a, g, p]
    if has_state:
        in_specs.append(sspec)
        args.append(s0)
    y, s_fin = pl.pallas_call(
        functools.partial(_scan_kernel, c=c, has_state=has_state, n_groups=ng),
        grid=(nb, nc),
        in_specs=in_specs,
        out_specs=[xspec, sspec],
        out_shape=[jax.ShapeDtypeStruct(r.shape, BF16),
                   jax.ShapeDtypeStruct((nb, ng, HEAD_SIZE, GROUP_LANES), F32)],
        scratch_shapes=[pltpu.VMEM((ng, GROUP_LANES, GROUP_LANES), F32)],
        compiler_params=_params(2),
    )(*args)
    return y.reshape(m, d), s_fin


def _wkv_to_groups(s):
    nb, h, n, _ = s.shape
    g = h // HEADS_PER_GROUP
    return (s.reshape(nb, g, HEADS_PER_GROUP, n, n).transpose(0, 1, 3, 2, 4)
            .reshape(nb, g, n, HEADS_PER_GROUP * n))


def _wkv_from_groups(s):
    nb, g, n, _ = s.shape
    return (s.reshape(nb, g, n, HEADS_PER_GROUP, n).transpose(0, 1, 3, 2, 4)
            .reshape(nb, g * HEADS_PER_GROUP, n, n))


def _trunk(x3, mod, conv_state, shift_state, wkv_state, wts, *, time_major, tile, out_tile,
           prep_tile, chunk):
    d = x3.shape[2]
    nb, t = (x3.shape[1], x3.shape[0]) if time_major else (x3.shape[0], x3.shape[1])
    depth = mod.shape[0]
    lay = _Layout(x3.shape, time_major, tile)
    out_lay = _Layout(x3.shape, time_major, out_tile)
    prep_lay = _Layout(x3.shape, time_major, prep_tile)
    m = nb * t
    new_conv, new_shift, new_wkv = [], [], []

    def ffn(x3, l, s, sub):
        act = _ffn_in_call(x3, lay, mod, wts["norm_pre"], wts["ffn_w_in"], l, s, sub, 512)
        return _out_proj_call(x3, out_lay, mod, wts["norm_post"], act, wts["ffn_w_out"],
                              l, (l, s), sub, HALF_STEP)

    for l in range(depth):
        jl = l // 2
        x3 = ffn(x3, l, 0, 0)
        if l % 2 == 0:
            st = conv_state[jl].transpose(1, 0, 2) if time_major else None
            bz, cs = _conv_call(x3, lay, mod, wts["norm_pre"], wts["conv_w_in"], wts["conv_w"],
                                st, l, jl, 1, 256)
            x3 = _out_proj_call(x3, out_lay, mod, wts["norm_post"], bz, wts["conv_w_out"],
                                l, (jl, 0), 1, 1.0)
            if time_major:
                new_conv.append(cs.transpose(1, 0, 2))
            else:
                tps = lay.tiles_per_seq
                new_conv.append(cs[tps - 1::tps])
        else:
            st = shift_state[jl] if time_major else None
            xr, xk, xv, tw, ta, tg, ss = _rw_prep_call(
                x3, prep_lay, mod, wts["norm_pre"], wts["rw_mix"], wts["rw_w1"], wts["rw_a1"],
                wts["rw_g1"], st, l, jl, 1)
            flat = lambda z: z.reshape(m, z.shape[2])
            r, k, v, w, a, g = _rw_proj_call(
                flat(xr), flat(xk), flat(xv), flat(tw), flat(ta), flat(tg),
                wts["rw_wr"], wts["rw_wk"], wts["rw_wv"], wts["rw_w2"], wts["rw_a2"],
                wts["rw_g2"], wts["rw_w0"], wts["rw_a0"], jl, lay.tile_rows, 256)
            p = jnp.concatenate([
                wts["rw_kk"][jl][None], wts["rw_ka"][jl][None], wts["rw_rk"][jl].reshape(1, d),
                wts["rw_lnw"][jl][None], wts["rw_lnb"][jl][None],
                jnp.zeros((3, d), F32)], axis=0)
            s0 = _wkv_to_groups(wkv_state[jl]) if time_major else None
            yg, s_fin = _scan_call(r, k, v, w, a, g, p, s0, nb, t, chunk, time_major)
            x3 = _out_proj_call(x3, out_lay, mod, wts["norm_post"], yg.reshape(x3.shape),
                                wts["rw_wo"], l, (jl, 0), 1, 1.0)
            new_shift.append(ss.reshape(nb, d))
            new_wkv.append(_wkv_from_groups(s_fin))
        x3 = ffn(x3, l, 1, 2)
    return x3, jnp.stack(new_conv), jnp.stack(new_shift), jnp.stack(new_wkv)


def kernel(x_prompt, x_sample, state_conv, state_shift, state_wkv, c_prompt, c_sample, mod_w, mod_b, norm_pre, norm_post, ffn_w_in, ffn_w_out, conv_w_in, conv_w, conv_w_out, rw_mix, rw_w0, rw_w1, rw_w2, rw_a0, rw_a1, rw_a2, rw_g1, rw_g2, rw_kk, rw_ka, rw_rk, rw_wr, rw_wk, rw_wv, rw_wo, rw_lnw, rw_lnb):
    b, t, d = x_prompt.shape
    sb, st, _ = x_sample.shape
    depth = mod_w.shape[0]
    n_sub = norm_pre.shape[1]
    wts = dict(norm_pre=norm_pre.reshape(depth, n_sub, 1, d),
               norm_post=norm_post.reshape(depth, n_sub, 1, d), ffn_w_in=ffn_w_in,
               ffn_w_out=ffn_w_out.astype(BF16),
               conv_w_out=conv_w_out.astype(BF16)[:, None],
               rw_wo=rw_wo.astype(BF16)[:, None],
               conv_w_in=conv_w_in, conv_w=conv_w, rw_mix=rw_mix,
               rw_w0=rw_w0, rw_w1=rw_w1, rw_w2=rw_w2, rw_a0=rw_a0, rw_a1=rw_a1, rw_a2=rw_a2,
               rw_g1=rw_g1, rw_g2=rw_g2, rw_kk=rw_kk, rw_ka=rw_ka, rw_rk=rw_rk, rw_wr=rw_wr,
               rw_wk=rw_wk, rw_wv=rw_wv, rw_lnw=rw_lnw, rw_lnb=rw_lnb)

    n_c = b + sb
    pad = (-n_c) % 8
    c_all = jnp.concatenate([c_prompt, c_sample, jnp.zeros((pad, d), F32)], axis=0)
    mod_all = _mod_call(c_all, mod_w, mod_b)
    mod_p = (mod_all[:, :b].reshape(depth, b, N_MOD, d).transpose(0, 2, 1, 3)
             .reshape(depth, N_MOD, b, 1, d))
    mod_s = mod_all[:, b:n_c].reshape(depth, sb, N_MOD, d).transpose(0, 2, 1, 3)

    y_p, conv_p, shift_p, wkv_p = _trunk(
        x_prompt, mod_p, None, None, None, wts, time_major=False,
        tile=min(t, 1024), out_tile=min(t, 512), prep_tile=min(t, 128), chunk=min(t, 64))
    y_s, conv_s, shift_s, wkv_s = _trunk(
        x_sample.transpose(1, 0, 2), mod_s, state_conv, state_shift, state_wkv, wts,
        time_major=True, tile=sb, out_tile=min(sb, 64), prep_tile=min(sb, 16), chunk=st)
    return (y_p, y_s.transpose(1, 0, 2), conv_p, shift_p, wkv_p, conv_s, shift_s, wkv_s)
```

```python
import functools
import math

import jax
import jax.numpy as jnp
from jax import lax
from jax.experimental import pallas as pl
from jax.experimental.pallas import tpu as pltpu

F32 = jnp.float32
BF16 = jnp.bfloat16

RMS_EPS = 1e-6
GN_EPS = 64e-5
NORM_EPS = 1e-12
HALF_STEP = 0.5
HEAD_SIZE = 64
N_MOD = 9

MXU_WIDTH_V7X = 256
GROUP_LANES = MXU_WIDTH_V7X
HEADS_PER_GROUP = GROUP_LANES // HEAD_SIZE
VMEM_LIMIT_V7X = 60 * 2**20
ROW_PIECE = 256

NN = (((1,), (0,)), ((), ()))
NT = (((1,), (1,)), ((), ()))
TN = (((0,), (0,)), ((), ()))


def _mm(a, b, dims=NN):
    return lax.dot_general(a.astype(BF16), b.astype(BF16), dims, preferred_element_type=F32)


def _sigmoid(x):
    return 1.0 / (1.0 + jnp.exp(-x))


def _params(n_axes):
    return pltpu.CompilerParams(dimension_semantics=("arbitrary",) * n_axes,
                                vmem_limit_bytes=VMEM_LIMIT_V7X)


def _rms(x, g):
    return x * lax.rsqrt(jnp.mean(x * x, axis=-1, keepdims=True) + RMS_EPS) * g


def _modulated_pre(x, m_ref, g):
    return _rms(x, g) * (1.0 + m_ref[1]) + m_ref[0]


def _gated_post(x, y, m_ref, g, res_w):
    return x + (res_w * m_ref[2]) * _rms(y, g)


def _pieces(lead, rows):
    if lead == 1:
        n = min(ROW_PIECE, rows)
        return [(slice(0, 1), slice(r0, r0 + n)) for r0 in range(0, rows, n)]
    n = min(lead, max(1, ROW_PIECE // rows))
    return [(slice(a0, a0 + n), slice(0, rows)) for a0 in range(0, lead, n)]


def _piece_rows(x_ref, ls, rs):
    shp = x_ref[ls, rs, :].shape
    return shp, shp[0] * shp[1]


def _pre_to_scratch(x_ref, m_ref, g, h_ref):
    lead, rows, d = x_ref.shape
    r0 = 0
    for ls, rs in _pieces(lead, rows):
        _, n = _piece_rows(x_ref, ls, rs)
        h = _modulated_pre(x_ref[ls, rs, :], m_ref, g)
        h_ref[r0:r0 + n, :] = h.reshape(n, d).astype(BF16)
        r0 += n


def _shift_time(u, k, fill, time_major):
    lead, rows, n = u.shape
    if time_major:
        return jnp.concatenate([fill, u[:lead - k]], axis=0)
    out = pltpu.roll(u.reshape(rows, n), k, 0)
    row = lax.broadcasted_iota(jnp.int32, (rows, n), 0)
    for t in range(k):
        out = jnp.where(row == t, fill[t:t + 1, :], out)
    return out.reshape(1, rows, n)


class _Layout:
    def __init__(self, shape, time_major, rows_per_tile):
        self.time_major = time_major
        self.shape = shape
        lead, rows, d = shape
        if time_major:
            self.block = (lead, rows_per_tile, d)
            self.tiles_per_seq = 1
            self.n_tiles = rows // rows_per_tile
        else:
            self.block = (1, rows_per_tile, d)
            self.tiles_per_seq = rows // rows_per_tile
            self.n_tiles = lead * self.tiles_per_seq
        self.tile_rows = self.block[0] * self.block[1]

    def xmap(self, i, *_):
        if self.time_major:
            return (0, i, 0)
        return (i // self.tiles_per_seq, i % self.tiles_per_seq, 0)

    def x_spec(self, **kw):
        return pl.BlockSpec(self.block, self.xmap, **kw)

    def mod_spec(self, mod, l, sub):
        d = self.shape[2]
        if self.time_major:
            return pl.BlockSpec((None, 3, self.block[1], d), lambda i, *_: (l, sub, i, 0))
        tps = self.tiles_per_seq
        return pl.BlockSpec((None, 3, None, 1, d), lambda i, *_: (l, sub, i // tps, 0, 0))


def _gain_spec(gains, l, sub):
    return pl.BlockSpec((None, None, 1, gains.shape[3]), lambda i, *_: (l, sub, 0, 0))


def _mod_kernel(c_ref, w_ref, b_ref, o_ref):
    c = c_ref[...]
    o_ref[...] = _mm(c * _sigmoid(c), w_ref[...]) + b_ref[...]


def _mod_call(c_all, mod_w, mod_b):
    depth, d, n = mod_w.shape
    nbp = c_all.shape[0]
    tn = math.gcd(n, 1024)
    return pl.pallas_call(
        _mod_kernel,
        grid=(depth, n // tn),
        in_specs=[
            pl.BlockSpec((nbp, d), lambda l, j: (0, 0)),
            pl.BlockSpec((None, d, tn), lambda l, j: (l, 0, j)),
            pl.BlockSpec((None, 1, tn), lambda l, j: (l, 0, j)),
        ],
        out_specs=pl.BlockSpec((None, nbp, tn), lambda l, j: (l, 0, j)),
        out_shape=jax.ShapeDtypeStruct((depth, nbp, n), F32),
        compiler_params=_params(2),
    )(c_all, mod_w, mod_b.reshape(depth, 1, n))


def _out_proj_kernel(x_ref, m_ref, gpost_ref, a_ref, w_ref, o_ref, *, res_w):
    lead, rows, _ = x_ref.shape
    for ls, rs in _pieces(lead, rows):
        shp, n = _piece_rows(x_ref, ls, rs)
        a = a_ref[ls, rs, :]
        y = lax.dot_general(a.reshape(n, a.shape[2]), w_ref[...], NN,
                            preferred_element_type=F32).reshape(shp)
        o_ref[ls, rs, :] = _gated_post(x_ref[ls, rs, :], y, m_ref, gpost_ref[...], res_w)


def _out_proj_call(x3, lay, mod, norm_post, act, w, l, widx, sub, res_w):
    k, d = w.shape[2], w.shape[3]
    return pl.pallas_call(
        functools.partial(_out_proj_kernel, res_w=res_w),
        grid=(lay.n_tiles,),
        in_specs=[
            lay.x_spec(),
            lay.mod_spec(mod, l, sub),
            _gain_spec(norm_post, l, sub),
            pl.BlockSpec((lay.block[0], lay.block[1], k), lay.xmap),
            pl.BlockSpec((None, None, k, d), lambda i: (widx[0], widx[1], 0, 0),
                         pipeline_mode=pl.Buffered(1)),
        ],
        out_specs=lay.x_spec(),
        out_shape=jax.ShapeDtypeStruct(x3.shape, F32),
        compiler_params=_params(1),
    )(x3, mod, norm_post, act, w)


def _ffn_in_kernel(x_ref, m_ref, gpre_ref, wg_ref, wu_ref, a_ref, h_ref):
    lead, rows, _ = x_ref.shape

    @pl.when(pl.program_id(1) == 0)
    def _():
        _pre_to_scratch(x_ref, m_ref, gpre_ref[...], h_ref)

    wg = wg_ref[...].astype(BF16)
    wu = wu_ref[...].astype(BF16)
    r0 = 0
    for ls, rs in _pieces(lead, rows):
        shp, n = _piece_rows(x_ref, ls, rs)
        h = h_ref[r0:r0 + n, :]
        gt = _mm(h, wg)
        up = _mm(h, wu)
        a_ref[ls, rs, :] = (gt * _sigmoid(gt) * up).astype(BF16).reshape(shp[0], shp[1], -1)
        r0 += n


def _ffn_in_call(x3, lay, mod, norm_pre, w_in, l, s, sub, tf):
    lead, rows, d = x3.shape
    f = w_in.shape[3] // 2
    nj = f // tf
    return pl.pallas_call(
        _ffn_in_kernel,
        grid=(lay.n_tiles, nj),
        in_specs=[
            lay.x_spec(pipeline_mode=pl.Buffered(1)),
            lay.mod_spec(mod, l, sub),
            _gain_spec(norm_pre, l, sub),
            pl.BlockSpec((None, None, d, tf), lambda i, j: (l, s, 0, j)),
            pl.BlockSpec((None, None, d, tf), lambda i, j: (l, s, 0, nj + j)),
        ],
        out_specs=pl.BlockSpec((lay.block[0], lay.block[1], tf),
                               lambda i, j: lay.xmap(i)[:2] + (j,)),
        out_shape=jax.ShapeDtypeStruct((lead, rows, f), BF16),
        scratch_shapes=[pltpu.VMEM((lay.tile_rows, d), BF16)],
        compiler_params=_params(2),
    )(x3, mod, norm_pre, w_in, w_in)


def _conv_kernel(*refs, time_major, tiles_per_seq):
    if time_major:
        (x_ref, m_ref, gpre_ref, wb_ref, wc_ref, wx_ref, cw_ref, st_ref,
         o_ref, so_ref, h_ref) = refs
    else:
        (x_ref, m_ref, gpre_ref, wb_ref, wc_ref, wx_ref, cw_ref,
         o_ref, so_ref, h_ref, carry_ref) = refs
    i = pl.program_id(0)
    j = pl.program_id(1)
    lead, rows, _ = x_ref.shape
    tm = lead * rows

    @pl.when(j == 0)
    def _():
        _pre_to_scratch(x_ref, m_ref, gpre_ref[...], h_ref)

    h = h_ref[...]
    bg = _mm(h, wb_ref[...])
    u2 = _mm(h, wc_ref[...]) * _mm(h, wx_ref[...])
    tn = u2.shape[1]
    u = u2.reshape(lead, rows, tn)
    if time_major:
        st = st_ref[...]
        fill1, fill2 = st[1:2], st
        so_ref[...] = u[lead - 2:lead]
    else:
        @pl.when(i % tiles_per_seq == 0)
        def _():
            carry_ref[j] = jnp.zeros((8, tn), F32)
        prev = carry_ref[j]
        fill1, fill2 = prev[7:8, :], prev[6:8, :]
        carry_ref[j] = u2[tm - 8:tm, :]
        so_ref[...] = u[:, rows - 2:rows, :]
    cw = cw_ref[...]
    z = (cw[0:1, :] * _shift_time(u, 2, fill2, time_major)
         + cw[1:2, :] * _shift_time(u, 1, fill1, time_major) + cw[2:3, :] * u)
    o_ref[...] = (bg.reshape(lead, rows, tn) * z).astype(BF16)


def _conv_call(x3, lay, mod, norm_pre, w_in, cw, state, l, jl, sub, tn):
    lead_n, rows_n, d = x3.shape
    dc = cw.shape[2]
    nj = dc // tn
    in_specs = [
        lay.x_spec(pipeline_mode=pl.Buffered(1)),
        lay.mod_spec(mod, l, sub),
        _gain_spec(norm_pre, l, sub),
        pl.BlockSpec((None, d, tn), lambda i, j: (jl, 0, j)),
        pl.BlockSpec((None, d, tn), lambda i, j: (jl, 0, nj + j)),
        pl.BlockSpec((None, d, tn), lambda i, j: (jl, 0, 2 * nj + j)),
        pl.BlockSpec((None, cw.shape[1], tn), lambda i, j: (jl, 0, j)),
    ]
    args = [x3, mod, norm_pre, w_in, w_in, w_in, cw]
    scratch = [pltpu.VMEM((lay.tile_rows, d), BF16)]
    if lay.time_major:
        nb = lay.block[1]
        in_specs.append(pl.BlockSpec((2, nb, tn), lambda i, j: (0, i, j)))
        args.append(state)
        so_spec = pl.BlockSpec((2, nb, tn), lambda i, j: (0, i, j))
        so_shape = jax.ShapeDtypeStruct((2, x3.shape[1], dc), F32)
    else:
        scratch.append(pltpu.VMEM((nj, 8, tn), F32))
        so_spec = pl.BlockSpec((1, 2, tn), lambda i, j: (i, 0, j))
        so_shape = jax.ShapeDtypeStruct((lay.n_tiles, 2, dc), F32)
    return pl.pallas_call(
        functools.partial(_conv_kernel, time_major=lay.time_major,
                          tiles_per_seq=lay.tiles_per_seq),
        grid=(lay.n_tiles, nj),
        in_specs=in_specs,
        out_specs=[pl.BlockSpec((lay.block[0], lay.block[1], tn),
                                lambda i, j: lay.xmap(i)[:2] + (j,)), so_spec],
        out_shape=[jax.ShapeDtypeStruct((lead_n, rows_n, dc), BF16), so_shape],
        scratch_shapes=scratch,
        compiler_params=_params(2),
    )(*args)


def _rw_prep_kernel(*refs, time_major, tiles_per_seq):
    if time_major:
        (x_ref, m_ref, gpre_ref, mix_ref, w1_ref, a1_ref, g1_ref, st_ref,
         xr_ref, xk_ref, xv_ref, tw_ref, ta_ref, tg_ref, so_ref) = refs
    else:
        (x_ref, m_ref, gpre_ref, mix_ref, w1_ref, a1_ref, g1_ref,
         xr_ref, xk_ref, xv_ref, tw_ref, ta_ref, tg_ref, so_ref, carry_ref) = refs
    i = pl.program_id(0)
    lead, rows, d = x_ref.shape
    n = lead * rows
    h = _modulated_pre(x_ref[...], m_ref, gpre_ref[...])
    if time_major:
        fill = st_ref[...][None]
        so_ref[...] = h[lead - 1]
    else:
        @pl.when(i % tiles_per_seq == 0)
        def _():
            carry_ref[...] = jnp.zeros((8, d), F32)
        fill = carry_ref[7:8, :]
        carry_ref[...] = h[0, rows - 8:rows, :]
        so_ref[...] = h[:, rows - 1:rows, :]
    xx = _shift_time(h, 1, fill, time_major) - h
    mix = mix_ref[...]

    def mixed(k):
        return h + xx * mix[k:k + 1, :]

    def low_rank(k, w_ref):
        return _mm(mixed(k).reshape(n, d), w_ref[...]).reshape(lead, rows, w_ref.shape[1])

    xr_ref[...] = mixed(0).astype(BF16)
    tw_ref[...] = jnp.tanh(low_rank(1, w1_ref))
    xk_ref[...] = mixed(2).astype(BF16)
    xv_ref[...] = mixed(3).astype(BF16)
    ta_ref[...] = low_rank(4, a1_ref)
    tg_ref[...] = _sigmoid(low_rank(5, g1_ref))


def _rw_prep_call(x3, lay, mod, norm_pre, mix, w1, a1, g1, state, l, jl, sub):
    lead, rows, d = x3.shape
    dl, dg = w1.shape[2], g1.shape[2]
    in_specs = [
        lay.x_spec(),
        lay.mod_spec(mod, l, sub),
        _gain_spec(norm_pre, l, sub),
        pl.BlockSpec((None, mix.shape[1], d), lambda i: (jl, 0, 0)),
        pl.BlockSpec((None, d, dl), lambda i: (jl, 0, 0)),
        pl.BlockSpec((None, d, dl), lambda i: (jl, 0, 0)),
        pl.BlockSpec((None, d, dg), lambda i: (jl, 0, 0)),
    ]
    args = [x3, mod, norm_pre, mix, w1, a1, g1]
    scratch = []
    blk = lay.block
    if lay.time_major:
        in_specs.append(pl.BlockSpec((blk[1], d), lambda i: (i, 0)))
        args.append(state)
        so_spec = pl.BlockSpec((blk[1], d), lambda i: (i, 0))
        so_shape = jax.ShapeDtypeStruct((rows, d), F32)
    else:
        scratch.append(pltpu.VMEM((8, d), F32))
        tps = lay.tiles_per_seq
        so_spec = pl.BlockSpec((1, 1, d), lambda i: (i // tps, 0, 0))
        so_shape = jax.ShapeDtypeStruct((lead, 1, d), F32)

    def ospec(width):
        return pl.BlockSpec((blk[0], blk[1], width), lay.xmap)

    return pl.pallas_call(
        functools.partial(_rw_prep_kernel, time_major=lay.time_major,
                          tiles_per_seq=lay.tiles_per_seq),
        grid=(lay.n_tiles,),
        in_specs=in_specs,
        out_specs=[ospec(d)] * 3 + [ospec(dl), ospec(dl), ospec(dg), so_spec],
        out_shape=[jax.ShapeDtypeStruct((lead, rows, d), BF16)] * 3
        + [jax.ShapeDtypeStruct((lead, rows, dl), F32)] * 2
        + [jax.ShapeDtypeStruct((lead, rows, dg), F32), so_shape],
        scratch_shapes=scratch,
        compiler_params=_params(1),
    )(*args)


def _rw_proj_kernel(xr_ref, xk_ref, xv_ref, tw_ref, ta_ref, tg_ref, wr_ref, wk_ref, wv_ref,
                    w2_ref, a2_ref, g2_ref, w0_ref, a0_ref,
                    r_ref, k_ref, v_ref, w_ref, a_ref, g_ref):
    r_ref[...] = _mm(xr_ref[...], wr_ref[...])
    k_ref[...] = _mm(xk_ref[...], wk_ref[...])
    v_ref[...] = _mm(xv_ref[...], wv_ref[...])
    w_ref[...] = w0_ref[...] + _mm(tw_ref[...], w2_ref[...])
    a_ref[...] = _sigmoid(a0_ref[...] + _mm(ta_ref[...], a2_ref[...]))
    g_ref[...] = _mm(tg_ref[...], g2_ref[...])


def _rw_proj_call(xr, xk, xv, tw, ta, tg, wr, wk, wv, w2, a2, g2, w0, a0, jl, tm, tn):
    m, d = xr.shape
    dl, dg = tw.shape[1], tg.shape[1]
    nr = w0.shape[0]
    xspec = pl.BlockSpec((tm, d), lambda i, j: (i, 0))
    wspec = pl.BlockSpec((None, d, tn), lambda i, j: (jl, 0, j))
    vspec = pl.BlockSpec((None, 1, tn), lambda i, j: (jl, 0, j))
    ospec = pl.BlockSpec((tm, tn), lambda i, j: (i, j))
    return pl.pallas_call(
        _rw_proj_kernel,
        grid=(m // tm, d // tn),
        in_specs=[xspec, xspec, xspec,
                  pl.BlockSpec((tm, dl), lambda i, j: (i, 0)),
                  pl.BlockSpec((tm, dl), lambda i, j: (i, 0)),
                  pl.BlockSpec((tm, dg), lambda i, j: (i, 0)),
                  wspec, wspec, wspec,
                  pl.BlockSpec((None, dl, tn), lambda i, j: (jl, 0, j)),
                  pl.BlockSpec((None, dl, tn), lambda i, j: (jl, 0, j)),
                  pl.BlockSpec((None, dg, tn), lambda i, j: (jl, 0, j)),
                  vspec, vspec],
        out_specs=[ospec] * 6,
        out_shape=[jax.ShapeDtypeStruct((m, d), F32)] * 6,
        compiler_params=_params(2),
    )(xr, xk, xv, tw, ta, tg, wr, wk, wv, w2, a2, g2,
      w0.reshape(nr, 1, d), a0.reshape(nr, 1, d))


def _iota2(shape, axis):
    return lax.broadcasted_iota(jnp.int32, shape, axis)


def _same_block(shape, row_block, lane_block):
    r = lax.shift_right_logical(_iota2(shape, 0), int(math.log2(row_block)))
    c = lax.shift_right_logical(_iota2(shape, 1), int(math.log2(lane_block)))
    return r == c


def _bd_rows(x, mask):
    return jnp.where(mask, jnp.concatenate([x] * HEADS_PER_GROUP, axis=0), 0.0)


def _split_mm(x, ones):
    hi = x.astype(BF16)
    lo = (x - hi.astype(F32)).astype(BF16)
    return (lax.dot_general(hi, ones, NN, preferred_element_type=F32)
            + lax.dot_general(lo, ones, NN, preferred_element_type=F32))


def _each(fn, *lists):
    return [fn(*xs) for xs in zip(*lists)]


def _wkv_chunk(r, k, v, a_, b_, lw, cum, s_bd, c):
    hs = HEAD_SIZE
    tlanes = HEADS_PER_GROUP * c
    m_ch = _same_block((tlanes, GROUP_LANES), c, hs)
    m_tt = _same_block((tlanes, tlanes), c, c)
    m_ss = _same_block((GROUP_LANES, GROUP_LANES), hs, hs)
    t_row = _iota2((c, tlanes), 0)
    s_lane = jnp.bitwise_and(_iota2((c, tlanes), 1), c - 1)
    strict = s_lane < t_row
    incl = s_lane <= t_row

    def bd_ch(x):
        return _bd_rows(x, m_ch)

    def bd_tt(x):
        return _bd_rows(x, m_tt)

    def apply(a, x):
        return _mm(a, bd_ch(x))

    cl = _each(lambda x: x[c - 1:c, :], cum)
    at = _each(lambda a, x, l: a * jnp.exp(x - l), a_, cum, lw)
    rt = _each(lambda a, x: a * jnp.exp(x), r, cum)
    bt = _each(lambda a, x: a * jnp.exp(-x), b_, cum)
    kt = _each(lambda a, x: a * jnp.exp(-x), k, cum)
    bh = _each(lambda a, x, xl: a * jnp.exp(xl - x), b_, cum, cl)
    kh = _each(lambda a, x, xl: a * jnp.exp(xl - x), k, cum, cl)

    lhs = _each(lambda a, b: jnp.concatenate([a, b], axis=0), at, rt)
    ob = _each(lambda a, x: _mm(a, bd_ch(x), NT), lhs, bt)
    ok = _each(lambda a, x: _mm(a, bd_ch(x), NT), lhs, kt)
    a_ab = _each(lambda x: jnp.where(strict, x[:c], 0.0), ob)
    a_rb = _each(lambda x: jnp.where(incl, x[c:], 0.0), ob)
    a_ak = _each(lambda x: jnp.where(strict, x[:c], 0.0), ok)
    a_rk = _each(lambda x: jnp.where(incl, x[c:], 0.0), ok)
    av = _each(apply, a_ak, v)
    y_kv = _each(apply, a_rk, v)

    eye = jnp.where(s_lane == t_row, 1.0, 0.0)
    tinv = _each(lambda x: eye + x, a_ab)
    npow = a_ab
    for _ in range(int(math.log2(c)) - 1):
        npow = _each(lambda x: _mm(x, bd_tt(x)), npow)
        tinv = _each(lambda t, x: t + _mm(t, bd_tt(x)), tinv, npow)

    wt = _each(apply, tinv, at)
    w = _each(apply, tinv, av)
    qt = _each(lambda x, a, y: x + apply(a, y), rt, a_rb, wt)
    yi = _each(lambda a, x, y: apply(a, x) + y, a_rb, w, y_kv)
    g_off = _each(lambda x, y: jnp.where(m_ss, _mm(x, y, TN), 0.0), wt, bh)
    h_t = _each(lambda x, y, p, q: jnp.where(m_ss, _mm(x, y, TN) + _mm(p, q, TN), 0.0),
                w, bh, v, kh)

    y = _each(lambda q, s, x: _mm(q, s, NT) + x, qt, s_bd, yi)
    s_new = _each(lambda s, xl, g, h: s * jnp.exp(xl) + _mm(s, g) + h, s_bd, cl, g_off, h_t)
    return y, s_new


def _scan_kernel(*refs, c, has_state, n_groups):
    if has_state:
        (r_ref, k_ref, v_ref, w_ref, a_ref, g_ref, p_ref, s0_ref, y_ref, so_ref, sbd_ref) = refs
    else:
        (r_ref, k_ref, v_ref, w_ref, a_ref, g_ref, p_ref, y_ref, so_ref, sbd_ref) = refs
    ci = pl.program_id(1)
    hs = HEAD_SIZE
    m_ss = _same_block((GROUP_LANES, GROUP_LANES), hs, hs)
    ones_bd = jnp.where(m_ss, 1.0, 0.0).astype(BF16)

    @pl.when(ci == 0)
    def _():
        for g in range(n_groups):
            if has_state:
                sbd_ref[g] = _bd_rows(s0_ref[0, g], m_ss)
            else:
                sbd_ref[g] = jnp.zeros((GROUP_LANES, GROUP_LANES), F32)

    z = -w_ref[...]
    softplus = jnp.maximum(z, 0.0) + jnp.log(1.0 + jnp.exp(-jnp.abs(z)))
    lw = -jnp.exp(-softplus - 0.5)
    tri = jnp.where(_iota2((c, c), 0) >= _iota2((c, c), 1), 1.0, 0.0)
    cum = lax.dot_general(tri, lw, NN, precision=lax.Precision.HIGHEST,
                          preferred_element_type=F32)

    sls = [slice(g * GROUP_LANES, (g + 1) * GROUP_LANES) for g in range(n_groups)]

    def seg_sum(x):
        return _split_mm(x, ones_bd)

    r = [r_ref[:, sl] for sl in sls]
    k = [k_ref[:, sl] for sl in sls]
    v = [v_ref[:, sl] for sl in sls]
    a_sig = [a_ref[:, sl] for sl in sls]
    kk = [x * p_ref[0:1, sl] for x, sl in zip(k, sls)]
    norm = _each(lambda x: jnp.sqrt(seg_sum(x * x)), kk)
    kk = _each(lambda x, n: x / jnp.maximum(n, NORM_EPS), kk, norm)
    k = [x * (1.0 + (a - 1.0) * p_ref[1:2, sl]) for x, a, sl in zip(k, a_sig, sls)]
    bonus = [seg_sum(x * y * p_ref[2:3, sl]) * z for x, y, z, sl in zip(r, k, v, sls)]
    y, s_new = _wkv_chunk(r, k, v, _each(lambda x: -x, kk), _each(lambda x, a: x * a, kk, a_sig),
                          [lw[:, sl] for sl in sls], [cum[:, sl] for sl in sls],
                          [sbd_ref[g] for g in range(n_groups)], c)
    for g in range(n_groups):
        sbd_ref[g] = s_new[g]

    mu = _each(lambda x: seg_sum(x) * (1.0 / hs), y)
    dy = _each(lambda x, m: x - m, y, mu)
    var = _each(lambda x: seg_sum(x * x) * (1.0 / hs), dy)
    for g, sl in enumerate(sls):
        yn = dy[g] * lax.rsqrt(var[g] + GN_EPS) * p_ref[3:4, sl] + p_ref[4:5, sl]
        y_ref[:, sl] = ((yn + bonus[g]) * g_ref[:, sl]).astype(BF16)

    @pl.when(ci == pl.num_programs(1) - 1)
    def _():
        for g in range(n_groups):
            s = sbd_ref[g]
            so_ref[0, g] = s[0:hs] + s[hs:2 * hs] + s[2 * hs:3 * hs] + s[3 * hs:4 * hs]


def _scan_call(r, k, v, w, a, g, p, s0, nb, t, c, time_major):
    m, d = r.shape
    nc = t // c
    ng = d // GROUP_LANES
    has_state = s0 is not None
    if time_major:
        r, k, v, w, a, g = (z.reshape(t, nb * d) for z in (r, k, v, w, a, g))
        xspec = pl.BlockSpec((c, d), lambda b, ci: (ci, b))
    else:
        xspec = pl.BlockSpec((c, d), lambda b, ci: (b * nc + ci, 0))
    sspec = pl.BlockSpec((1, ng, HEAD_SIZE, GROUP_LANES), lambda b, ci: (b, 0, 0, 0))
    in_specs = [xspec] * 6 + [pl.BlockSpec(p.shape, lambda b, ci: (0, 0))]
    args = [r, k, v, w, a, g, p]
    if has_state:
        in_specs.append(sspec)
        args.append(s0)
    y, s_fin = pl.pallas_call(
        functools.partial(_scan_kernel, c=c, has_state=has_state, n_groups=ng),
        grid=(nb, nc),
        in_specs=in_specs,
        out_specs=[xspec, sspec],
        out_shape=[jax.ShapeDtypeStruct(r.shape, BF16),
                   jax.ShapeDtypeStruct((nb, ng, HEAD_SIZE, GROUP_LANES), F32)],
        scratch_shapes=[pltpu.VMEM((ng, GROUP_LANES, GROUP_LANES), F32)],
        compiler_params=_params(2),
    )(*args)
    return y.reshape(m, d), s_fin


def _wkv_to_groups(s):
    nb, h, n, _ = s.shape
    g = h // HEADS_PER_GROUP
    return (s.reshape(nb, g, HEADS_PER_GROUP, n, n).transpose(0, 1, 3, 2, 4)
            .reshape(nb, g, n, HEADS_PER_GROUP * n))


def _wkv_from_groups(s):
    nb, g, n, _ = s.shape
    return (s.reshape(nb, g, n, HEADS_PER_GROUP, n).transpose(0, 1, 3, 2, 4)
            .reshape(nb, g * HEADS_PER_GROUP, n, n))


def _trunk(x3, mod, conv_state, shift_state, wkv_state, wts, *, time_major, tile, out_tile,
           prep_tile, chunk):
    d = x3.shape[2]
    nb, t = (x3.shape[1], x3.shape[0]) if time_major else (x3.shape[0], x3.shape[1])
    depth = mod.shape[0]
    lay = _Layout(x3.shape, time_major, tile)
    out_lay = _Layout(x3.shape, time_major, out_tile)
    prep_lay = _Layout(x3.shape, time_major, prep_tile)
    m = nb * t
    new_conv, new_shift, new_wkv = [], [], []

    def ffn(x3, l, s, sub):
        act = _ffn_in_call(x3, lay, mod, wts["norm_pre"], wts["ffn_w_in"], l, s, sub, 512)
        return _out_proj_call(x3, out_lay, mod, wts["norm_post"], act, wts["ffn_w_out"],
                              l, (l, s), sub, HALF_STEP)

    for l in range(depth):
        jl = l // 2
        x3 = ffn(x3, l, 0, 0)
        if l % 2 == 0:
            st = conv_state[jl].transpose(1, 0, 2) if time_major else None
            bz, cs = _conv_call(x3, lay, mod, wts["norm_pre"], wts["conv_w_in"], wts["conv_w"],
                                st, l, jl, 1, 256)
            x3 = _out_proj_call(x3, out_lay, mod, wts["norm_post"], bz, wts["conv_w_out"],
                                l, (jl, 0), 1, 1.0)
            if time_major:
                new_conv.append(cs.transpose(1, 0, 2))
            else:
                tps = lay.tiles_per_seq
                new_conv.append(cs[tps - 1::tps])
        else:
            st = shift_state[jl] if time_major else None
            xr, xk, xv, tw, ta, tg, ss = _rw_prep_call(
                x3, prep_lay, mod, wts["norm_pre"], wts["rw_mix"], wts["rw_w1"], wts["rw_a1"],
                wts["rw_g1"], st, l, jl, 1)
            flat = lambda z: z.reshape(m, z.shape[2])
            r, k, v, w, a, g = _rw_proj_call(
                flat(xr), flat(xk), flat(xv), flat(tw), flat(ta), flat(tg),
                wts["rw_wr"], wts["rw_wk"], wts["rw_wv"], wts["rw_w2"], wts["rw_a2"],
                wts["rw_g2"], wts["rw_w0"], wts["rw_a0"], jl, lay.tile_rows, 256)
            p = jnp.concatenate([
                wts["rw_kk"][jl][None], wts["rw_ka"][jl][None], wts["rw_rk"][jl].reshape(1, d),
                wts["rw_lnw"][jl][None], wts["rw_lnb"][jl][None],
                jnp.zeros((3, d), F32)], axis=0)
            s0 = _wkv_to_groups(wkv_state[jl]) if time_major else None
            yg, s_fin = _scan_call(r, k, v, w, a, g, p, s0, nb, t, chunk, time_major)
            x3 = _out_proj_call(x3, out_lay, mod, wts["norm_post"], yg.reshape(x3.shape),
                                wts["rw_wo"], l, (jl, 0), 1, 1.0)
            new_shift.append(ss.reshape(nb, d))
            new_wkv.append(_wkv_from_groups(s_fin))
        x3 = ffn(x3, l, 1, 2)
    return x3, jnp.stack(new_conv), jnp.stack(new_shift), jnp.stack(new_wkv)


def kernel(x_prompt, x_sample, state_conv, state_shift, state_wkv, c_prompt, c_sample, mod_w, mod_b, norm_pre, norm_post, ffn_w_in, ffn_w_out, conv_w_in, conv_w, conv_w_out, rw_mix, rw_w0, rw_w1, rw_w2, rw_a0, rw_a1, rw_a2, rw_g1, rw_g2, rw_kk, rw_ka, rw_rk, rw_wr, rw_wk, rw_wv, rw_wo, rw_lnw, rw_lnb):
    b, t, d = x_prompt.shape
    sb, st, _ = x_sample.shape
    depth = mod_w.shape[0]
    n_sub = norm_pre.shape[1]
    wts = dict(norm_pre=norm_pre.reshape(depth, n_sub, 1, d),
               norm_post=norm_post.reshape(depth, n_sub, 1, d), ffn_w_in=ffn_w_in,
               ffn_w_out=ffn_w_out.astype(BF16),
               conv_w_out=conv_w_out.astype(BF16)[:, None],
               rw_wo=rw_wo.astype(BF16)[:, None],
               conv_w_in=conv_w_in, conv_w=conv_w, rw_mix=rw_mix,
               rw_w0=rw_w0, rw_w1=rw_w1, rw_w2=rw_w2, rw_a0=rw_a0, rw_a1=rw_a1, rw_a2=rw_a2,
               rw_g1=rw_g1, rw_g2=rw_g2, rw_kk=rw_kk, rw_ka=rw_ka, rw_rk=rw_rk, rw_wr=rw_wr,
               rw_wk=rw_wk, rw_wv=rw_wv, rw_lnw=rw_lnw, rw_lnb=rw_lnb)

    n_c = b + sb
    pad = (-n_c) % 8
    c_all = jnp.concatenate([c_prompt, c_sample, jnp.zeros((pad, d), F32)], axis=0)
    mod_all = _mod_call(c_all, mod_w, mod_b)
    mod_p = (mod_all[:, :b].reshape(depth, b, N_MOD, d).transpose(0, 2, 1, 3)
             .reshape(depth, N_MOD, b, 1, d))
    mod_s = mod_all[:, b:n_c].reshape(depth, sb, N_MOD, d).transpose(0, 2, 1, 3)

    y_p, conv_p, shift_p, wkv_p = _trunk(
        x_prompt, mod_p, None, None, None, wts, time_major=False,
        tile=min(t, 1024), out_tile=min(t, 512), prep_tile=min(t, 128), chunk=min(t, 64))
    y_s, conv_s, shift_s, wkv_s = _trunk(
        x_sample.transpose(1, 0, 2), mod_s, state_conv, state_shift, state_wkv, wts,
        time_major=True, tile=sb, out_tile=min(sb, 64), prep_tile=min(sb, 16), chunk=st)
    return (y_p, y_s.transpose(1, 0, 2), conv_p, shift_p, wkv_p, conv_s, shift_s, wkv_s)
```

```python
import functools
import math

import jax
import jax.numpy as jnp
from jax import lax
from jax.experimental import pallas as pl
from jax.experimental.pallas import tpu as pltpu

F32 = jnp.float32
BF16 = jnp.bfloat16

RMS_EPS = 1e-6
GN_EPS = 64e-5
NORM_EPS = 1e-12
HALF_STEP = 0.5
HEAD_SIZE = 64
N_MOD = 9

MXU_WIDTH_V7X = 256
GROUP_LANES = MXU_WIDTH_V7X
HEADS_PER_GROUP = GROUP_LANES // HEAD_SIZE
VMEM_LIMIT_V7X = 60 * 2**20
ROW_PIECE = 256

NN = (((1,), (0,)), ((), ()))
NT = (((1,), (1,)), ((), ()))
TN = (((0,), (0,)), ((), ()))


def _mm(a, b, dims=NN):
    return lax.dot_general(a.astype(BF16), b.astype(BF16), dims, preferred_element_type=F32)


def _sigmoid(x):
    return 1.0 / (1.0 + jnp.exp(-x))


def _params(n_axes):
    return pltpu.CompilerParams(dimension_semantics=("arbitrary",) * n_axes,
                                vmem_limit_bytes=VMEM_LIMIT_V7X)


def _rms(x, g):
    return x * lax.rsqrt(jnp.mean(x * x, axis=-1, keepdims=True) + RMS_EPS) * g


def _modulated_pre(x, m_ref, g):
    return _rms(x, g) * (1.0 + m_ref[1]) + m_ref[0]


def _gated_post(x, y, m_ref, g, res_w):
    return x + (res_w * m_ref[2]) * _rms(y, g)


def _pieces(lead, rows):
    if lead == 1:
        n = min(ROW_PIECE, rows)
        return [(slice(0, 1), slice(r0, r0 + n)) for r0 in range(0, rows, n)]
    n = min(lead, max(1, ROW_PIECE // rows))
    return [(slice(a0, a0 + n), slice(0, rows)) for a0 in range(0, lead, n)]


def _piece_rows(x_ref, ls, rs):
    shp = x_ref[ls, rs, :].shape
    return shp, shp[0] * shp[1]


def _pre_to_scratch(x_ref, m_ref, g, h_ref):
    lead, rows, d = x_ref.shape
    r0 = 0
    for ls, rs in _pieces(lead, rows):
        _, n = _piece_rows(x_ref, ls, rs)
        h = _modulated_pre(x_ref[ls, rs, :], m_ref, g)
        h_ref[r0:r0 + n, :] = h.reshape(n, d).astype(BF16)
        r0 += n


def _shift_time(u, k, fill, time_major):
    lead, rows, n = u.shape
    if time_major:
        return jnp.concatenate([fill, u[:lead - k]], axis=0)
    out = pltpu.roll(u.reshape(rows, n), k, 0)
    row = lax.broadcasted_iota(jnp.int32, (rows, n), 0)
    for t in range(k):
        out = jnp.where(row == t, fill[t:t + 1, :], out)
    return out.reshape(1, rows, n)


class _Layout:
    def __init__(self, shape, time_major, rows_per_tile):
        self.time_major = time_major
        self.shape = shape
        lead, rows, d = shape
        if time_major:
            self.block = (lead, rows_per_tile, d)
            self.tiles_per_seq = 1
            self.n_tiles = rows // rows_per_tile
        else:
            self.block = (1, rows_per_tile, d)
            self.tiles_per_seq = rows // rows_per_tile
            self.n_tiles = lead * self.tiles_per_seq
        self.tile_rows = self.block[0] * self.block[1]

    def xmap(self, i, *_):
        if self.time_major:
            return (0, i, 0)
        return (i // self.tiles_per_seq, i % self.tiles_per_seq, 0)

    def x_spec(self, **kw):
        return pl.BlockSpec(self.block, self.xmap, **kw)

    def mod_spec(self, mod, l, sub):
        d = self.shape[2]
        if self.time_major:
            return pl.BlockSpec((None, 3, self.block[1], d), lambda i, *_: (l, sub, i, 0))
        tps = self.tiles_per_seq
        return pl.BlockSpec((None, 3, None, 1, d), lambda i, *_: (l, sub, i // tps, 0, 0))


def _gain_spec(gains, l, sub):
    return pl.BlockSpec((None, None, 1, gains.shape[3]), lambda i, *_: (l, sub, 0, 0))


def _mod_kernel(c_ref, w_ref, b_ref, o_ref):
    c = c_ref[...]
    o_ref[...] = _mm(c * _sigmoid(c), w_ref[...]) + b_ref[...]


def _mod_call(c_all, mod_w, mod_b):
    depth, d, n = mod_w.shape
    nbp = c_all.shape[0]
    tn = math.gcd(n, 1024)
    return pl.pallas_call(
        _mod_kernel,
        grid=(depth, n // tn),
        in_specs=[
            pl.BlockSpec((nbp, d), lambda l, j: (0, 0)),
            pl.BlockSpec((None, d, tn), lambda l, j: (l, 0, j)),
            pl.BlockSpec((None, 1, tn), lambda l, j: (l, 0, j)),
        ],
        out_specs=pl.BlockSpec((None, nbp, tn), lambda l, j: (l, 0, j)),
        out_shape=jax.ShapeDtypeStruct((depth, nbp, n), F32),
        compiler_params=_params(2),
    )(c_all, mod_w, mod_b.reshape(depth, 1, n))


def _out_proj_kernel(x_ref, m_ref, gpost_ref, a_ref, w_ref, o_ref, *, res_w):
    lead, rows, _ = x_ref.shape
    for ls, rs in _pieces(lead, rows):
        shp, n = _piece_rows(x_ref, ls, rs)
        a = a_ref[ls, rs, :]
        y = lax.dot_general(a.reshape(n, a.shape[2]), w_ref[...], NN,
                            preferred_element_type=F32).reshape(shp)
        o_ref[ls, rs, :] = _gated_post(x_ref[ls, rs, :], y, m_ref, gpost_ref[...], res_w)


def _out_proj_call(x3, lay, mod, norm_post, act, w, l, widx, sub, res_w):
    k, d = w.shape[2], w.shape[3]
    return pl.pallas_call(
        functools.partial(_out_proj_kernel, res_w=res_w),
        grid=(lay.n_tiles,),
        in_specs=[
            lay.x_spec(),
            lay.mod_spec(mod, l, sub),
            _gain_spec(norm_post, l, sub),
            pl.BlockSpec((lay.block[0], lay.block[1], k), lay.xmap),
            pl.BlockSpec((None, None, k, d), lambda i: (widx[0], widx[1], 0, 0),
                         pipeline_mode=pl.Buffered(1)),
        ],
        out_specs=lay.x_spec(),
        out_shape=jax.ShapeDtypeStruct(x3.shape, F32),
        compiler_params=_params(1),
    )(x3, mod, norm_post, act, w)


def _ffn_in_kernel(x_ref, m_ref, gpre_ref, wg_ref, wu_ref, a_ref, h_ref):
    lead, rows, _ = x_ref.shape

    @pl.when(pl.program_id(1) == 0)
    def _():
        _pre_to_scratch(x_ref, m_ref, gpre_ref[...], h_ref)

    wg = wg_ref[...].astype(BF16)
    wu = wu_ref[...].astype(BF16)
    r0 = 0
    for ls, rs in _pieces(lead, rows):
        shp, n = _piece_rows(x_ref, ls, rs)
        h = h_ref[r0:r0 + n, :]
        gt = _mm(h, wg)
        up = _mm(h, wu)
        a_ref[ls, rs, :] = (gt * _sigmoid(gt) * up).astype(BF16).reshape(shp[0], shp[1], -1)
        r0 += n


def _ffn_in_call(x3, lay, mod, norm_pre, w_in, l, s, sub, tf):
    lead, rows, d = x3.shape
    f = w_in.shape[3] // 2
    nj = f // tf
    return pl.pallas_call(
        _ffn_in_kernel,
        grid=(lay.n_tiles, nj),
        in_specs=[
            lay.x_spec(pipeline_mode=pl.Buffered(1)),
            lay.mod_spec(mod, l, sub),
            _gain_spec(norm_pre, l, sub),
            pl.BlockSpec((None, None, d, tf), lambda i, j: (l, s, 0, j)),
            pl.BlockSpec((None, None, d, tf), lambda i, j: (l, s, 0, nj + j)),
        ],
        out_specs=pl.BlockSpec((lay.block[0], lay.block[1], tf),
                               lambda i, j: lay.xmap(i)[:2] + (j,)),
        out_shape=jax.ShapeDtypeStruct((lead, rows, f), BF16),
        scratch_shapes=[pltpu.VMEM((lay.tile_rows, d), BF16)],
        compiler_params=_params(2),
    )(x3, mod, norm_pre, w_in, w_in)


def _conv_kernel(*refs, time_major, tiles_per_seq):
    if time_major:
        (x_ref, m_ref, gpre_ref, wb_ref, wc_ref, wx_ref, cw_ref, st_ref,
         o_ref, so_ref, h_ref) = refs
    else:
        (x_ref, m_ref, gpre_ref, wb_ref, wc_ref, wx_ref, cw_ref,
         o_ref, so_ref, h_ref, carry_ref) = refs
    i = pl.program_id(0)
    j = pl.program_id(1)
    lead, rows, _ = x_ref.shape
    tm = lead * rows

    @pl.when(j == 0)
    def _():
        _pre_to_scratch(x_ref, m_ref, gpre_ref[...], h_ref)

    h = h_ref[...]
    bg = _mm(h, wb_ref[...])
    u2 = _mm(h, wc_ref[...]) * _mm(h, wx_ref[...])
    tn = u2.shape[1]
    u = u2.reshape(lead, rows, tn)
    if time_major:
        st = st_ref[...]
        fill1, fill2 = st[1:2], st
        so_ref[...] = u[lead - 2:lead]
    else:
        @pl.when(i % tiles_per_seq == 0)
        def _():
            carry_ref[j] = jnp.zeros((8, tn), F32)
        prev = carry_ref[j]
        fill1, fill2 = prev[7:8, :], prev[6:8, :]
        carry_ref[j] = u2[tm - 8:tm, :]
        so_ref[...] = u[:, rows - 2:rows, :]
    cw = cw_ref[...]
    z = (cw[0:1, :] * _shift_time(u, 2, fill2, time_major)
         + cw[1:2, :] * _shift_time(u, 1, fill1, time_major) + cw[2:3, :] * u)
    o_ref[...] = (bg.reshape(lead, rows, tn) * z).astype(BF16)


def _conv_call(x3, lay, mod, norm_pre, w_in, cw, state, l, jl, sub, tn):
    lead_n, rows_n, d = x3.shape
    dc = cw.shape[2]
    nj = dc // tn
    in_specs = [
        lay.x_spec(pipeline_mode=pl.Buffered(1)),
        lay.mod_spec(mod, l, sub),
        _gain_spec(norm_pre, l, sub),
        pl.BlockSpec((None, d, tn), lambda i, j: (jl, 0, j)),
        pl.BlockSpec((None, d, tn), lambda i, j: (jl, 0, nj + j)),
        pl.BlockSpec((None, d, tn), lambda i, j: (jl, 0, 2 * nj + j)),
        pl.BlockSpec((None, cw.shape[1], tn), lambda i, j: (jl, 0, j)),
    ]
    args = [x3, mod, norm_pre, w_in, w_in, w_in, cw]
    scratch = [pltpu.VMEM((lay.tile_rows, d), BF16)]
    if lay.time_major:
        nb = lay.block[1]
        in_specs.append(pl.BlockSpec((2, nb, tn), lambda i, j: (0, i, j)))
        args.append(state)
        so_spec = pl.BlockSpec((2, nb, tn), lambda i, j: (0, i, j))
        so_shape = jax.ShapeDtypeStruct((2, x3.shape[1], dc), F32)
    else:
        scratch.append(pltpu.VMEM((nj, 8, tn), F32))
        so_spec = pl.BlockSpec((1, 2, tn), lambda i, j: (i, 0, j))
        so_shape = jax.ShapeDtypeStruct((lay.n_tiles, 2, dc), F32)
    return pl.pallas_call(
        functools.partial(_conv_kernel, time_major=lay.time_major,
                          tiles_per_seq=lay.tiles_per_seq),
        grid=(lay.n_tiles, nj),
        in_specs=in_specs,
        out_specs=[pl.BlockSpec((lay.block[0], lay.block[1], tn),
                                lambda i, j: lay.xmap(i)[:2] + (j,)), so_spec],
        out_shape=[jax.ShapeDtypeStruct((lead_n, rows_n, dc), BF16), so_shape],
        scratch_shapes=scratch,
        compiler_params=_params(2),
    )(*args)


def _rw_prep_kernel(*refs, time_major, tiles_per_seq):
    if time_major:
        (x_ref, m_ref, gpre_ref, mix_ref, w1_ref, a1_ref, g1_ref, st_ref,
         xr_ref, xk_ref, xv_ref, tw_ref, ta_ref, tg_ref, so_ref) = refs
    else:
        (x_ref, m_ref, gpre_ref, mix_ref, w1_ref, a1_ref, g1_ref,
         xr_ref, xk_ref, xv_ref, tw_ref, ta_ref, tg_ref, so_ref, carry_ref) = refs
    i = pl.program_id(0)
    lead, rows, d = x_ref.shape
    n = lead * rows
    h = _modulated_pre(x_ref[...], m_ref, gpre_ref[...])
    if time_major:
        fill = st_ref[...][None]
        so_ref[...] = h[lead - 1]
    else:
        @pl.when(i % tiles_per_seq == 0)
        def _():
            carry_ref[...] = jnp.zeros((8, d), F32)
        fill = carry_ref[7:8, :]
        carry_ref[...] = h[0, rows - 8:rows, :]
        so_ref[...] = h[:, rows - 1:rows, :]
    xx = _shift_time(h, 1, fill, time_major) - h
    mix = mix_ref[...]

    def mixed(k):
        return h + xx * mix[k:k + 1, :]

    def low_rank(k, w_ref):
        return _mm(mixed(k).reshape(n, d), w_ref[...]).reshape(lead, rows, w_ref.shape[1])

    xr_ref[...] = mixed(0).astype(BF16)
    tw_ref[...] = jnp.tanh(low_rank(1, w1_ref))
    xk_ref[...] = mixed(2).astype(BF16)
    xv_ref[...] = mixed(3).astype(BF16)
    ta_ref[...] = low_rank(4, a1_ref)
    tg_ref[...] = _sigmoid(low_rank(5, g1_ref))


def _rw_prep_call(x3, lay, mod, norm_pre, mix, w1, a1, g1, state, l, jl, sub):
    lead, rows, d = x3.shape
    dl, dg = w1.shape[2], g1.shape[2]
    in_specs = [
        lay.x_spec(),
        lay.mod_spec(mod, l, sub),
        _gain_spec(norm_pre, l, sub),
        pl.BlockSpec((None, mix.shape[1], d), lambda i: (jl, 0, 0)),
        pl.BlockSpec((None, d, dl), lambda i: (jl, 0, 0)),
        pl.BlockSpec((None, d, dl), lambda i: (jl, 0, 0)),
        pl.BlockSpec((None, d, dg), lambda i: (jl, 0, 0)),
    ]
    args = [x3, mod, norm_pre, mix, w1, a1, g1]
    scratch = []
    blk = lay.block
    if lay.time_major:
        in_specs.append(pl.BlockSpec((blk[1], d), lambda i: (i, 0)))
        args.append(state)
        so_spec = pl.BlockSpec((blk[1], d), lambda i: (i, 0))
        so_shape = jax.ShapeDtypeStruct((rows, d), F32)
    else:
        scratch.append(pltpu.VMEM((8, d), F32))
        tps = lay.tiles_per_seq
        so_spec = pl.BlockSpec((1, 1, d), lambda i: (i // tps, 0, 0))
        so_shape = jax.ShapeDtypeStruct((lead, 1, d), F32)

    def ospec(width):
        return pl.BlockSpec((blk[0], blk[1], width), lay.xmap)

    return pl.pallas_call(
        functools.partial(_rw_prep_kernel, time_major=lay.time_major,
                          tiles_per_seq=lay.tiles_per_seq),
        grid=(lay.n_tiles,),
        in_specs=in_specs,
        out_specs=[ospec(d)] * 3 + [ospec(dl), ospec(dl), ospec(dg), so_spec],
        out_shape=[jax.ShapeDtypeStruct((lead, rows, d), BF16)] * 3
        + [jax.ShapeDtypeStruct((lead, rows, dl), F32)] * 2
        + [jax.ShapeDtypeStruct((lead, rows, dg), F32), so_shape],
        scratch_shapes=scratch,
        compiler_params=_params(1),
    )(*args)


def _rw_proj_kernel(xr_ref, xk_ref, xv_ref, tw_ref, ta_ref, tg_ref, wr_ref, wk_ref, wv_ref,
                    w2_ref, a2_ref, g2_ref, w0_ref, a0_ref,
                    r_ref, k_ref, v_ref, w_ref, a_ref, g_ref):
    r_ref[...] = _mm(xr_ref[...], wr_ref[...])
    k_ref[...] = _mm(xk_ref[...], wk_ref[...])
    v_ref[...] = _mm(xv_ref[...], wv_ref[...])
    w_ref[...] = w0_ref[...] + _mm(tw_ref[...], w2_ref[...])
    a_ref[...] = _sigmoid(a0_ref[...] + _mm(ta_ref[...], a2_ref[...]))
    g_ref[...] = _mm(tg_ref[...], g2_ref[...])


def _rw_proj_call(xr, xk, xv, tw, ta, tg, wr, wk, wv, w2, a2, g2, w0, a0, jl, tm, tn):
    m, d = xr.shape
    dl, dg = tw.shape[1], tg.shape[1]
    nr = w0.shape[0]
    xspec = pl.BlockSpec((tm, d), lambda i, j: (i, 0))
    wspec = pl.BlockSpec((None, d, tn), lambda i, j: (jl, 0, j))
    vspec = pl.BlockSpec((None, 1, tn), lambda i, j: (jl, 0, j))
    ospec = pl.BlockSpec((tm, tn), lambda i, j: (i, j))
    return pl.pallas_call(
        _rw_proj_kernel,
        grid=(m // tm, d // tn),
        in_specs=[xspec, xspec, xspec,
                  pl.BlockSpec((tm, dl), lambda i, j: (i, 0)),
                  pl.BlockSpec((tm, dl), lambda i, j: (i, 0)),
                  pl.BlockSpec((tm, dg), lambda i, j: (i, 0)),
                  wspec, wspec, wspec,
                  pl.BlockSpec((None, dl, tn), lambda i, j: (jl, 0, j)),
                  pl.BlockSpec((None, dl, tn), lambda i, j: (jl, 0, j)),
                  pl.BlockSpec((None, dg, tn), lambda i, j: (jl, 0, j)),
                  vspec, vspec],
        out_specs=[ospec] * 6,
        out_shape=[jax.ShapeDtypeStruct((m, d), F32)] * 6,
        compiler_params=_params(2),
    )(xr, xk, xv, tw, ta, tg, wr, wk, wv, w2, a2, g2,
      w0.reshape(nr, 1, d), a0.reshape(nr, 1, d))


def _iota2(shape, axis):
    return lax.broadcasted_iota(jnp.int32, shape, axis)


def _same_block(shape, row_block, lane_block):
    r = lax.shift_right_logical(_iota2(shape, 0), int(math.log2(row_block)))
    c = lax.shift_right_logical(_iota2(shape, 1), int(math.log2(lane_block)))
    return r == c


def _bd_rows(x, mask):
    return jnp.where(mask, jnp.concatenate([x] * HEADS_PER_GROUP, axis=0), 0.0)


def _split_mm(x, ones):
    hi = x.astype(BF16)
    lo = (x - hi.astype(F32)).astype(BF16)
    return (lax.dot_general(hi, ones, NN, preferred_element_type=F32)
            + lax.dot_general(lo, ones, NN, preferred_element_type=F32))


def _each(fn, *lists):
    return [fn(*xs) for xs in zip(*lists)]


def _wkv_chunk(r, k, v, a_, b_, lw, cum, s_bd, c):
    hs = HEAD_SIZE
    tlanes = HEADS_PER_GROUP * c
    m_ch = _same_block((tlanes, GROUP_LANES), c, hs)
    m_tt = _same_block((tlanes, tlanes), c, c)
    m_ss = _same_block((GROUP_LANES, GROUP_LANES), hs, hs)
    t_row = _iota2((c, tlanes), 0)
    s_lane = jnp.bitwise_and(_iota2((c, tlanes), 1), c - 1)
    strict = s_lane < t_row
    incl = s_lane <= t_row

    def bd_ch(x):
        return _bd_rows(x, m_ch)

    def bd_tt(x):
        return _bd_rows(x, m_tt)

    def apply(a, x):
        return _mm(a, bd_ch(x))

    cl = _each(lambda x: x[c - 1:c, :], cum)
    at = _each(lambda a, x, l: a * jnp.exp(x - l), a_, cum, lw)
    rt = _each(lambda a, x: a * jnp.exp(x), r, cum)
    bt = _each(lambda a, x: a * jnp.exp(-x), b_, cum)
    kt = _each(lambda a, x: a * jnp.exp(-x), k, cum)
    bh = _each(lambda a, x, xl: a * jnp.exp(xl - x), b_, cum, cl)
    kh = _each(lambda a, x, xl: a * jnp.exp(xl - x), k, cum, cl)

    lhs = _each(lambda a, b: jnp.concatenate([a, b], axis=0), at, rt)
    ob = _each(lambda a, x: _mm(a, bd_ch(x), NT), lhs, bt)
    ok = _each(lambda a, x: _mm(a, bd_ch(x), NT), lhs, kt)
    a_ab = _each(lambda x: jnp.where(strict, x[:c], 0.0), ob)
    a_rb = _each(lambda x: jnp.where(incl, x[c:], 0.0), ob)
    a_ak = _each(lambda x: jnp.where(strict, x[:c], 0.0), ok)
    a_rk = _each(lambda x: jnp.where(incl, x[c:], 0.0), ok)
    akv = _each(lambda a, b, x: apply(jnp.concatenate([a, b], axis=0), x), a_ak, a_rk, v)
    av = _each(lambda x: x[:c], akv)
    y_kv = _each(lambda x: x[c:], akv)

    eye = jnp.where(s_lane == t_row, 1.0, 0.0)
    tinv = _each(lambda x: eye + x, a_ab)
    npow = a_ab
    for _ in range(int(math.log2(c)) - 1):
        npow = _each(lambda x: _mm(x, bd_tt(x)), npow)
        tinv = _each(lambda t, x: t + _mm(t, bd_tt(x)), tinv, npow)

    wt = _each(apply, tinv, at)
    w = _each(apply, tinv, av)
    qt = _each(lambda x, a, y: x + apply(a, y), rt, a_rb, wt)
    yi = _each(lambda a, x, y: apply(a, x) + y, a_rb, w, y_kv)
    g_off = _each(lambda x, y: jnp.where(m_ss, _mm(x, y, TN), 0.0), wt, bh)
    h_t = _each(lambda x, y, p, q: jnp.where(
        m_ss, _mm(jnp.concatenate([x, p], axis=0), jnp.concatenate([y, q], axis=0), TN), 0.0),
                w, bh, v, kh)

    y = _each(lambda q, s, x: _mm(q, s, NT) + x, qt, s_bd, yi)
    s_new = _each(lambda s, xl, g, h: s * jnp.exp(xl) + _mm(s, g) + h, s_bd, cl, g_off, h_t)
    return y, s_new


def _scan_kernel(*refs, c, d, nq, has_state, lane_packed):
    if has_state:
        (r_ref, k_ref, v_ref, w_ref, a_ref, g_ref, p_ref, s0_ref, y_ref, so_ref, sbd_ref) = refs
    else:
        (r_ref, k_ref, v_ref, w_ref, a_ref, g_ref, p_ref, y_ref, so_ref, sbd_ref) = refs
    ci = pl.program_id(1)
    hs = HEAD_SIZE
    ng = d // GROUP_LANES
    chains = [(q, g) for q in range(nq) for g in range(ng)]
    n_ch = len(chains)
    m_ss = _same_block((GROUP_LANES, GROUP_LANES), hs, hs)
    ones_bd = jnp.where(m_ss, 1.0, 0.0).astype(BF16)

    def lanes(g):
        return slice(g * GROUP_LANES, (g + 1) * GROUP_LANES)

    def rd(ref, q, g):
        if lane_packed:
            return ref[:, q * d + g * GROUP_LANES:q * d + (g + 1) * GROUP_LANES]
        return ref[q, :, lanes(g)]

    def par(row, g):
        return p_ref[row:row + 1, lanes(g)]

    @pl.when(ci == 0)
    def _():
        for i, (q, g) in enumerate(chains):
            if has_state:
                heads = [s0_ref[q, HEADS_PER_GROUP * g + h] for h in range(HEADS_PER_GROUP)]
                sbd_ref[i] = _bd_rows(jnp.concatenate(heads, axis=1), m_ss)
            else:
                sbd_ref[i] = jnp.zeros((GROUP_LANES, GROUP_LANES), F32)

    def log_decay(wq):
        z = -wq
        softplus = jnp.maximum(z, 0.0) + jnp.log(1.0 + jnp.exp(-jnp.abs(z)))
        lw = -jnp.exp(-softplus - 0.5)
        row = _iota2(lw.shape, 0)
        cum = lw
        step = 1
        while step < c:
            cum = cum + jnp.where(row >= step, pltpu.roll(cum, step, 0), 0.0)
            step *= 2
        return lw, cum

    if lane_packed:
        lw_all, cum_all = log_decay(w_ref[...])
        lw = [lw_all[:, q * d + g * GROUP_LANES:q * d + (g + 1) * GROUP_LANES] for q, g in chains]
        cum = [cum_all[:, q * d + g * GROUP_LANES:q * d + (g + 1) * GROUP_LANES] for q, g in chains]
    else:
        per_q = [log_decay(w_ref[q]) for q in range(nq)]
        lw = [per_q[q][0][:, lanes(g)] for q, g in chains]
        cum = [per_q[q][1][:, lanes(g)] for q, g in chains]

    def seg_sum(xs):
        y = _split_mm(jnp.concatenate(xs, axis=0), ones_bd)
        return [y[i * c:(i + 1) * c] for i in range(len(xs))]

    r = [rd(r_ref, q, g) for q, g in chains]
    k = [rd(k_ref, q, g) for q, g in chains]
    v = [rd(v_ref, q, g) for q, g in chains]
    a_sig = [rd(a_ref, q, g) for q, g in chains]
    kk = [x * par(0, g) for x, (q, g) in zip(k, chains)]
    k = [x * (1.0 + (a - 1.0) * par(1, g)) for x, a, (q, g) in zip(k, a_sig, chains)]
    sums = seg_sum(_each(lambda x: x * x, kk)
                   + [x * y * par(2, g) for x, y, (q, g) in zip(r, k, chains)])
    kk = _each(lambda x, n: x / jnp.maximum(jnp.sqrt(n), NORM_EPS), kk, sums[:n_ch])
    bonus = _each(lambda s, z: s * z, sums[n_ch:], v)
    y, s_new = _wkv_chunk(r, k, v, _each(lambda x: -x, kk), _each(lambda x, a: x * a, kk, a_sig),
                          lw, cum, [sbd_ref[i] for i in range(n_ch)], c)
    for i in range(n_ch):
        sbd_ref[i] = s_new[i]

    mu = _each(lambda x: x * (1.0 / hs), seg_sum(y))
    dy = _each(lambda x, m: x - m, y, mu)
    var = _each(lambda x: x * (1.0 / hs), seg_sum(_each(lambda x: x * x, dy)))
    for i, (q, g) in enumerate(chains):
        yn = dy[i] * lax.rsqrt(var[i] + GN_EPS) * par(3, g) + par(4, g)
        out = ((yn + bonus[i]) * rd(g_ref, q, g)).astype(BF16)
        if lane_packed:
            y_ref[:, q * d + g * GROUP_LANES:q * d + (g + 1) * GROUP_LANES] = out
        else:
            y_ref[q, :, lanes(g)] = out

    @pl.when(ci == pl.num_programs(1) - 1)
    def _():
        for i, (q, g) in enumerate(chains):
            s = sbd_ref[i]
            s = s[0:hs] + s[hs:2 * hs] + s[2 * hs:3 * hs] + s[3 * hs:4 * hs]
            for h in range(HEADS_PER_GROUP):
                so_ref[q, HEADS_PER_GROUP * g + h] = s[:, h * hs:(h + 1) * hs]


def _scan_call(r, k, v, w, a, g, p, s0, nb, t, c, nq, time_major):
    m, d = r.shape
    nc = t // c
    ng = d // GROUP_LANES
    nh = d // HEAD_SIZE
    has_state = s0 is not None
    if time_major:
        r, k, v, w, a, g = (z.reshape(t, nb * d) for z in (r, k, v, w, a, g))
        xspec = pl.BlockSpec((c, nq * d), lambda b, ci: (ci, b))
    else:
        r, k, v, w, a, g = (z.reshape(nb, t, d) for z in (r, k, v, w, a, g))
        xspec = pl.BlockSpec((nq, c, d), lambda b, ci: (b, ci, 0))
    sspec = pl.BlockSpec((nq, nh, HEAD_SIZE, HEAD_SIZE), lambda b, ci: (b, 0, 0, 0))
    in_specs = [xspec] * 6 + [pl.BlockSpec(p.shape, lambda b, ci: (0, 0))]
    args = [r, k, v, w, a, g, p]
    if has_state:
        in_specs.append(sspec)
        args.append(s0)
    y, s_fin = pl.pallas_call(
        functools.partial(_scan_kernel, c=c, d=d, nq=nq, has_state=has_state,
                          lane_packed=time_major),
        grid=(nb // nq, nc),
        in_specs=in_specs,
        out_specs=[xspec, sspec],
        out_shape=[jax.ShapeDtypeStruct(r.shape, BF16),
                   jax.ShapeDtypeStruct((nb, nh, HEAD_SIZE, HEAD_SIZE), F32)],
        scratch_shapes=[pltpu.VMEM((nq * ng, GROUP_LANES, GROUP_LANES), F32)],
        compiler_params=_params(2),
    )(*args)
    return y.reshape(m, d), s_fin


def _trunk(x3, mod, conv_state, shift_state, wkv_state, wts, *, time_major, tile, out_tile,
           prep_tile, chunk, scan_seqs):
    d = x3.shape[2]
    nb, t = (x3.shape[1], x3.shape[0]) if time_major else (x3.shape[0], x3.shape[1])
    depth = mod.shape[0]
    lay = _Layout(x3.shape, time_major, tile)
    out_lay = _Layout(x3.shape, time_major, out_tile)
    prep_lay = _Layout(x3.shape, time_major, prep_tile)
    m = nb * t
    new_conv, new_shift, new_wkv = [], [], []

    def ffn(x3, l, s, sub):
        act = _ffn_in_call(x3, lay, mod, wts["norm_pre"], wts["ffn_w_in"], l, s, sub, 512)
        return _out_proj_call(x3, out_lay, mod, wts["norm_post"], act, wts["ffn_w_out"],
                              l, (l, s), sub, HALF_STEP)

    for l in range(depth):
        jl = l // 2
        x3 = ffn(x3, l, 0, 0)
        if l % 2 == 0:
            st = conv_state[jl].transpose(1, 0, 2) if time_major else None
            bz, cs = _conv_call(x3, lay, mod, wts["norm_pre"], wts["conv_w_in"], wts["conv_w"],
                                st, l, jl, 1, 256)
            x3 = _out_proj_call(x3, out_lay, mod, wts["norm_post"], bz, wts["conv_w_out"],
                                l, (jl, 0), 1, 1.0)
            if time_major:
                new_conv.append(cs.transpose(1, 0, 2))
            else:
                tps = lay.tiles_per_seq
                new_conv.append(cs[tps - 1::tps])
        else:
            st = shift_state[jl] if time_major else None
            xr, xk, xv, tw, ta, tg, ss = _rw_prep_call(
                x3, prep_lay, mod, wts["norm_pre"], wts["rw_mix"], wts["rw_w1"], wts["rw_a1"],
                wts["rw_g1"], st, l, jl, 1)
            flat = lambda z: z.reshape(m, z.shape[2])
            r, k, v, w, a, g = _rw_proj_call(
                flat(xr), flat(xk), flat(xv), flat(tw), flat(ta), flat(tg),
                wts["rw_wr"], wts["rw_wk"], wts["rw_wv"], wts["rw_w2"], wts["rw_a2"],
                wts["rw_g2"], wts["rw_w0"], wts["rw_a0"], jl, lay.tile_rows, 256)
            p = jnp.concatenate([
                wts["rw_kk"][jl][None], wts["rw_ka"][jl][None], wts["rw_rk"][jl].reshape(1, d),
                wts["rw_lnw"][jl][None], wts["rw_lnb"][jl][None],
                jnp.zeros((3, d), F32)], axis=0)
            s0 = wkv_state[jl] if time_major else None
            yg, s_fin = _scan_call(r, k, v, w, a, g, p, s0, nb, t, chunk, scan_seqs, time_major)
            x3 = _out_proj_call(x3, out_lay, mod, wts["norm_post"], yg.reshape(x3.shape),
                                wts["rw_wo"], l, (jl, 0), 1, 1.0)
            new_shift.append(ss.reshape(nb, d))
            new_wkv.append(s_fin)
        x3 = ffn(x3, l, 1, 2)
    return x3, jnp.stack(new_conv), jnp.stack(new_shift), jnp.stack(new_wkv)


def kernel(x_prompt, x_sample, state_conv, state_shift, state_wkv, c_prompt, c_sample, mod_w, mod_b, norm_pre, norm_post, ffn_w_in, ffn_w_out, conv_w_in, conv_w, conv_w_out, rw_mix, rw_w0, rw_w1, rw_w2, rw_a0, rw_a1, rw_a2, rw_g1, rw_g2, rw_kk, rw_ka, rw_rk, rw_wr, rw_wk, rw_wv, rw_wo, rw_lnw, rw_lnb):
    b, t, d = x_prompt.shape
    sb, st, _ = x_sample.shape
    depth = mod_w.shape[0]
    n_sub = norm_pre.shape[1]
    wts = dict(norm_pre=norm_pre.reshape(depth, n_sub, 1, d),
               norm_post=norm_post.reshape(depth, n_sub, 1, d), ffn_w_in=ffn_w_in,
               ffn_w_out=ffn_w_out.astype(BF16),
               conv_w_out=conv_w_out.astype(BF16)[:, None],
               rw_wo=rw_wo.astype(BF16)[:, None],
               conv_w_in=conv_w_in, conv_w=conv_w, rw_mix=rw_mix,
               rw_w0=rw_w0, rw_w1=rw_w1, rw_w2=rw_w2, rw_a0=rw_a0, rw_a1=rw_a1, rw_a2=rw_a2,
               rw_g1=rw_g1, rw_g2=rw_g2, rw_kk=rw_kk, rw_ka=rw_ka, rw_rk=rw_rk, rw_wr=rw_wr,
               rw_wk=rw_wk, rw_wv=rw_wv, rw_lnw=rw_lnw, rw_lnb=rw_lnb)

    n_c = b + sb
    pad = (-n_c) % 8
    c_all = jnp.concatenate([c_prompt, c_sample, jnp.zeros((pad, d), F32)], axis=0)
    mod_all = _mod_call(c_all, mod_w, mod_b)
    mod_p = (mod_all[:, :b].reshape(depth, b, N_MOD, d).transpose(0, 2, 1, 3)
             .reshape(depth, N_MOD, b, 1, d))
    mod_s = mod_all[:, b:n_c].reshape(depth, sb, N_MOD, d).transpose(0, 2, 1, 3)

    y_p, conv_p, shift_p, wkv_p = _trunk(
        x_prompt, mod_p, None, None, None, wts, time_major=False,
        tile=min(t, 1024), out_tile=min(t, 512), prep_tile=min(t, 128), chunk=min(t, 64),
        scan_seqs=math.gcd(b, 2))
    y_s, conv_s, shift_s, wkv_s = _trunk(
        x_sample.transpose(1, 0, 2), mod_s, state_conv, state_shift, state_wkv, wts,
        time_major=True, tile=sb, out_tile=min(sb, 64), prep_tile=min(sb, 16), chunk=st,
        scan_seqs=math.gcd(sb, 2))
    return (y_p, y_s.transpose(1, 0, 2), conv_p, shift_p, wkv_p, conv_s, shift_s, wkv_s)
```

```python
import functools
import math

import jax
import jax.numpy as jnp
from jax import lax
from jax.experimental import pallas as pl
from jax.experimental.pallas import tpu as pltpu

F32 = jnp.float32
BF16 = jnp.bfloat16

RMS_EPS = 1e-6
GN_EPS = 64e-5
NORM_EPS = 1e-12
HALF_STEP = 0.5
HEAD_SIZE = 64
N_MOD = 9

MXU_WIDTH_V7X = 256
GROUP_LANES = MXU_WIDTH_V7X
HEADS_PER_GROUP = GROUP_LANES // HEAD_SIZE
VMEM_LIMIT_V7X = 60 * 2**20
ROW_PIECE = 256

NN = (((1,), (0,)), ((), ()))
NT = (((1,), (1,)), ((), ()))
TN = (((0,), (0,)), ((), ()))


def _mm(a, b, dims=NN):
    return lax.dot_general(a.astype(BF16), b.astype(BF16), dims, preferred_element_type=F32)


def _sigmoid(x):
    return 1.0 / (1.0 + jnp.exp(-x))


def _params(n_axes):
    return pltpu.CompilerParams(dimension_semantics=("arbitrary",) * n_axes,
                                vmem_limit_bytes=VMEM_LIMIT_V7X)


def _rms(x, g):
    return x * lax.rsqrt(jnp.mean(x * x, axis=-1, keepdims=True) + RMS_EPS) * g


def _modulated_pre(x, m_ref, g):
    return _rms(x, g) * (1.0 + m_ref[1]) + m_ref[0]


def _gated_post(x, y, m_ref, g, res_w):
    return x + (res_w * m_ref[2]) * _rms(y, g)


def _pieces(lead, rows):
    if lead == 1:
        n = min(ROW_PIECE, rows)
        return [(slice(0, 1), slice(r0, r0 + n)) for r0 in range(0, rows, n)]
    n = min(lead, max(1, ROW_PIECE // rows))
    return [(slice(a0, a0 + n), slice(0, rows)) for a0 in range(0, lead, n)]


def _piece_rows(x_ref, ls, rs):
    shp = x_ref[ls, rs, :].shape
    return shp, shp[0] * shp[1]


def _shift_time(u, k, fill, time_major):
    lead, rows, n = u.shape
    if time_major:
        return jnp.concatenate([fill, u[:lead - k]], axis=0)
    out = pltpu.roll(u.reshape(rows, n), k, 0)
    row = lax.broadcasted_iota(jnp.int32, (rows, n), 0)
    for t in range(k):
        out = jnp.where(row == t, fill[t:t + 1, :], out)
    return out.reshape(1, rows, n)


class _Layout:
    def __init__(self, shape, time_major, rows_per_tile):
        self.time_major = time_major
        self.shape = shape
        lead, rows, d = shape
        if time_major:
            self.block = (lead, rows_per_tile, d)
            self.tiles_per_seq = 1
            self.n_tiles = rows // rows_per_tile
        else:
            self.block = (1, rows_per_tile, d)
            self.tiles_per_seq = rows // rows_per_tile
            self.n_tiles = lead * self.tiles_per_seq
        self.tile_rows = self.block[0] * self.block[1]

    def xmap(self, i, *_):
        if self.time_major:
            return (0, i, 0)
        return (i // self.tiles_per_seq, i % self.tiles_per_seq, 0)

    def x_spec(self, **kw):
        return pl.BlockSpec(self.block, self.xmap, **kw)

    def mod_spec(self, mod, l, sub):
        d = self.shape[2]
        if self.time_major:
            return pl.BlockSpec((None, 3, self.block[1], d), lambda i, *_: (l, sub, i, 0))
        tps = self.tiles_per_seq
        return pl.BlockSpec((None, 3, None, 1, d), lambda i, *_: (l, sub, i // tps, 0, 0))


def _gain_spec(gains, l, sub):
    return pl.BlockSpec((None, None, 1, gains.shape[3]), lambda i, *_: (l, sub, 0, 0))


def _mod_kernel(c_ref, w_ref, b_ref, o_ref):
    c = c_ref[...]
    o_ref[...] = _mm(c * _sigmoid(c), w_ref[...]) + b_ref[...]


def _mod_call(c_all, mod_w, mod_b):
    depth, d, n = mod_w.shape
    nbp = c_all.shape[0]
    tn = math.gcd(n, 1024)
    return pl.pallas_call(
        _mod_kernel,
        grid=(depth, n // tn),
        in_specs=[
            pl.BlockSpec((nbp, d), lambda l, j: (0, 0)),
            pl.BlockSpec((None, d, tn), lambda l, j: (l, 0, j)),
            pl.BlockSpec((None, 1, tn), lambda l, j: (l, 0, j)),
        ],
        out_specs=pl.BlockSpec((None, nbp, tn), lambda l, j: (l, 0, j)),
        out_shape=jax.ShapeDtypeStruct((depth, nbp, n), F32),
        compiler_params=_params(2),
    )(c_all, mod_w, mod_b.reshape(depth, 1, n))


def _out_proj_kernel(*refs, res_w, emit_next):
    if emit_next:
        x_ref, m_ref, gpost_ref, a_ref, w_ref, mn_ref, gn_ref, o_ref, h_ref = refs
    else:
        x_ref, m_ref, gpost_ref, a_ref, w_ref, o_ref = refs
    lead, rows, _ = x_ref.shape
    for ls, rs in _pieces(lead, rows):
        shp, n = _piece_rows(x_ref, ls, rs)
        a = a_ref[ls, rs, :]
        y = lax.dot_general(a.reshape(n, a.shape[2]), w_ref[...], NN,
                            preferred_element_type=F32).reshape(shp)
        o = _gated_post(x_ref[ls, rs, :], y, m_ref, gpost_ref[...], res_w)
        o_ref[ls, rs, :] = o
        if emit_next:
            h_ref[ls, rs, :] = _modulated_pre(o, mn_ref, gn_ref[...]).astype(BF16)


def _out_proj_call(x3, lay, mod, norm_post, act, w, l, widx, sub, res_w, nxt=None):
    k, d = w.shape[2], w.shape[3]
    in_specs = [
        lay.x_spec(),
        lay.mod_spec(mod, l, sub),
        _gain_spec(norm_post, l, sub),
        pl.BlockSpec((lay.block[0], lay.block[1], k), lay.xmap),
        pl.BlockSpec((None, None, k, d), lambda i: (widx[0], widx[1], 0, 0),
                     pipeline_mode=pl.Buffered(1)),
    ]
    args = [x3, mod, norm_post, act, w]
    out_specs = [lay.x_spec()]
    out_shape = [jax.ShapeDtypeStruct(x3.shape, F32)]
    if nxt is not None:
        norm_pre, ln, subn = nxt
        in_specs += [lay.mod_spec(mod, ln, subn), _gain_spec(norm_pre, ln, subn)]
        args += [mod, norm_pre]
        out_specs.append(lay.x_spec())
        out_shape.append(jax.ShapeDtypeStruct(x3.shape, BF16))
    out = pl.pallas_call(
        functools.partial(_out_proj_kernel, res_w=res_w, emit_next=nxt is not None),
        grid=(lay.n_tiles,),
        in_specs=in_specs,
        out_specs=out_specs,
        out_shape=out_shape,
        compiler_params=_params(1),
    )(*args)
    return (out[0], out[1]) if nxt is not None else (out[0], None)


def _pre_kernel(x_ref, m_ref, g_ref, h_ref):
    h_ref[...] = _modulated_pre(x_ref[...], m_ref, g_ref[...]).astype(BF16)


def _pre_call(x3, lay, mod, norm_pre, l, sub):
    return pl.pallas_call(
        _pre_kernel,
        grid=(lay.n_tiles,),
        in_specs=[lay.x_spec(), lay.mod_spec(mod, l, sub), _gain_spec(norm_pre, l, sub)],
        out_specs=lay.x_spec(),
        out_shape=jax.ShapeDtypeStruct(x3.shape, BF16),
        compiler_params=_params(1),
    )(x3, mod, norm_pre)


def _cast_weights_once(i, pairs):
    @pl.when(i == 0)
    def _():
        for src, dst in pairs:
            dst[...] = src[...].astype(BF16)


def _ffn_in_kernel(h_ref, wg_ref, wu_ref, a_ref, wgb_ref, wub_ref):
    _cast_weights_once(pl.program_id(1), [(wg_ref, wgb_ref), (wu_ref, wub_ref)])
    lead, rows, d = h_ref.shape
    for ls, rs in _pieces(lead, rows):
        shp, n = _piece_rows(h_ref, ls, rs)
        h = h_ref[ls, rs, :].reshape(n, d)
        gt = lax.dot_general(h, wgb_ref[...], NN, preferred_element_type=F32)
        up = lax.dot_general(h, wub_ref[...], NN, preferred_element_type=F32)
        a_ref[ls, rs, :] = (gt * _sigmoid(gt) * up).astype(BF16).reshape(shp[0], shp[1], -1)


def _ffn_in_call(h3, lay, w_in, l, s, tf):
    lead, rows, d = h3.shape
    f = w_in.shape[3] // 2
    nj = f // tf
    return pl.pallas_call(
        _ffn_in_kernel,
        grid=(nj, lay.n_tiles),
        in_specs=[
            pl.BlockSpec(lay.block, lambda j, i: lay.xmap(i)),
            pl.BlockSpec((None, None, d, tf), lambda j, i: (l, s, 0, j)),
            pl.BlockSpec((None, None, d, tf), lambda j, i: (l, s, 0, nj + j)),
        ],
        out_specs=pl.BlockSpec((lay.block[0], lay.block[1], tf),
                               lambda j, i: lay.xmap(i)[:2] + (j,)),
        out_shape=jax.ShapeDtypeStruct((lead, rows, f), BF16),
        scratch_shapes=[pltpu.VMEM((d, tf), BF16), pltpu.VMEM((d, tf), BF16)],
        compiler_params=_params(2),
    )(h3, w_in, w_in)


def _conv_kernel(*refs, time_major, tiles_per_seq):
    if time_major:
        (h_ref, wb_ref, wc_ref, wx_ref, cw_ref, st_ref,
         o_ref, so_ref, wbb_ref, wcb_ref, wxb_ref) = refs
    else:
        (h_ref, wb_ref, wc_ref, wx_ref, cw_ref,
         o_ref, so_ref, wbb_ref, wcb_ref, wxb_ref, carry_ref) = refs
    i = pl.program_id(1)
    _cast_weights_once(i, [(wb_ref, wbb_ref), (wc_ref, wcb_ref), (wx_ref, wxb_ref)])
    lead, rows, d = h_ref.shape
    tm = lead * rows

    h = h_ref[...].reshape(tm, d)

    def proj(w_ref):
        return lax.dot_general(h, w_ref[...], NN, preferred_element_type=F32)

    bg = proj(wbb_ref)
    u2 = proj(wcb_ref) * proj(wxb_ref)
    tn = u2.shape[1]
    u = u2.reshape(lead, rows, tn)
    if time_major:
        st = st_ref[...]
        fill1, fill2 = st[1:2], st
        so_ref[...] = u[lead - 2:lead]
    else:
        @pl.when(i % tiles_per_seq == 0)
        def _():
            carry_ref[...] = jnp.zeros((8, tn), F32)
        prev = carry_ref[...]
        fill1, fill2 = prev[7:8, :], prev[6:8, :]
        carry_ref[...] = u2[tm - 8:tm, :]
        so_ref[...] = u[:, rows - 2:rows, :]
    cw = cw_ref[...]
    z = (cw[0:1, :] * _shift_time(u, 2, fill2, time_major)
         + cw[1:2, :] * _shift_time(u, 1, fill1, time_major) + cw[2:3, :] * u)
    o_ref[...] = (bg.reshape(lead, rows, tn) * z).astype(BF16)


def _conv_call(h3, lay, w_in, cw, state, jl, tn):
    lead_n, rows_n, d = h3.shape
    dc = cw.shape[2]
    nj = dc // tn
    in_specs = [
        pl.BlockSpec(lay.block, lambda j, i: lay.xmap(i)),
        pl.BlockSpec((None, d, tn), lambda j, i: (jl, 0, j)),
        pl.BlockSpec((None, d, tn), lambda j, i: (jl, 0, nj + j)),
        pl.BlockSpec((None, d, tn), lambda j, i: (jl, 0, 2 * nj + j)),
        pl.BlockSpec((None, cw.shape[1], tn), lambda j, i: (jl, 0, j)),
    ]
    args = [h3, w_in, w_in, w_in, cw]
    scratch = [pltpu.VMEM((d, tn), BF16)] * 3
    if lay.time_major:
        nb = lay.block[1]
        in_specs.append(pl.BlockSpec((2, nb, tn), lambda j, i: (0, i, j)))
        args.append(state)
        so_spec = pl.BlockSpec((2, nb, tn), lambda j, i: (0, i, j))
        so_shape = jax.ShapeDtypeStruct((2, h3.shape[1], dc), F32)
    else:
        scratch.append(pltpu.VMEM((8, tn), F32))
        so_spec = pl.BlockSpec((1, 2, tn), lambda j, i: (i, 0, j))
        so_shape = jax.ShapeDtypeStruct((lay.n_tiles, 2, dc), F32)
    return pl.pallas_call(
        functools.partial(_conv_kernel, time_major=lay.time_major,
                          tiles_per_seq=lay.tiles_per_seq),
        grid=(nj, lay.n_tiles),
        in_specs=in_specs,
        out_specs=[pl.BlockSpec((lay.block[0], lay.block[1], tn),
                                lambda j, i: lay.xmap(i)[:2] + (j,)), so_spec],
        out_shape=[jax.ShapeDtypeStruct((lead_n, rows_n, dc), BF16), so_shape],
        scratch_shapes=scratch,
        compiler_params=_params(2),
    )(*args)


def _rw_prep_kernel(*refs, time_major, tiles_per_seq):
    if time_major:
        (x_ref, m_ref, gpre_ref, mix_ref, w1_ref, a1_ref, g1_ref, st_ref,
         xr_ref, xk_ref, xv_ref, tw_ref, ta_ref, tg_ref, so_ref) = refs
    else:
        (x_ref, m_ref, gpre_ref, mix_ref, w1_ref, a1_ref, g1_ref,
         xr_ref, xk_ref, xv_ref, tw_ref, ta_ref, tg_ref, so_ref, carry_ref) = refs
    i = pl.program_id(0)
    lead, rows, d = x_ref.shape
    n = lead * rows
    h = _modulated_pre(x_ref[...], m_ref, gpre_ref[...])
    if time_major:
        fill = st_ref[...][None]
        so_ref[...] = h[lead - 1]
    else:
        @pl.when(i % tiles_per_seq == 0)
        def _():
            carry_ref[...] = jnp.zeros((8, d), F32)
        fill = carry_ref[7:8, :]
        carry_ref[...] = h[0, rows - 8:rows, :]
        so_ref[...] = h[:, rows - 1:rows, :]
    xx = _shift_time(h, 1, fill, time_major) - h
    mix = mix_ref[...]

    def mixed(k):
        return h + xx * mix[k:k + 1, :]

    def low_rank(k, w_ref):
        return _mm(mixed(k).reshape(n, d), w_ref[...]).reshape(lead, rows, w_ref.shape[1])

    xr_ref[...] = mixed(0).astype(BF16)
    tw_ref[...] = jnp.tanh(low_rank(1, w1_ref))
    xk_ref[...] = mixed(2).astype(BF16)
    xv_ref[...] = mixed(3).astype(BF16)
    ta_ref[...] = low_rank(4, a1_ref)
    tg_ref[...] = _sigmoid(low_rank(5, g1_ref))


def _rw_prep_call(x3, lay, mod, norm_pre, mix, w1, a1, g1, state, l, jl, sub):
    lead, rows, d = x3.shape
    dl, dg = w1.shape[2], g1.shape[2]
    in_specs = [
        lay.x_spec(),
        lay.mod_spec(mod, l, sub),
        _gain_spec(norm_pre, l, sub),
        pl.BlockSpec((None, mix.shape[1], d), lambda i: (jl, 0, 0)),
        pl.BlockSpec((None, d, dl), lambda i: (jl, 0, 0)),
        pl.BlockSpec((None, d, dl), lambda i: (jl, 0, 0)),
        pl.BlockSpec((None, d, dg), lambda i: (jl, 0, 0)),
    ]
    args = [x3, mod, norm_pre, mix, w1, a1, g1]
    scratch = []
    blk = lay.block
    if lay.time_major:
        in_specs.append(pl.BlockSpec((blk[1], d), lambda i: (i, 0)))
        args.append(state)
        so_spec = pl.BlockSpec((blk[1], d), lambda i: (i, 0))
        so_shape = jax.ShapeDtypeStruct((rows, d), F32)
    else:
        scratch.append(pltpu.VMEM((8, d), F32))
        tps = lay.tiles_per_seq
        so_spec = pl.BlockSpec((1, 1, d), lambda i: (i // tps, 0, 0))
        so_shape = jax.ShapeDtypeStruct((lead, 1, d), F32)

    def ospec(width):
        return pl.BlockSpec((blk[0], blk[1], width), lay.xmap)

    return pl.pallas_call(
        functools.partial(_rw_prep_kernel, time_major=lay.time_major,
                          tiles_per_seq=lay.tiles_per_seq),
        grid=(lay.n_tiles,),
        in_specs=in_specs,
        out_specs=[ospec(d)] * 3 + [ospec(dl), ospec(dl), ospec(dg), so_spec],
        out_shape=[jax.ShapeDtypeStruct((lead, rows, d), BF16)] * 3
        + [jax.ShapeDtypeStruct((lead, rows, dl), F32)] * 2
        + [jax.ShapeDtypeStruct((lead, rows, dg), F32), so_shape],
        scratch_shapes=scratch,
        compiler_params=_params(1),
    )(*args)


def _rw_proj_kernel(xr_ref, xk_ref, xv_ref, tw_ref, ta_ref, tg_ref, wr_ref, wk_ref, wv_ref,
                    w2_ref, a2_ref, g2_ref, w0_ref, a0_ref,
                    r_ref, k_ref, v_ref, w_ref, a_ref, g_ref):
    r_ref[...] = _mm(xr_ref[...], wr_ref[...])
    k_ref[...] = _mm(xk_ref[...], wk_ref[...])
    v_ref[...] = _mm(xv_ref[...], wv_ref[...])
    w_ref[...] = w0_ref[...] + _mm(tw_ref[...], w2_ref[...])
    a_ref[...] = _sigmoid(a0_ref[...] + _mm(ta_ref[...], a2_ref[...]))
    g_ref[...] = _mm(tg_ref[...], g2_ref[...])


def _rw_proj_call(xr, xk, xv, tw, ta, tg, wr, wk, wv, w2, a2, g2, w0, a0, jl, tm, tn):
    m, d = xr.shape
    dl, dg = tw.shape[1], tg.shape[1]
    nr = w0.shape[0]
    xspec = pl.BlockSpec((tm, d), lambda i, j: (i, 0))
    wspec = pl.BlockSpec((None, d, tn), lambda i, j: (jl, 0, j))
    vspec = pl.BlockSpec((None, 1, tn), lambda i, j: (jl, 0, j))
    ospec = pl.BlockSpec((tm, tn), lambda i, j: (i, j))
    return pl.pallas_call(
        _rw_proj_kernel,
        grid=(m // tm, d // tn),
        in_specs=[xspec, xspec, xspec,
                  pl.BlockSpec((tm, dl), lambda i, j: (i, 0)),
                  pl.BlockSpec((tm, dl), lambda i, j: (i, 0)),
                  pl.BlockSpec((tm, dg), lambda i, j: (i, 0)),
                  wspec, wspec, wspec,
                  pl.BlockSpec((None, dl, tn), lambda i, j: (jl, 0, j)),
                  pl.BlockSpec((None, dl, tn), lambda i, j: (jl, 0, j)),
                  pl.BlockSpec((None, dg, tn), lambda i, j: (jl, 0, j)),
                  vspec, vspec],
        out_specs=[ospec] * 6,
        out_shape=[jax.ShapeDtypeStruct((m, d), F32)] * 6,
        compiler_params=_params(2),
    )(xr, xk, xv, tw, ta, tg, wr, wk, wv, w2, a2, g2,
      w0.reshape(nr, 1, d), a0.reshape(nr, 1, d))


def _iota2(shape, axis):
    return lax.broadcasted_iota(jnp.int32, shape, axis)


def _same_block(shape, row_block, lane_block):
    r = lax.shift_right_logical(_iota2(shape, 0), int(math.log2(row_block)))
    c = lax.shift_right_logical(_iota2(shape, 1), int(math.log2(lane_block)))
    return r == c


def _bd_rows(x, mask):
    return jnp.where(mask, jnp.concatenate([x] * HEADS_PER_GROUP, axis=0), 0.0)


def _split_mm(x, ones):
    hi = x.astype(BF16)
    lo = (x - hi.astype(F32)).astype(BF16)
    return (lax.dot_general(hi, ones, NN, preferred_element_type=F32)
            + lax.dot_general(lo, ones, NN, preferred_element_type=F32))


def _each(fn, *lists):
    return [fn(*xs) for xs in zip(*lists)]


def _wkv_chunk(r, k, v, a_, b_, lw, cum, s_bd, c):
    hs = HEAD_SIZE
    tlanes = HEADS_PER_GROUP * c
    m_ch = _same_block((tlanes, GROUP_LANES), c, hs)
    m_tt = _same_block((tlanes, tlanes), c, c)
    m_ss = _same_block((GROUP_LANES, GROUP_LANES), hs, hs)
    t_row = _iota2((c, tlanes), 0)
    s_lane = jnp.bitwise_and(_iota2((c, tlanes), 1), c - 1)
    strict = s_lane < t_row
    incl = s_lane <= t_row

    def bd_ch(x):
        return _bd_rows(x, m_ch)

    def bd_tt(x):
        return _bd_rows(x, m_tt)

    def apply(a, x):
        return _mm(a, bd_ch(x))

    cl = _each(lambda x: x[c - 1:c, :], cum)
    at = _each(lambda a, x, l: a * jnp.exp(x - l), a_, cum, lw)
    rt = _each(lambda a, x: a * jnp.exp(x), r, cum)
    bt = _each(lambda a, x: a * jnp.exp(-x), b_, cum)
    kt = _each(lambda a, x: a * jnp.exp(-x), k, cum)
    bh = _each(lambda a, x, xl: a * jnp.exp(xl - x), b_, cum, cl)
    kh = _each(lambda a, x, xl: a * jnp.exp(xl - x), k, cum, cl)

    lhs = _each(lambda a, b: jnp.concatenate([a, b], axis=0), at, rt)
    ob = _each(lambda a, x: _mm(a, bd_ch(x), NT), lhs, bt)
    ok = _each(lambda a, x: _mm(a, bd_ch(x), NT), lhs, kt)
    a_ab = _each(lambda x: jnp.where(strict, x[:c], 0.0), ob)
    a_rb = _each(lambda x: jnp.where(incl, x[c:], 0.0), ob)
    a_ak = _each(lambda x: jnp.where(strict, x[:c], 0.0), ok)
    a_rk = _each(lambda x: jnp.where(incl, x[c:], 0.0), ok)
    akv = _each(lambda a, b, x: apply(jnp.concatenate([a, b], axis=0), x), a_ak, a_rk, v)
    av = _each(lambda x: x[:c], akv)
    y_kv = _each(lambda x: x[c:], akv)

    eye = jnp.where(s_lane == t_row, 1.0, 0.0)
    tinv = _each(lambda x: eye + x, a_ab)
    npow = a_ab
    for _ in range(int(math.log2(c)) - 1):
        npow = _each(lambda x: _mm(x, bd_tt(x)), npow)
        tinv = _each(lambda t, x: t + _mm(t, bd_tt(x)), tinv, npow)

    wt = _each(apply, tinv, at)
    w = _each(apply, tinv, av)
    qt = _each(lambda x, a, y: x + apply(a, y), rt, a_rb, wt)
    yi = _each(lambda a, x, y: apply(a, x) + y, a_rb, w, y_kv)
    g_off = _each(lambda x, y: jnp.where(m_ss, _mm(x, y, TN), 0.0), wt, bh)
    h_t = _each(lambda x, y, p, q: jnp.where(
        m_ss, _mm(jnp.concatenate([x, p], axis=0), jnp.concatenate([y, q], axis=0), TN), 0.0),
                w, bh, v, kh)

    y = _each(lambda q, s, x: _mm(q, s, NT) + x, qt, s_bd, yi)
    s_new = _each(lambda s, xl, g, h: s * jnp.exp(xl) + _mm(s, g) + h, s_bd, cl, g_off, h_t)
    return y, s_new


def _scan_kernel(*refs, c, d, nq, has_state, lane_packed):
    if has_state:
        (r_ref, k_ref, v_ref, w_ref, a_ref, g_ref, p_ref, s0_ref, y_ref, so_ref, sbd_ref) = refs
    else:
        (r_ref, k_ref, v_ref, w_ref, a_ref, g_ref, p_ref, y_ref, so_ref, sbd_ref) = refs
    ci = pl.program_id(1)
    hs = HEAD_SIZE
    ng = d // GROUP_LANES
    chains = [(q, g) for q in range(nq) for g in range(ng)]
    n_ch = len(chains)
    m_ss = _same_block((GROUP_LANES, GROUP_LANES), hs, hs)
    ones_bd = jnp.where(m_ss, 1.0, 0.0).astype(BF16)

    def lanes(g):
        return slice(g * GROUP_LANES, (g + 1) * GROUP_LANES)

    def rd(ref, q, g):
        if lane_packed:
            return ref[:, q * d + g * GROUP_LANES:q * d + (g + 1) * GROUP_LANES]
        return ref[q, :, lanes(g)]

    def par(row, g):
        return p_ref[row:row + 1, lanes(g)]

    @pl.when(ci == 0)
    def _():
        for i, (q, g) in enumerate(chains):
            if has_state:
                heads = [s0_ref[q, HEADS_PER_GROUP * g + h] for h in range(HEADS_PER_GROUP)]
                sbd_ref[i] = _bd_rows(jnp.concatenate(heads, axis=1), m_ss)
            else:
                sbd_ref[i] = jnp.zeros((GROUP_LANES, GROUP_LANES), F32)

    def log_decay(wq):
        z = -wq
        softplus = jnp.maximum(z, 0.0) + jnp.log(1.0 + jnp.exp(-jnp.abs(z)))
        lw = -jnp.exp(-softplus - 0.5)
        row = _iota2(lw.shape, 0)
        cum = lw
        step = 1
        while step < c:
            cum = cum + jnp.where(row >= step, pltpu.roll(cum, step, 0), 0.0)
            step *= 2
        return lw, cum

    if lane_packed:
        lw_all, cum_all = log_decay(w_ref[...])
        lw = [lw_all[:, q * d + g * GROUP_LANES:q * d + (g + 1) * GROUP_LANES] for q, g in chains]
        cum = [cum_all[:, q * d + g * GROUP_LANES:q * d + (g + 1) * GROUP_LANES] for q, g in chains]
    else:
        per_q = [log_decay(w_ref[q]) for q in range(nq)]
        lw = [per_q[q][0][:, lanes(g)] for q, g in chains]
        cum = [per_q[q][1][:, lanes(g)] for q, g in chains]

    def seg_sum(xs):
        y = _split_mm(jnp.concatenate(xs, axis=0), ones_bd)
        return [y[i * c:(i + 1) * c] for i in range(len(xs))]

    r = [rd(r_ref, q, g) for q, g in chains]
    k = [rd(k_ref, q, g) for q, g in chains]
    v = [rd(v_ref, q, g) for q, g in chains]
    a_sig = [rd(a_ref, q, g) for q, g in chains]
    kk = [x * par(0, g) for x, (q, g) in zip(k, chains)]
    k = [x * (1.0 + (a - 1.0) * par(1, g)) for x, a, (q, g) in zip(k, a_sig, chains)]
    sums = seg_sum(_each(lambda x: x * x, kk)
                   + [x * y * par(2, g) for x, y, (q, g) in zip(r, k, chains)])
    kk = _each(lambda x, n: x / jnp.maximum(jnp.sqrt(n), NORM_EPS), kk, sums[:n_ch])
    bonus = _each(lambda s, z: s * z, sums[n_ch:], v)
    y, s_new = _wkv_chunk(r, k, v, _each(lambda x: -x, kk), _each(lambda x, a: x * a, kk, a_sig),
                          lw, cum, [sbd_ref[i] for i in range(n_ch)], c)
    for i in range(n_ch):
        sbd_ref[i] = s_new[i]

    mu = _each(lambda x: x * (1.0 / hs), seg_sum(y))
    dy = _each(lambda x, m: x - m, y, mu)
    var = _each(lambda x: x * (1.0 / hs), seg_sum(_each(lambda x: x * x, dy)))
    for i, (q, g) in enumerate(chains):
        yn = dy[i] * lax.rsqrt(var[i] + GN_EPS) * par(3, g) + par(4, g)
        out = ((yn + bonus[i]) * rd(g_ref, q, g)).astype(BF16)
        if lane_packed:
            y_ref[:, q * d + g * GROUP_LANES:q * d + (g + 1) * GROUP_LANES] = out
        else:
            y_ref[q, :, lanes(g)] = out

    @pl.when(ci == pl.num_programs(1) - 1)
    def _():
        for i, (q, g) in enumerate(chains):
            s = sbd_ref[i]
            s = s[0:hs] + s[hs:2 * hs] + s[2 * hs:3 * hs] + s[3 * hs:4 * hs]
            for h in range(HEADS_PER_GROUP):
                so_ref[q, HEADS_PER_GROUP * g + h] = s[:, h * hs:(h + 1) * hs]


def _scan_call(r, k, v, w, a, g, p, s0, nb, t, c, nq, time_major):
    m, d = r.shape
    nc = t // c
    ng = d // GROUP_LANES
    nh = d // HEAD_SIZE
    has_state = s0 is not None
    if time_major:
        r, k, v, w, a, g = (z.reshape(t, nb * d) for z in (r, k, v, w, a, g))
        xspec = pl.BlockSpec((c, nq * d), lambda b, ci: (ci, b))
    else:
        r, k, v, w, a, g = (z.reshape(nb, t, d) for z in (r, k, v, w, a, g))
        xspec = pl.BlockSpec((nq, c, d), lambda b, ci: (b, ci, 0))
    sspec = pl.BlockSpec((nq, nh, HEAD_SIZE, HEAD_SIZE), lambda b, ci: (b, 0, 0, 0))
    in_specs = [xspec] * 6 + [pl.BlockSpec(p.shape, lambda b, ci: (0, 0))]
    args = [r, k, v, w, a, g, p]
    if has_state:
        in_specs.append(sspec)
        args.append(s0)
    y, s_fin = pl.pallas_call(
        functools.partial(_scan_kernel, c=c, d=d, nq=nq, has_state=has_state,
                          lane_packed=time_major),
        grid=(nb // nq, nc),
        in_specs=in_specs,
        out_specs=[xspec, sspec],
        out_shape=[jax.ShapeDtypeStruct(r.shape, BF16),
                   jax.ShapeDtypeStruct((nb, nh, HEAD_SIZE, HEAD_SIZE), F32)],
        scratch_shapes=[pltpu.VMEM((nq * ng, GROUP_LANES, GROUP_LANES), F32)],
        compiler_params=_params(2),
    )(*args)
    return y.reshape(m, d), s_fin


def _trunk(x3, mod, conv_state, shift_state, wkv_state, wts, *, time_major, tile, out_tile,
           prep_tile, chunk, scan_seqs):
    d = x3.shape[2]
    nb, t = (x3.shape[1], x3.shape[0]) if time_major else (x3.shape[0], x3.shape[1])
    depth = mod.shape[0]
    lay = _Layout(x3.shape, time_major, tile)
    out_lay = _Layout(x3.shape, time_major, out_tile)
    prep_lay = _Layout(x3.shape, time_major, prep_tile)
    m = nb * t
    new_conv, new_shift, new_wkv = [], [], []

    subs = [(l, sub) for l in range(depth) for sub in range(3)]

    def takes_h(l, sub):
        return not (sub == 1 and l % 2 == 1)

    def nxt_of(l, sub):
        i = subs.index((l, sub)) + 1
        if i < len(subs) and takes_h(*subs[i]):
            return (wts["norm_pre"],) + subs[i]
        return None

    def ffn(x3, h3, l, s, sub):
        act = _ffn_in_call(h3, lay, wts["ffn_w_in"], l, s, 512)
        return _out_proj_call(x3, out_lay, mod, wts["norm_post"], act, wts["ffn_w_out"],
                              l, (l, s), sub, HALF_STEP, nxt_of(l, sub))

    h3 = _pre_call(x3, prep_lay, mod, wts["norm_pre"], 0, 0)
    for l in range(depth):
        jl = l // 2
        x3, h3 = ffn(x3, h3, l, 0, 0)
        if l % 2 == 0:
            st = conv_state[jl].transpose(1, 0, 2) if time_major else None
            bz, cs = _conv_call(h3, lay, wts["conv_w_in"], wts["conv_w"], st, jl, 256)
            x3, h3 = _out_proj_call(x3, out_lay, mod, wts["norm_post"], bz, wts["conv_w_out"],
                                    l, (jl, 0), 1, 1.0, nxt_of(l, 1))
            if time_major:
                new_conv.append(cs.transpose(1, 0, 2))
            else:
                tps = lay.tiles_per_seq
                new_conv.append(cs[tps - 1::tps])
        else:
            st = shift_state[jl] if time_major else None
            xr, xk, xv, tw, ta, tg, ss = _rw_prep_call(
                x3, prep_lay, mod, wts["norm_pre"], wts["rw_mix"], wts["rw_w1"], wts["rw_a1"],
                wts["rw_g1"], st, l, jl, 1)
            flat = lambda z: z.reshape(m, z.shape[2])
            r, k, v, w, a, g = _rw_proj_call(
                flat(xr), flat(xk), flat(xv), flat(tw), flat(ta), flat(tg),
                wts["rw_wr"], wts["rw_wk"], wts["rw_wv"], wts["rw_w2"], wts["rw_a2"],
                wts["rw_g2"], wts["rw_w0"], wts["rw_a0"], jl, lay.tile_rows, 256)
            p = jnp.concatenate([
                wts["rw_kk"][jl][None], wts["rw_ka"][jl][None], wts["rw_rk"][jl].reshape(1, d),
                wts["rw_lnw"][jl][None], wts["rw_lnb"][jl][None],
                jnp.zeros((3, d), F32)], axis=0)
            s0 = wkv_state[jl] if time_major else None
            yg, s_fin = _scan_call(r, k, v, w, a, g, p, s0, nb, t, chunk, scan_seqs, time_major)
            x3, h3 = _out_proj_call(x3, out_lay, mod, wts["norm_post"], yg.reshape(x3.shape),
                                    wts["rw_wo"], l, (jl, 0), 1, 1.0, nxt_of(l, 1))
            new_shift.append(ss.reshape(nb, d))
            new_wkv.append(s_fin)
        x3, h3 = ffn(x3, h3, l, 1, 2)
    return x3, jnp.stack(new_conv), jnp.stack(new_shift), jnp.stack(new_wkv)


def kernel(x_prompt, x_sample, state_conv, state_shift, state_wkv, c_prompt, c_sample, mod_w, mod_b, norm_pre, norm_post, ffn_w_in, ffn_w_out, conv_w_in, conv_w, conv_w_out, rw_mix, rw_w0, rw_w1, rw_w2, rw_a0, rw_a1, rw_a2, rw_g1, rw_g2, rw_kk, rw_ka, rw_rk, rw_wr, rw_wk, rw_wv, rw_wo, rw_lnw, rw_lnb):
    b, t, d = x_prompt.shape
    sb, st, _ = x_sample.shape
    depth = mod_w.shape[0]
    n_sub = norm_pre.shape[1]
    wts = dict(norm_pre=norm_pre.reshape(depth, n_sub, 1, d),
               norm_post=norm_post.reshape(depth, n_sub, 1, d), ffn_w_in=ffn_w_in,
               ffn_w_out=ffn_w_out.astype(BF16),
               conv_w_out=conv_w_out.astype(BF16)[:, None],
               rw_wo=rw_wo.astype(BF16)[:, None],
               conv_w_in=conv_w_in, conv_w=conv_w, rw_mix=rw_mix,
               rw_w0=rw_w0, rw_w1=rw_w1, rw_w2=rw_w2, rw_a0=rw_a0, rw_a1=rw_a1, rw_a2=rw_a2,
               rw_g1=rw_g1, rw_g2=rw_g2, rw_kk=rw_kk, rw_ka=rw_ka, rw_rk=rw_rk, rw_wr=rw_wr,
               rw_wk=rw_wk, rw_wv=rw_wv, rw_lnw=rw_lnw, rw_lnb=rw_lnb)

    n_c = b + sb
    pad = (-n_c) % 8
    c_all = jnp.concatenate([c_prompt, c_sample, jnp.zeros((pad, d), F32)], axis=0)
    mod_all = _mod_call(c_all, mod_w, mod_b)
    mod_p = (mod_all[:, :b].reshape(depth, b, N_MOD, d).transpose(0, 2, 1, 3)
             .reshape(depth, N_MOD, b, 1, d))
    mod_s = mod_all[:, b:n_c].reshape(depth, sb, N_MOD, d).transpose(0, 2, 1, 3)

    y_p, conv_p, shift_p, wkv_p = _trunk(
        x_prompt, mod_p, None, None, None, wts, time_major=False,
        tile=min(t, 1024), out_tile=min(t, 512), prep_tile=min(t, 128), chunk=min(t, 64),
        scan_seqs=math.gcd(b, 2))
    y_s, conv_s, shift_s, wkv_s = _trunk(
        x_sample.transpose(1, 0, 2), mod_s, state_conv, state_shift, state_wkv, wts,
        time_major=True, tile=sb, out_tile=min(sb, 32), prep_tile=min(sb, 16), chunk=st,
        scan_seqs=math.gcd(sb, 2))
    return (y_p, y_s.transpose(1, 0, 2), conv_p, shift_p, wkv_p, conv_s, shift_s, wkv_s)
```

```python
import functools
import math

import jax
import jax.numpy as jnp
from jax import lax
from jax.experimental import pallas as pl
from jax.experimental.pallas import tpu as pltpu

F32 = jnp.float32
BF16 = jnp.bfloat16

RMS_EPS = 1e-6
GN_EPS = 64e-5
NORM_EPS = 1e-12
HALF_STEP = 0.5
HEAD_SIZE = 64
N_MOD = 9

MXU_WIDTH_V7X = 256
GROUP_LANES = MXU_WIDTH_V7X
HEADS_PER_GROUP = GROUP_LANES // HEAD_SIZE
VMEM_LIMIT_V7X = 60 * 2**20
ROW_PIECE = 256

NN = (((1,), (0,)), ((), ()))
NT = (((1,), (1,)), ((), ()))
TN = (((0,), (0,)), ((), ()))


def _mm(a, b, dims=NN):
    return lax.dot_general(a.astype(BF16), b.astype(BF16), dims, preferred_element_type=F32)


def _sigmoid(x):
    return 1.0 / (1.0 + jnp.exp(-x))


def _params(n_axes):
    return pltpu.CompilerParams(dimension_semantics=("arbitrary",) * n_axes,
                                vmem_limit_bytes=VMEM_LIMIT_V7X)


def _rms(x, g):
    return x * lax.rsqrt(jnp.mean(x * x, axis=-1, keepdims=True) + RMS_EPS) * g


def _modulated_pre(x, m_ref, g):
    return _rms(x, g) * (1.0 + m_ref[1]) + m_ref[0]


def _gated_post(x, y, m_ref, g, res_w):
    return x + (res_w * m_ref[2]) * _rms(y, g)


def _pieces(lead, rows):
    if lead == 1:
        n = min(ROW_PIECE, rows)
        return [(slice(0, 1), slice(r0, r0 + n)) for r0 in range(0, rows, n)]
    n = min(lead, max(1, ROW_PIECE // rows))
    return [(slice(a0, a0 + n), slice(0, rows)) for a0 in range(0, lead, n)]


def _piece_rows(x_ref, ls, rs):
    shp = x_ref[ls, rs, :].shape
    return shp, shp[0] * shp[1]


def _shift_time(u, k, fill, time_major):
    lead, rows, n = u.shape
    if time_major:
        return jnp.concatenate([fill, u[:lead - k]], axis=0)
    out = pltpu.roll(u.reshape(rows, n), k, 0)
    row = lax.broadcasted_iota(jnp.int32, (rows, n), 0)
    for t in range(k):
        out = jnp.where(row == t, fill[t:t + 1, :], out)
    return out.reshape(1, rows, n)


class _Layout:
    def __init__(self, shape, time_major, rows_per_tile):
        self.time_major = time_major
        self.shape = shape
        lead, rows, d = shape
        if time_major:
            self.block = (lead, rows_per_tile, d)
            self.tiles_per_seq = 1
            self.n_tiles = rows // rows_per_tile
        else:
            self.block = (1, rows_per_tile, d)
            self.tiles_per_seq = rows // rows_per_tile
            self.n_tiles = lead * self.tiles_per_seq
        self.tile_rows = self.block[0] * self.block[1]

    def xmap(self, i, *_):
        if self.time_major:
            return (0, i, 0)
        return (i // self.tiles_per_seq, i % self.tiles_per_seq, 0)

    def x_spec(self, **kw):
        return pl.BlockSpec(self.block, self.xmap, **kw)

    def mod_spec(self, mod, l, sub):
        d = self.shape[2]
        if self.time_major:
            return pl.BlockSpec((None, 3, self.block[1], d), lambda i, *_: (l, sub, i, 0))
        tps = self.tiles_per_seq
        return pl.BlockSpec((None, 3, None, 1, d), lambda i, *_: (l, sub, i // tps, 0, 0))


def _gain_spec(gains, l, sub):
    return pl.BlockSpec((None, None, 1, gains.shape[3]), lambda i, *_: (l, sub, 0, 0))


def _mod_kernel(c_ref, w_ref, b_ref, o_ref):
    c = c_ref[...]
    o_ref[...] = _mm(c * _sigmoid(c), w_ref[...]) + b_ref[...]


def _mod_call(c_all, mod_w, mod_b):
    depth, d, n = mod_w.shape
    nbp = c_all.shape[0]
    tn = math.gcd(n, 1024)
    return pl.pallas_call(
        _mod_kernel,
        grid=(depth, n // tn),
        in_specs=[
            pl.BlockSpec((nbp, d), lambda l, j: (0, 0)),
            pl.BlockSpec((None, d, tn), lambda l, j: (l, 0, j)),
            pl.BlockSpec((None, 1, tn), lambda l, j: (l, 0, j)),
        ],
        out_specs=pl.BlockSpec((None, nbp, tn), lambda l, j: (l, 0, j)),
        out_shape=jax.ShapeDtypeStruct((depth, nbp, n), F32),
        compiler_params=_params(2),
    )(c_all, mod_w, mod_b.reshape(depth, 1, n))


def _out_proj_kernel(*refs, res_w, emit_next):
    if emit_next:
        x_ref, m_ref, gpost_ref, a_ref, w_ref, mn_ref, gn_ref, o_ref, h_ref = refs
    else:
        x_ref, m_ref, gpost_ref, a_ref, w_ref, o_ref = refs
    lead, rows, _ = x_ref.shape
    for ls, rs in _pieces(lead, rows):
        shp, n = _piece_rows(x_ref, ls, rs)
        a = a_ref[ls, rs, :]
        y = lax.dot_general(a.reshape(n, a.shape[2]), w_ref[...], NN,
                            preferred_element_type=F32).reshape(shp)
        o = _gated_post(x_ref[ls, rs, :], y, m_ref, gpost_ref[...], res_w)
        o_ref[ls, rs, :] = o
        if emit_next:
            h_ref[ls, rs, :] = _modulated_pre(o, mn_ref, gn_ref[...]).astype(BF16)


def _out_proj_call(x3, lay, mod, norm_post, act, w, l, widx, sub, res_w, nxt=None):
    k, d = w.shape[2], w.shape[3]
    in_specs = [
        lay.x_spec(),
        lay.mod_spec(mod, l, sub),
        _gain_spec(norm_post, l, sub),
        pl.BlockSpec((lay.block[0], lay.block[1], k), lay.xmap),
        pl.BlockSpec((None, None, k, d), lambda i: (widx[0], widx[1], 0, 0),
                     pipeline_mode=pl.Buffered(1)),
    ]
    args = [x3, mod, norm_post, act, w]
    out_specs = [lay.x_spec()]
    out_shape = [jax.ShapeDtypeStruct(x3.shape, F32)]
    if nxt is not None:
        norm_pre, ln, subn = nxt
        in_specs += [lay.mod_spec(mod, ln, subn), _gain_spec(norm_pre, ln, subn)]
        args += [mod, norm_pre]
        out_specs.append(lay.x_spec())
        out_shape.append(jax.ShapeDtypeStruct(x3.shape, BF16))
    out = pl.pallas_call(
        functools.partial(_out_proj_kernel, res_w=res_w, emit_next=nxt is not None),
        grid=(lay.n_tiles,),
        in_specs=in_specs,
        out_specs=out_specs,
        out_shape=out_shape,
        compiler_params=_params(1),
    )(*args)
    return (out[0], out[1]) if nxt is not None else (out[0], None)


def _pre_kernel(x_ref, m_ref, g_ref, h_ref):
    lead, rows, _ = x_ref.shape
    for ls, rs in _pieces(lead, rows):
        h_ref[ls, rs, :] = _modulated_pre(x_ref[ls, rs, :], m_ref, g_ref[...]).astype(BF16)


def _pre_call(x3, lay, mod, norm_pre, l, sub):
    return pl.pallas_call(
        _pre_kernel,
        grid=(lay.n_tiles,),
        in_specs=[lay.x_spec(), lay.mod_spec(mod, l, sub), _gain_spec(norm_pre, l, sub)],
        out_specs=lay.x_spec(),
        out_shape=jax.ShapeDtypeStruct(x3.shape, BF16),
        compiler_params=_params(1),
    )(x3, mod, norm_pre)


def _cast_weights_once(i, pairs):
    @pl.when(i == 0)
    def _():
        for src, dst in pairs:
            dst[...] = src[...].astype(BF16)


def _ffn_in_kernel(*refs, cast_w_out):
    if cast_w_out:
        h_ref, wg_ref, wu_ref, wo_ref, a_ref, wob_ref, wgb_ref, wub_ref = refs
        _cast_weights_once(pl.program_id(1), [(wo_ref, wob_ref)])
    else:
        h_ref, wg_ref, wu_ref, a_ref, wgb_ref, wub_ref = refs
    _cast_weights_once(pl.program_id(1), [(wg_ref, wgb_ref), (wu_ref, wub_ref)])
    lead, rows, d = h_ref.shape
    for ls, rs in _pieces(lead, rows):
        shp, n = _piece_rows(h_ref, ls, rs)
        h = h_ref[ls, rs, :].reshape(n, d)
        gt = lax.dot_general(h, wgb_ref[...], NN, preferred_element_type=F32)
        up = lax.dot_general(h, wub_ref[...], NN, preferred_element_type=F32)
        a_ref[ls, rs, :] = (gt * _sigmoid(gt) * up).astype(BF16).reshape(shp[0], shp[1], -1)


def _ffn_in_call(h3, lay, w_in, l, s, tf, w_out=None):
    lead, rows, d = h3.shape
    f = w_in.shape[3] // 2
    nj = f // tf
    in_specs = [
        pl.BlockSpec(lay.block, lambda j, i: lay.xmap(i)),
        pl.BlockSpec((None, None, d, tf), lambda j, i: (l, s, 0, j)),
        pl.BlockSpec((None, None, d, tf), lambda j, i: (l, s, 0, nj + j)),
    ]
    args = [h3, w_in, w_in]
    out_specs = [pl.BlockSpec((lay.block[0], lay.block[1], tf),
                              lambda j, i: lay.xmap(i)[:2] + (j,))]
    out_shape = [jax.ShapeDtypeStruct((lead, rows, f), BF16)]
    if w_out is not None:
        in_specs.append(pl.BlockSpec((None, None, tf, d), lambda j, i: (l, s, j, 0)))
        args.append(w_out)
        out_specs.append(pl.BlockSpec((tf, d), lambda j, i: (j, 0)))
        out_shape.append(jax.ShapeDtypeStruct((f, d), BF16))
    out = pl.pallas_call(
        functools.partial(_ffn_in_kernel, cast_w_out=w_out is not None),
        grid=(nj, lay.n_tiles),
        in_specs=in_specs,
        out_specs=out_specs,
        out_shape=out_shape,
        scratch_shapes=[pltpu.VMEM((d, tf), BF16), pltpu.VMEM((d, tf), BF16)],
        compiler_params=_params(2),
    )(*args)
    return (out[0], out[1]) if w_out is not None else (out[0], None)


def _conv_kernel(*refs, time_major, tiles_per_seq):
    if time_major:
        (h_ref, wb_ref, wc_ref, wx_ref, cw_ref, st_ref,
         o_ref, so_ref, wbb_ref, wcb_ref, wxb_ref) = refs
    else:
        (h_ref, wb_ref, wc_ref, wx_ref, cw_ref,
         o_ref, so_ref, wbb_ref, wcb_ref, wxb_ref, carry_ref) = refs
    i = pl.program_id(1)
    _cast_weights_once(i, [(wb_ref, wbb_ref), (wc_ref, wcb_ref), (wx_ref, wxb_ref)])
    lead, rows, d = h_ref.shape
    tm = lead * rows

    h = h_ref[...].reshape(tm, d)

    def proj(w_ref):
        return lax.dot_general(h, w_ref[...], NN, preferred_element_type=F32)

    bg = proj(wbb_ref)
    u2 = proj(wcb_ref) * proj(wxb_ref)
    tn = u2.shape[1]
    u = u2.reshape(lead, rows, tn)
    if time_major:
        st = st_ref[...]
        fill1, fill2 = st[1:2], st
        so_ref[...] = u[lead - 2:lead]
    else:
        @pl.when(i % tiles_per_seq == 0)
        def _():
            carry_ref[...] = jnp.zeros((8, tn), F32)
        prev = carry_ref[...]
        fill1, fill2 = prev[7:8, :], prev[6:8, :]
        carry_ref[...] = u2[tm - 8:tm, :]
        so_ref[...] = u[:, rows - 2:rows, :]
    cw = cw_ref[...]
    z = (cw[0:1, :] * _shift_time(u, 2, fill2, time_major)
         + cw[1:2, :] * _shift_time(u, 1, fill1, time_major) + cw[2:3, :] * u)
    o_ref[...] = (bg.reshape(lead, rows, tn) * z).astype(BF16)


def _conv_call(h3, lay, w_in, cw, state, jl, tn):
    lead_n, rows_n, d = h3.shape
    dc = cw.shape[2]
    nj = dc // tn
    in_specs = [
        pl.BlockSpec(lay.block, lambda j, i: lay.xmap(i)),
        pl.BlockSpec((None, d, tn), lambda j, i: (jl, 0, j)),
        pl.BlockSpec((None, d, tn), lambda j, i: (jl, 0, nj + j)),
        pl.BlockSpec((None, d, tn), lambda j, i: (jl, 0, 2 * nj + j)),
        pl.BlockSpec((None, cw.shape[1], tn), lambda j, i: (jl, 0, j)),
    ]
    args = [h3, w_in, w_in, w_in, cw]
    scratch = [pltpu.VMEM((d, tn), BF16)] * 3
    if lay.time_major:
        nb = lay.block[1]
        in_specs.append(pl.BlockSpec((2, nb, tn), lambda j, i: (0, i, j)))
        args.append(state)
        so_spec = pl.BlockSpec((2, nb, tn), lambda j, i: (0, i, j))
        so_shape = jax.ShapeDtypeStruct((2, h3.shape[1], dc), F32)
    else:
        scratch.append(pltpu.VMEM((8, tn), F32))
        so_spec = pl.BlockSpec((1, 2, tn), lambda j, i: (i, 0, j))
        so_shape = jax.ShapeDtypeStruct((lay.n_tiles, 2, dc), F32)
    return pl.pallas_call(
        functools.partial(_conv_kernel, time_major=lay.time_major,
                          tiles_per_seq=lay.tiles_per_seq),
        grid=(nj, lay.n_tiles),
        in_specs=in_specs,
        out_specs=[pl.BlockSpec((lay.block[0], lay.block[1], tn),
                                lambda j, i: lay.xmap(i)[:2] + (j,)), so_spec],
        out_shape=[jax.ShapeDtypeStruct((lead_n, rows_n, dc), BF16), so_shape],
        scratch_shapes=scratch,
        compiler_params=_params(2),
    )(*args)


def _rw_prep_kernel(*refs, time_major, tiles_per_seq):
    if time_major:
        (x_ref, m_ref, gpre_ref, mix_ref, w1_ref, a1_ref, g1_ref, st_ref,
         xr_ref, xk_ref, xv_ref, tw_ref, ta_ref, tg_ref, so_ref) = refs
    else:
        (x_ref, m_ref, gpre_ref, mix_ref, w1_ref, a1_ref, g1_ref,
         xr_ref, xk_ref, xv_ref, tw_ref, ta_ref, tg_ref, so_ref, carry_ref) = refs
    i = pl.program_id(0)
    lead, rows, d = x_ref.shape
    n = lead * rows
    h = _modulated_pre(x_ref[...], m_ref, gpre_ref[...])
    if time_major:
        fill = st_ref[...][None]
        so_ref[...] = h[lead - 1]
    else:
        @pl.when(i % tiles_per_seq == 0)
        def _():
            carry_ref[...] = jnp.zeros((8, d), F32)
        fill = carry_ref[7:8, :]
        carry_ref[...] = h[0, rows - 8:rows, :]
        so_ref[...] = h[:, rows - 1:rows, :]
    xx = _shift_time(h, 1, fill, time_major) - h
    mix = mix_ref[...]

    def mixed(k):
        return h + xx * mix[k:k + 1, :]

    def low_rank(k, w_ref):
        return _mm(mixed(k).reshape(n, d), w_ref[...]).reshape(lead, rows, w_ref.shape[1])

    xr_ref[...] = mixed(0).astype(BF16)
    tw_ref[...] = jnp.tanh(low_rank(1, w1_ref))
    xk_ref[...] = mixed(2).astype(BF16)
    xv_ref[...] = mixed(3).astype(BF16)
    ta_ref[...] = low_rank(4, a1_ref)
    tg_ref[...] = _sigmoid(low_rank(5, g1_ref))


def _rw_prep_call(x3, lay, mod, norm_pre, mix, w1, a1, g1, state, l, jl, sub):
    lead, rows, d = x3.shape
    dl, dg = w1.shape[2], g1.shape[2]
    in_specs = [
        lay.x_spec(),
        lay.mod_spec(mod, l, sub),
        _gain_spec(norm_pre, l, sub),
        pl.BlockSpec((None, mix.shape[1], d), lambda i: (jl, 0, 0)),
        pl.BlockSpec((None, d, dl), lambda i: (jl, 0, 0)),
        pl.BlockSpec((None, d, dl), lambda i: (jl, 0, 0)),
        pl.BlockSpec((None, d, dg), lambda i: (jl, 0, 0)),
    ]
    args = [x3, mod, norm_pre, mix, w1, a1, g1]
    scratch = []
    blk = lay.block
    if lay.time_major:
        in_specs.append(pl.BlockSpec((blk[1], d), lambda i: (i, 0)))
        args.append(state)
        so_spec = pl.BlockSpec((blk[1], d), lambda i: (i, 0))
        so_shape = jax.ShapeDtypeStruct((rows, d), F32)
    else:
        scratch.append(pltpu.VMEM((8, d), F32))
        tps = lay.tiles_per_seq
        so_spec = pl.BlockSpec((1, 1, d), lambda i: (i // tps, 0, 0))
        so_shape = jax.ShapeDtypeStruct((lead, 1, d), F32)

    def ospec(width):
        return pl.BlockSpec((blk[0], blk[1], width), lay.xmap)

    return pl.pallas_call(
        functools.partial(_rw_prep_kernel, time_major=lay.time_major,
                          tiles_per_seq=lay.tiles_per_seq),
        grid=(lay.n_tiles,),
        in_specs=in_specs,
        out_specs=[ospec(d)] * 3 + [ospec(dl), ospec(dl), ospec(dg), so_spec],
        out_shape=[jax.ShapeDtypeStruct((lead, rows, d), BF16)] * 3
        + [jax.ShapeDtypeStruct((lead, rows, dl), F32)] * 2
        + [jax.ShapeDtypeStruct((lead, rows, dg), F32), so_shape],
        scratch_shapes=scratch,
        compiler_params=_params(1),
    )(*args)


def _rw_proj_kernel(xr_ref, xk_ref, xv_ref, tw_ref, ta_ref, tg_ref, wr_ref, wk_ref, wv_ref,
                    w2_ref, a2_ref, g2_ref, w0_ref, a0_ref,
                    r_ref, k_ref, v_ref, w_ref, a_ref, g_ref):
    r_ref[...] = _mm(xr_ref[...], wr_ref[...])
    k_ref[...] = _mm(xk_ref[...], wk_ref[...])
    v_ref[...] = _mm(xv_ref[...], wv_ref[...])
    w_ref[...] = w0_ref[...] + _mm(tw_ref[...], w2_ref[...])
    a_ref[...] = _sigmoid(a0_ref[...] + _mm(ta_ref[...], a2_ref[...]))
    g_ref[...] = _mm(tg_ref[...], g2_ref[...])


def _column_blocked(w, tn):
    n, k, nn = w.shape
    return w.astype(BF16).reshape(n, k, nn // tn, tn).transpose(0, 2, 1, 3)


def _rw_proj_call(xr, xk, xv, tw, ta, tg, wr, wk, wv, w2, a2, g2, w0, a0, jl, tm):
    m, d = xr.shape
    dl, dg = tw.shape[1], tg.shape[1]
    nr = w0.shape[0]
    tn = wr.shape[3]
    xspec = pl.BlockSpec((tm, d), lambda i, j: (i, 0))
    wspec = pl.BlockSpec((None, None, d, tn), lambda i, j: (jl, j, 0, 0))
    vspec = pl.BlockSpec((None, 1, tn), lambda i, j: (jl, 0, j))
    ospec = pl.BlockSpec((tm, tn), lambda i, j: (i, j))
    return pl.pallas_call(
        _rw_proj_kernel,
        grid=(m // tm, d // tn),
        in_specs=[xspec, xspec, xspec,
                  pl.BlockSpec((tm, dl), lambda i, j: (i, 0)),
                  pl.BlockSpec((tm, dl), lambda i, j: (i, 0)),
                  pl.BlockSpec((tm, dg), lambda i, j: (i, 0)),
                  wspec, wspec, wspec,
                  pl.BlockSpec((None, dl, tn), lambda i, j: (jl, 0, j)),
                  pl.BlockSpec((None, dl, tn), lambda i, j: (jl, 0, j)),
                  pl.BlockSpec((None, dg, tn), lambda i, j: (jl, 0, j)),
                  vspec, vspec],
        out_specs=[ospec] * 6,
        out_shape=[jax.ShapeDtypeStruct((m, d), F32)] * 6,
        compiler_params=_params(2),
    )(xr, xk, xv, tw, ta, tg, wr, wk, wv, w2, a2, g2,
      w0.reshape(nr, 1, d), a0.reshape(nr, 1, d))


def _iota2(shape, axis):
    return lax.broadcasted_iota(jnp.int32, shape, axis)


def _same_block(shape, row_block, lane_block):
    r = lax.shift_right_logical(_iota2(shape, 0), int(math.log2(row_block)))
    c = lax.shift_right_logical(_iota2(shape, 1), int(math.log2(lane_block)))
    return r == c


def _bd_rows(x, mask):
    return jnp.where(mask, jnp.concatenate([x] * HEADS_PER_GROUP, axis=0), 0.0)


def _split_mm(x, ones):
    hi = x.astype(BF16)
    lo = (x - hi.astype(F32)).astype(BF16)
    return (lax.dot_general(hi, ones, NN, preferred_element_type=F32)
            + lax.dot_general(lo, ones, NN, preferred_element_type=F32))


def _each(fn, *lists):
    return [fn(*xs) for xs in zip(*lists)]


def _wkv_chunk(r, k, v, a_, b_, lw, cum, s_bd, c):
    hs = HEAD_SIZE
    tlanes = HEADS_PER_GROUP * c
    m_ch = _same_block((tlanes, GROUP_LANES), c, hs)
    m_tt = _same_block((tlanes, tlanes), c, c)
    m_ss = _same_block((GROUP_LANES, GROUP_LANES), hs, hs)
    t_row = _iota2((c, tlanes), 0)
    s_lane = jnp.bitwise_and(_iota2((c, tlanes), 1), c - 1)
    strict = s_lane < t_row
    incl = s_lane <= t_row

    def bd_ch(x):
        return _bd_rows(x, m_ch)

    def bd_tt(x):
        return _bd_rows(x, m_tt)

    def apply(a, x):
        return _mm(a, bd_ch(x))

    cl = _each(lambda x: x[c - 1:c, :], cum)
    at = _each(lambda a, x, l: a * jnp.exp(x - l), a_, cum, lw)
    rt = _each(lambda a, x: a * jnp.exp(x), r, cum)
    bt = _each(lambda a, x: a * jnp.exp(-x), b_, cum)
    kt = _each(lambda a, x: a * jnp.exp(-x), k, cum)
    bh = _each(lambda a, x, xl: a * jnp.exp(xl - x), b_, cum, cl)
    kh = _each(lambda a, x, xl: a * jnp.exp(xl - x), k, cum, cl)

    lhs = _each(lambda a, b: jnp.concatenate([a, b], axis=0), at, rt)
    ob = _each(lambda a, x: _mm(a, bd_ch(x), NT), lhs, bt)
    ok = _each(lambda a, x: _mm(a, bd_ch(x), NT), lhs, kt)
    a_ab = _each(lambda x: jnp.where(strict, x[:c], 0.0), ob)
    a_rb = _each(lambda x: jnp.where(incl, x[c:], 0.0), ob)
    a_ak = _each(lambda x: jnp.where(strict, x[:c], 0.0), ok)
    a_rk = _each(lambda x: jnp.where(incl, x[c:], 0.0), ok)
    akv = _each(lambda a, b, x: apply(jnp.concatenate([a, b], axis=0), x), a_ak, a_rk, v)
    av = _each(lambda x: x[:c], akv)
    y_kv = _each(lambda x: x[c:], akv)

    eye = jnp.where(s_lane == t_row, 1.0, 0.0)
    tinv = _each(lambda x: eye + x, a_ab)
    npow = a_ab
    for _ in range(int(math.log2(c)) - 1):
        npow = _each(lambda x: _mm(x, bd_tt(x)), npow)
        tinv = _each(lambda t, x: t + _mm(t, bd_tt(x)), tinv, npow)

    wt = _each(apply, tinv, at)
    w = _each(apply, tinv, av)
    qt = _each(lambda x, a, y: x + apply(a, y), rt, a_rb, wt)
    yi = _each(lambda a, x, y: apply(a, x) + y, a_rb, w, y_kv)
    g_off = _each(lambda x, y: jnp.where(m_ss, _mm(x, y, TN), 0.0), wt, bh)
    h_t = _each(lambda x, y, p, q: jnp.where(
        m_ss, _mm(jnp.concatenate([x, p], axis=0), jnp.concatenate([y, q], axis=0), TN), 0.0),
                w, bh, v, kh)

    y = _each(lambda q, s, x: _mm(q, s, NT) + x, qt, s_bd, yi)
    s_new = _each(lambda s, xl, g, h: s * jnp.exp(xl) + _mm(s, g) + h, s_bd, cl, g_off, h_t)
    return y, s_new


def _scan_kernel(*refs, c, d, nq, has_state, lane_packed):
    if has_state:
        (r_ref, k_ref, v_ref, w_ref, a_ref, g_ref, p_ref, s0_ref, y_ref, so_ref, sbd_ref) = refs
    else:
        (r_ref, k_ref, v_ref, w_ref, a_ref, g_ref, p_ref, y_ref, so_ref, sbd_ref) = refs
    ci = pl.program_id(1)
    hs = HEAD_SIZE
    ng = d // GROUP_LANES
    chains = [(q, g) for q in range(nq) for g in range(ng)]
    n_ch = len(chains)
    m_ss = _same_block((GROUP_LANES, GROUP_LANES), hs, hs)
    ones_bd = jnp.where(m_ss, 1.0, 0.0).astype(BF16)

    def lanes(g):
        return slice(g * GROUP_LANES, (g + 1) * GROUP_LANES)

    def rd(ref, q, g):
        if lane_packed:
            return ref[:, q * d + g * GROUP_LANES:q * d + (g + 1) * GROUP_LANES]
        return ref[q, :, lanes(g)]

    def par(row, g):
        return p_ref[row:row + 1, lanes(g)]

    @pl.when(ci == 0)
    def _():
        for i, (q, g) in enumerate(chains):
            if has_state:
                heads = [s0_ref[q, HEADS_PER_GROUP * g + h] for h in range(HEADS_PER_GROUP)]
                sbd_ref[i] = _bd_rows(jnp.concatenate(heads, axis=1), m_ss)
            else:
                sbd_ref[i] = jnp.zeros((GROUP_LANES, GROUP_LANES), F32)

    def log_decay(wq):
        z = -wq
        softplus = jnp.maximum(z, 0.0) + jnp.log(1.0 + jnp.exp(-jnp.abs(z)))
        lw = -jnp.exp(-softplus - 0.5)
        row = _iota2(lw.shape, 0)
        cum = lw
        step = 1
        while step < c:
            cum = cum + jnp.where(row >= step, pltpu.roll(cum, step, 0), 0.0)
            step *= 2
        return lw, cum

    if lane_packed:
        lw_all, cum_all = log_decay(w_ref[...])
        lw = [lw_all[:, q * d + g * GROUP_LANES:q * d + (g + 1) * GROUP_LANES] for q, g in chains]
        cum = [cum_all[:, q * d + g * GROUP_LANES:q * d + (g + 1) * GROUP_LANES] for q, g in chains]
    else:
        per_q = [log_decay(w_ref[q]) for q in range(nq)]
        lw = [per_q[q][0][:, lanes(g)] for q, g in chains]
        cum = [per_q[q][1][:, lanes(g)] for q, g in chains]

    def seg_sum(xs):
        y = _split_mm(jnp.concatenate(xs, axis=0), ones_bd)
        return [y[i * c:(i + 1) * c] for i in range(len(xs))]

    r = [rd(r_ref, q, g) for q, g in chains]
    k = [rd(k_ref, q, g) for q, g in chains]
    v = [rd(v_ref, q, g) for q, g in chains]
    a_sig = [rd(a_ref, q, g) for q, g in chains]
    kk = [x * par(0, g) for x, (q, g) in zip(k, chains)]
    k = [x * (1.0 + (a - 1.0) * par(1, g)) for x, a, (q, g) in zip(k, a_sig, chains)]
    sums = seg_sum(_each(lambda x: x * x, kk)
                   + [x * y * par(2, g) for x, y, (q, g) in zip(r, k, chains)])
    kk = _each(lambda x, n: x / jnp.maximum(jnp.sqrt(n), NORM_EPS), kk, sums[:n_ch])
    bonus = _each(lambda s, z: s * z, sums[n_ch:], v)
    y, s_new = _wkv_chunk(r, k, v, _each(lambda x: -x, kk), _each(lambda x, a: x * a, kk, a_sig),
                          lw, cum, [sbd_ref[i] for i in range(n_ch)], c)
    for i in range(n_ch):
        sbd_ref[i] = s_new[i]

    mu = _each(lambda x: x * (1.0 / hs), seg_sum(y))
    dy = _each(lambda x, m: x - m, y, mu)
    var = _each(lambda x: x * (1.0 / hs), seg_sum(_each(lambda x: x * x, dy)))
    for i, (q, g) in enumerate(chains):
        yn = dy[i] * lax.rsqrt(var[i] + GN_EPS) * par(3, g) + par(4, g)
        out = ((yn + bonus[i]) * rd(g_ref, q, g)).astype(BF16)
        if lane_packed:
            y_ref[:, q * d + g * GROUP_LANES:q * d + (g + 1) * GROUP_LANES] = out
        else:
            y_ref[q, :, lanes(g)] = out

    @pl.when(ci == pl.num_programs(1) - 1)
    def _():
        for i, (q, g) in enumerate(chains):
            s = sbd_ref[i]
            s = s[0:hs] + s[hs:2 * hs] + s[2 * hs:3 * hs] + s[3 * hs:4 * hs]
            for h in range(HEADS_PER_GROUP):
                so_ref[q, HEADS_PER_GROUP * g + h] = s[:, h * hs:(h + 1) * hs]


def _scan_call(r, k, v, w, a, g, p, s0, nb, t, c, nq, time_major):
    m, d = r.shape
    nc = t // c
    ng = d // GROUP_LANES
    nh = d // HEAD_SIZE
    has_state = s0 is not None
    if time_major:
        r, k, v, w, a, g = (z.reshape(t, nb * d) for z in (r, k, v, w, a, g))
        xspec = pl.BlockSpec((c, nq * d), lambda b, ci: (ci, b))
    else:
        r, k, v, w, a, g = (z.reshape(nb, t, d) for z in (r, k, v, w, a, g))
        xspec = pl.BlockSpec((nq, c, d), lambda b, ci: (b, ci, 0))
    sspec = pl.BlockSpec((nq, nh, HEAD_SIZE, HEAD_SIZE), lambda b, ci: (b, 0, 0, 0))
    in_specs = [xspec] * 6 + [pl.BlockSpec(p.shape, lambda b, ci: (0, 0))]
    args = [r, k, v, w, a, g, p]
    if has_state:
        in_specs.append(sspec)
        args.append(s0)
    y, s_fin = pl.pallas_call(
        functools.partial(_scan_kernel, c=c, d=d, nq=nq, has_state=has_state,
                          lane_packed=time_major),
        grid=(nb // nq, nc),
        in_specs=in_specs,
        out_specs=[xspec, sspec],
        out_shape=[jax.ShapeDtypeStruct(r.shape, BF16),
                   jax.ShapeDtypeStruct((nb, nh, HEAD_SIZE, HEAD_SIZE), F32)],
        scratch_shapes=[pltpu.VMEM((nq * ng, GROUP_LANES, GROUP_LANES), F32)],
        compiler_params=_params(2),
    )(*args)
    return y.reshape(m, d), s_fin


def _trunk(x3, mod, conv_state, shift_state, wkv_state, wts, *, time_major, tile, out_tile,
           prep_tile, chunk, scan_seqs):
    d = x3.shape[2]
    nb, t = (x3.shape[1], x3.shape[0]) if time_major else (x3.shape[0], x3.shape[1])
    depth = mod.shape[0]
    lay = _Layout(x3.shape, time_major, tile)
    out_lay = _Layout(x3.shape, time_major, out_tile)
    prep_lay = _Layout(x3.shape, time_major, prep_tile)
    m = nb * t
    new_conv, new_shift, new_wkv = [], [], []

    subs = [(l, sub) for l in range(depth) for sub in range(3)]

    def takes_h(l, sub):
        return not (sub == 1 and l % 2 == 1)

    def nxt_of(l, sub):
        i = subs.index((l, sub)) + 1
        if i < len(subs) and takes_h(*subs[i]):
            return (wts["norm_pre"],) + subs[i]
        return None

    def ffn(x3, h3, l, s, sub):
        w_out = wts["ffn_w_out_bf16"].get((l, s))
        act, cast = _ffn_in_call(h3, lay, wts["ffn_w_in"], l, s, 512,
                                 wts["ffn_w_out"] if w_out is None else None)
        if w_out is None:
            w_out = wts["ffn_w_out_bf16"][(l, s)] = cast
        return _out_proj_call(x3, out_lay, mod, wts["norm_post"], act, w_out[None, None],
                              l, (0, 0), sub, HALF_STEP, nxt_of(l, sub))

    h3 = _pre_call(x3, out_lay, mod, wts["norm_pre"], 0, 0)
    for l in range(depth):
        jl = l // 2
        x3, h3 = ffn(x3, h3, l, 0, 0)
        if l % 2 == 0:
            st = conv_state[jl].transpose(1, 0, 2) if time_major else None
            bz, cs = _conv_call(h3, lay, wts["conv_w_in"], wts["conv_w"], st, jl, 256)
            x3, h3 = _out_proj_call(x3, out_lay, mod, wts["norm_post"], bz, wts["conv_w_out"],
                                    l, (jl, 0), 1, 1.0, nxt_of(l, 1))
            if time_major:
                new_conv.append(cs.transpose(1, 0, 2))
            else:
                tps = lay.tiles_per_seq
                new_conv.append(cs[tps - 1::tps])
        else:
            st = shift_state[jl] if time_major else None
            xr, xk, xv, tw, ta, tg, ss = _rw_prep_call(
                x3, prep_lay, mod, wts["norm_pre"], wts["rw_mix"], wts["rw_w1"], wts["rw_a1"],
                wts["rw_g1"], st, l, jl, 1)
            flat = lambda z: z.reshape(m, z.shape[2])
            r, k, v, w, a, g = _rw_proj_call(
                flat(xr), flat(xk), flat(xv), flat(tw), flat(ta), flat(tg),
                wts["rw_wr"], wts["rw_wk"], wts["rw_wv"], wts["rw_w2"], wts["rw_a2"],
                wts["rw_g2"], wts["rw_w0"], wts["rw_a0"], jl, lay.tile_rows)
            p = jnp.concatenate([
                wts["rw_kk"][jl][None], wts["rw_ka"][jl][None], wts["rw_rk"][jl].reshape(1, d),
                wts["rw_lnw"][jl][None], wts["rw_lnb"][jl][None],
                jnp.zeros((3, d), F32)], axis=0)
            s0 = wkv_state[jl] if time_major else None
            yg, s_fin = _scan_call(r, k, v, w, a, g, p, s0, nb, t, chunk, scan_seqs, time_major)
            x3, h3 = _out_proj_call(x3, out_lay, mod, wts["norm_post"], yg.reshape(x3.shape),
                                    wts["rw_wo"], l, (jl, 0), 1, 1.0, nxt_of(l, 1))
            new_shift.append(ss.reshape(nb, d))
            new_wkv.append(s_fin)
        x3, h3 = ffn(x3, h3, l, 1, 2)
    def stack(xs):
        return xs[0][None] if len(xs) == 1 else jnp.stack(xs)

    return x3, stack(new_conv), stack(new_shift), stack(new_wkv)


def kernel(x_prompt, x_sample, state_conv, state_shift, state_wkv, c_prompt, c_sample, mod_w, mod_b, norm_pre, norm_post, ffn_w_in, ffn_w_out, conv_w_in, conv_w, conv_w_out, rw_mix, rw_w0, rw_w1, rw_w2, rw_a0, rw_a1, rw_a2, rw_g1, rw_g2, rw_kk, rw_ka, rw_rk, rw_wr, rw_wk, rw_wv, rw_wo, rw_lnw, rw_lnb):
    b, t, d = x_prompt.shape
    sb, st, _ = x_sample.shape
    depth = mod_w.shape[0]
    n_sub = norm_pre.shape[1]
    wts = dict(norm_pre=norm_pre.reshape(depth, n_sub, 1, d),
               norm_post=norm_post.reshape(depth, n_sub, 1, d), ffn_w_in=ffn_w_in,
               ffn_w_out=ffn_w_out, ffn_w_out_bf16={},
               conv_w_out=conv_w_out.astype(BF16)[:, None],
               rw_wo=rw_wo.astype(BF16)[:, None],
               conv_w_in=conv_w_in, conv_w=conv_w, rw_mix=rw_mix,
               rw_w0=rw_w0, rw_w1=rw_w1, rw_w2=rw_w2, rw_a0=rw_a0, rw_a1=rw_a1, rw_a2=rw_a2,
               rw_g1=rw_g1, rw_g2=rw_g2, rw_kk=rw_kk, rw_ka=rw_ka, rw_rk=rw_rk,
               rw_wr=_column_blocked(rw_wr, 256), rw_wk=_column_blocked(rw_wk, 256),
               rw_wv=_column_blocked(rw_wv, 256), rw_lnw=rw_lnw, rw_lnb=rw_lnb)

    n_c = b + sb
    pad = (-n_c) % 8
    c_all = jnp.concatenate([c_prompt, c_sample, jnp.zeros((pad, d), F32)], axis=0)
    mod_all = _mod_call(c_all, mod_w, mod_b)
    mod_p = (mod_all[:, :b].reshape(depth, b, N_MOD, d).transpose(0, 2, 1, 3)
             .reshape(depth, N_MOD, b, 1, d))
    mod_s = mod_all[:, b:n_c].reshape(depth, sb, N_MOD, d).transpose(0, 2, 1, 3)

    y_p, conv_p, shift_p, wkv_p = _trunk(
        x_prompt, mod_p, None, None, None, wts, time_major=False,
        tile=min(t, 1024), out_tile=min(t, 512), prep_tile=min(t, 256), chunk=min(t, 64),
        scan_seqs=math.gcd(b, 2))
    y_s, conv_s, shift_s, wkv_s = _trunk(
        x_sample.transpose(1, 0, 2), mod_s, state_conv, state_shift, state_wkv, wts,
        time_major=True, tile=sb, out_tile=min(sb, 32), prep_tile=min(sb, 32), chunk=st,
        scan_seqs=math.gcd(sb, 2))
    return (y_p, y_s.transpose(1, 0, 2), conv_p, shift_p, wkv_p, conv_s, shift_s, wkv_s)
```

```python
import functools
import math

import jax
import jax.numpy as jnp
from jax import lax
from jax.experimental import pallas as pl
from jax.experimental.pallas import tpu as pltpu

F32 = jnp.float32
BF16 = jnp.bfloat16

RMS_EPS = 1e-6
GN_EPS = 64e-5
NORM_EPS = 1e-12
HALF_STEP = 0.5
HEAD_SIZE = 64
N_MOD = 9

MXU_WIDTH_V7X = 256
GROUP_LANES = MXU_WIDTH_V7X
HEADS_PER_GROUP = GROUP_LANES // HEAD_SIZE
VMEM_LIMIT_V7X = 60 * 2**20
ROW_PIECE = 256

NN = (((1,), (0,)), ((), ()))
NT = (((1,), (1,)), ((), ()))
TN = (((0,), (0,)), ((), ()))


def _mm(a, b, dims=NN):
    return lax.dot_general(a.astype(BF16), b.astype(BF16), dims, preferred_element_type=F32)


def _sigmoid(x):
    return 1.0 / (1.0 + jnp.exp(-x))


def _params(n_axes):
    return pltpu.CompilerParams(dimension_semantics=("arbitrary",) * n_axes,
                                vmem_limit_bytes=VMEM_LIMIT_V7X)


def _rms(x, g):
    return x * lax.rsqrt(jnp.mean(x * x, axis=-1, keepdims=True) + RMS_EPS) * g


def _modulated_pre(x, m_ref, g):
    return _rms(x, g) * (1.0 + m_ref[1]) + m_ref[0]


def _gated_post(x, y, m_ref, g, res_w):
    return x + (res_w * m_ref[2]) * _rms(y, g)


def _pieces(lead, rows):
    if lead == 1:
        n = min(ROW_PIECE, rows)
        return [(slice(0, 1), slice(r0, r0 + n)) for r0 in range(0, rows, n)]
    n = min(lead, max(1, ROW_PIECE // rows))
    return [(slice(a0, a0 + n), slice(0, rows)) for a0 in range(0, lead, n)]


def _piece_rows(x_ref, ls, rs):
    shp = x_ref[ls, rs, :].shape
    return shp, shp[0] * shp[1]


def _shift_time(u, k, fill, time_major):
    lead, rows, n = u.shape
    if time_major:
        return jnp.concatenate([fill, u[:lead - k]], axis=0)
    out = pltpu.roll(u.reshape(rows, n), k, 0)
    row = lax.broadcasted_iota(jnp.int32, (rows, n), 0)
    for t in range(k):
        out = jnp.where(row == t, fill[t:t + 1, :], out)
    return out.reshape(1, rows, n)


class _Layout:
    def __init__(self, shape, time_major, rows_per_tile):
        self.time_major = time_major
        self.shape = shape
        lead, rows, d = shape
        if time_major:
            self.block = (lead, rows_per_tile, d)
            self.tiles_per_seq = 1
            self.n_tiles = rows // rows_per_tile
        else:
            self.block = (1, rows_per_tile, d)
            self.tiles_per_seq = rows // rows_per_tile
            self.n_tiles = lead * self.tiles_per_seq
        self.tile_rows = self.block[0] * self.block[1]

    def xmap(self, i, *_):
        if self.time_major:
            return (0, i, 0)
        return (i // self.tiles_per_seq, i % self.tiles_per_seq, 0)

    def x_spec(self, **kw):
        return pl.BlockSpec(self.block, self.xmap, **kw)

    def mod_spec(self, mod, l, sub):
        d = self.shape[2]
        if self.time_major:
            return pl.BlockSpec((None, 3, self.block[1], d), lambda i, *_: (l, sub, i, 0))
        tps = self.tiles_per_seq
        return pl.BlockSpec((None, 3, None, 1, d), lambda i, *_: (l, sub, i // tps, 0, 0))


def _gain_spec(gains, l, sub):
    return pl.BlockSpec((None, None, 1, gains.shape[3]), lambda i, *_: (l, sub, 0, 0))


def _mod_kernel(c_ref, w_ref, b_ref, o_ref):
    c = c_ref[...]
    o_ref[...] = _mm(c * _sigmoid(c), w_ref[...]) + b_ref[...]


def _mod_call(c_all, mod_w, mod_b):
    depth, d, n = mod_w.shape
    nbp = c_all.shape[0]
    tn = math.gcd(n, 1024)
    return pl.pallas_call(
        _mod_kernel,
        grid=(depth, n // tn),
        in_specs=[
            pl.BlockSpec((nbp, d), lambda l, j: (0, 0)),
            pl.BlockSpec((None, d, tn), lambda l, j: (l, 0, j)),
            pl.BlockSpec((None, 1, tn), lambda l, j: (l, 0, j)),
        ],
        out_specs=pl.BlockSpec((None, nbp, tn), lambda l, j: (l, 0, j)),
        out_shape=jax.ShapeDtypeStruct((depth, nbp, n), F32),
        compiler_params=_params(2),
    )(c_all, mod_w, mod_b.reshape(depth, 1, n))


def _out_proj_kernel(*refs, res_w, emit_next):
    if emit_next:
        x_ref, m_ref, gpost_ref, a_ref, w_ref, mn_ref, gn_ref, o_ref, h_ref = refs
    else:
        x_ref, m_ref, gpost_ref, a_ref, w_ref, o_ref = refs
    lead, rows, _ = x_ref.shape
    for ls, rs in _pieces(lead, rows):
        shp, n = _piece_rows(x_ref, ls, rs)
        a = a_ref[ls, rs, :]
        y = lax.dot_general(a.reshape(n, a.shape[2]), w_ref[...], NN,
                            preferred_element_type=F32).reshape(shp)
        o = _gated_post(x_ref[ls, rs, :], y, m_ref, gpost_ref[...], res_w)
        o_ref[ls, rs, :] = o
        if emit_next:
            h_ref[ls, rs, :] = _modulated_pre(o, mn_ref, gn_ref[...]).astype(BF16)


def _out_proj_call(x3, lay, mod, norm_post, act, w, l, widx, sub, res_w, nxt=None):
    k, d = w.shape[2], w.shape[3]
    in_specs = [
        lay.x_spec(),
        lay.mod_spec(mod, l, sub),
        _gain_spec(norm_post, l, sub),
        pl.BlockSpec((lay.block[0], lay.block[1], k), lay.xmap),
        pl.BlockSpec((None, None, k, d), lambda i: (widx[0], widx[1], 0, 0),
                     pipeline_mode=pl.Buffered(1)),
    ]
    args = [x3, mod, norm_post, act, w]
    out_specs = [lay.x_spec()]
    out_shape = [jax.ShapeDtypeStruct(x3.shape, F32)]
    if nxt is not None:
        norm_pre, ln, subn = nxt
        in_specs += [lay.mod_spec(mod, ln, subn), _gain_spec(norm_pre, ln, subn)]
        args += [mod, norm_pre]
        out_specs.append(lay.x_spec())
        out_shape.append(jax.ShapeDtypeStruct(x3.shape, BF16))
    out = pl.pallas_call(
        functools.partial(_out_proj_kernel, res_w=res_w, emit_next=nxt is not None),
        grid=(lay.n_tiles,),
        in_specs=in_specs,
        out_specs=out_specs,
        out_shape=out_shape,
        compiler_params=_params(1),
    )(*args)
    return (out[0], out[1]) if nxt is not None else (out[0], None)


def _pre_kernel(x_ref, m_ref, g_ref, h_ref):
    lead, rows, _ = x_ref.shape
    for ls, rs in _pieces(lead, rows):
        h_ref[ls, rs, :] = _modulated_pre(x_ref[ls, rs, :], m_ref, g_ref[...]).astype(BF16)


def _pre_call(x3, lay, mod, norm_pre, l, sub):
    return pl.pallas_call(
        _pre_kernel,
        grid=(lay.n_tiles,),
        in_specs=[lay.x_spec(), lay.mod_spec(mod, l, sub), _gain_spec(norm_pre, l, sub)],
        out_specs=lay.x_spec(),
        out_shape=jax.ShapeDtypeStruct(x3.shape, BF16),
        compiler_params=_params(1),
    )(x3, mod, norm_pre)


def _cast_weights_once(i, pairs):
    @pl.when(i == 0)
    def _():
        for src, dst in pairs:
            dst[...] = src[...].astype(BF16)


def _ffn_in_kernel(*refs, cast_w_out):
    if cast_w_out:
        h_ref, wg_ref, wu_ref, wo_ref, a_ref, wob_ref, wgb_ref, wub_ref = refs
        _cast_weights_once(pl.program_id(1), [(wo_ref, wob_ref)])
    else:
        h_ref, wg_ref, wu_ref, a_ref, wgb_ref, wub_ref = refs
    _cast_weights_once(pl.program_id(1), [(wg_ref, wgb_ref), (wu_ref, wub_ref)])
    lead, rows, d = h_ref.shape
    for ls, rs in _pieces(lead, rows):
        shp, n = _piece_rows(h_ref, ls, rs)
        h = h_ref[ls, rs, :].reshape(n, d)
        gt = lax.dot_general(h, wgb_ref[...], NN, preferred_element_type=F32)
        up = lax.dot_general(h, wub_ref[...], NN, preferred_element_type=F32)
        a_ref[ls, rs, :] = (gt * _sigmoid(gt) * up).astype(BF16).reshape(shp[0], shp[1], -1)


def _ffn_in_call(h3, lay, w_in, l, s, tf, w_out=None):
    lead, rows, d = h3.shape
    f = w_in.shape[3] // 2
    nj = f // tf
    in_specs = [
        pl.BlockSpec(lay.block, lambda j, i: lay.xmap(i)),
        pl.BlockSpec((None, None, d, tf), lambda j, i: (l, s, 0, j)),
        pl.BlockSpec((None, None, d, tf), lambda j, i: (l, s, 0, nj + j)),
    ]
    args = [h3, w_in, w_in]
    out_specs = [pl.BlockSpec((lay.block[0], lay.block[1], tf),
                              lambda j, i: lay.xmap(i)[:2] + (j,))]
    out_shape = [jax.ShapeDtypeStruct((lead, rows, f), BF16)]
    if w_out is not None:
        in_specs.append(pl.BlockSpec((None, None, tf, d), lambda j, i: (l, s, j, 0)))
        args.append(w_out)
        out_specs.append(pl.BlockSpec((tf, d), lambda j, i: (j, 0)))
        out_shape.append(jax.ShapeDtypeStruct((f, d), BF16))
    out = pl.pallas_call(
        functools.partial(_ffn_in_kernel, cast_w_out=w_out is not None),
        grid=(nj, lay.n_tiles),
        in_specs=in_specs,
        out_specs=out_specs,
        out_shape=out_shape,
        scratch_shapes=[pltpu.VMEM((d, tf), BF16), pltpu.VMEM((d, tf), BF16)],
        compiler_params=_params(2),
    )(*args)
    return (out[0], out[1]) if w_out is not None else (out[0], None)


def _conv_kernel(*refs, time_major, tiles_per_seq):
    if time_major:
        (h_ref, wb_ref, wc_ref, wx_ref, cw_ref, st_ref,
         o_ref, so_ref, wbb_ref, wcb_ref, wxb_ref) = refs
    else:
        (h_ref, wb_ref, wc_ref, wx_ref, cw_ref,
         o_ref, so_ref, wbb_ref, wcb_ref, wxb_ref, carry_ref) = refs
    i = pl.program_id(1)
    _cast_weights_once(i, [(wb_ref, wbb_ref), (wc_ref, wcb_ref), (wx_ref, wxb_ref)])
    lead, rows, d = h_ref.shape
    tm = lead * rows

    h = h_ref[...].reshape(tm, d)

    def proj(w_ref):
        return lax.dot_general(h, w_ref[...], NN, preferred_element_type=F32)

    bg = proj(wbb_ref)
    u2 = proj(wcb_ref) * proj(wxb_ref)
    tn = u2.shape[1]
    u = u2.reshape(lead, rows, tn)
    if time_major:
        st = st_ref[...]
        fill1, fill2 = st[1:2], st
        so_ref[...] = u[lead - 2:lead]
    else:
        @pl.when(i % tiles_per_seq == 0)
        def _():
            carry_ref[...] = jnp.zeros((8, tn), F32)
        prev = carry_ref[...]
        fill1, fill2 = prev[7:8, :], prev[6:8, :]
        carry_ref[...] = u2[tm - 8:tm, :]
        so_ref[...] = u[:, rows - 2:rows, :]
    cw = cw_ref[...]
    z = (cw[0:1, :] * _shift_time(u, 2, fill2, time_major)
         + cw[1:2, :] * _shift_time(u, 1, fill1, time_major) + cw[2:3, :] * u)
    o_ref[...] = (bg.reshape(lead, rows, tn) * z).astype(BF16)


def _conv_call(h3, lay, w_in, cw, state, jl, tn):
    lead_n, rows_n, d = h3.shape
    dc = cw.shape[2]
    nj = dc // tn
    in_specs = [
        pl.BlockSpec(lay.block, lambda j, i: lay.xmap(i)),
        pl.BlockSpec((None, d, tn), lambda j, i: (jl, 0, j)),
        pl.BlockSpec((None, d, tn), lambda j, i: (jl, 0, nj + j)),
        pl.BlockSpec((None, d, tn), lambda j, i: (jl, 0, 2 * nj + j)),
        pl.BlockSpec((None, cw.shape[1], tn), lambda j, i: (jl, 0, j)),
    ]
    args = [h3, w_in, w_in, w_in, cw]
    scratch = [pltpu.VMEM((d, tn), BF16)] * 3
    if lay.time_major:
        nb = lay.block[1]
        in_specs.append(pl.BlockSpec((2, nb, tn), lambda j, i: (0, i, j)))
        args.append(state)
        so_spec = pl.BlockSpec((2, nb, tn), lambda j, i: (0, i, j))
        so_shape = jax.ShapeDtypeStruct((2, h3.shape[1], dc), F32)
    else:
        scratch.append(pltpu.VMEM((8, tn), F32))
        so_spec = pl.BlockSpec((1, 2, tn), lambda j, i: (i, 0, j))
        so_shape = jax.ShapeDtypeStruct((lay.n_tiles, 2, dc), F32)
    return pl.pallas_call(
        functools.partial(_conv_kernel, time_major=lay.time_major,
                          tiles_per_seq=lay.tiles_per_seq),
        grid=(nj, lay.n_tiles),
        in_specs=in_specs,
        out_specs=[pl.BlockSpec((lay.block[0], lay.block[1], tn),
                                lambda j, i: lay.xmap(i)[:2] + (j,)), so_spec],
        out_shape=[jax.ShapeDtypeStruct((lead_n, rows_n, dc), BF16), so_shape],
        scratch_shapes=scratch,
        compiler_params=_params(2),
    )(*args)


def _rw_prep_kernel(*refs, time_major, tiles_per_seq):
    if time_major:
        (x_ref, m_ref, gpre_ref, mix_ref, w1_ref, a1_ref, g1_ref, st_ref,
         xr_ref, xk_ref, xv_ref, tw_ref, ta_ref, tg_ref, so_ref) = refs
    else:
        (x_ref, m_ref, gpre_ref, mix_ref, w1_ref, a1_ref, g1_ref,
         xr_ref, xk_ref, xv_ref, tw_ref, ta_ref, tg_ref, so_ref, carry_ref) = refs
    i = pl.program_id(0)
    lead, rows, d = x_ref.shape
    n = lead * rows
    h = _modulated_pre(x_ref[...], m_ref, gpre_ref[...])
    if time_major:
        fill = st_ref[...][None]
        so_ref[...] = h[lead - 1]
    else:
        @pl.when(i % tiles_per_seq == 0)
        def _():
            carry_ref[...] = jnp.zeros((8, d), F32)
        fill = carry_ref[7:8, :]
        carry_ref[...] = h[0, rows - 8:rows, :]
        so_ref[...] = h[:, rows - 1:rows, :]
    xx = _shift_time(h, 1, fill, time_major) - h
    mix = mix_ref[...]

    def mixed(k):
        return h + xx * mix[k:k + 1, :]

    def low_rank(k, w_ref):
        return _mm(mixed(k).reshape(n, d), w_ref[...]).reshape(lead, rows, w_ref.shape[1])

    xr_ref[...] = mixed(0).astype(BF16)
    tw_ref[...] = jnp.tanh(low_rank(1, w1_ref))
    xk_ref[...] = mixed(2).astype(BF16)
    xv_ref[...] = mixed(3).astype(BF16)
    ta_ref[...] = low_rank(4, a1_ref)
    tg_ref[...] = _sigmoid(low_rank(5, g1_ref))


def _rw_prep_call(x3, lay, mod, norm_pre, mix, w1, a1, g1, state, l, jl, sub):
    lead, rows, d = x3.shape
    dl, dg = w1.shape[2], g1.shape[2]
    in_specs = [
        lay.x_spec(),
        lay.mod_spec(mod, l, sub),
        _gain_spec(norm_pre, l, sub),
        pl.BlockSpec((None, mix.shape[1], d), lambda i: (jl, 0, 0)),
        pl.BlockSpec((None, d, dl), lambda i: (jl, 0, 0)),
        pl.BlockSpec((None, d, dl), lambda i: (jl, 0, 0)),
        pl.BlockSpec((None, d, dg), lambda i: (jl, 0, 0)),
    ]
    args = [x3, mod, norm_pre, mix, w1, a1, g1]
    scratch = []
    blk = lay.block
    if lay.time_major:
        in_specs.append(pl.BlockSpec((blk[1], d), lambda i: (i, 0)))
        args.append(state)
        so_spec = pl.BlockSpec((blk[1], d), lambda i: (i, 0))
        so_shape = jax.ShapeDtypeStruct((rows, d), F32)
    else:
        scratch.append(pltpu.VMEM((8, d), F32))
        tps = lay.tiles_per_seq
        so_spec = pl.BlockSpec((1, 1, d), lambda i: (i // tps, 0, 0))
        so_shape = jax.ShapeDtypeStruct((lead, 1, d), F32)

    def ospec(width):
        return pl.BlockSpec((blk[0], blk[1], width), lay.xmap)

    return pl.pallas_call(
        functools.partial(_rw_prep_kernel, time_major=lay.time_major,
                          tiles_per_seq=lay.tiles_per_seq),
        grid=(lay.n_tiles,),
        in_specs=in_specs,
        out_specs=[ospec(d)] * 3 + [ospec(dl), ospec(dl), ospec(dg), so_spec],
        out_shape=[jax.ShapeDtypeStruct((lead, rows, d), BF16)] * 3
        + [jax.ShapeDtypeStruct((lead, rows, dl), F32)] * 2
        + [jax.ShapeDtypeStruct((lead, rows, dg), F32), so_shape],
        scratch_shapes=scratch,
        compiler_params=_params(1),
    )(*args)


def _rw_proj_kernel(xr_ref, xk_ref, xv_ref, tw_ref, ta_ref, tg_ref, wr_ref, wk_ref, wv_ref,
                    w2_ref, a2_ref, g2_ref, w0_ref, a0_ref,
                    r_ref, k_ref, v_ref, w_ref, a_ref, g_ref):
    r_ref[...] = _mm(xr_ref[...], wr_ref[...])
    k_ref[...] = _mm(xk_ref[...], wk_ref[...])
    v_ref[...] = _mm(xv_ref[...], wv_ref[...])
    w_ref[...] = w0_ref[...] + _mm(tw_ref[...], w2_ref[...])
    a_ref[...] = _sigmoid(a0_ref[...] + _mm(ta_ref[...], a2_ref[...]))
    g_ref[...] = _mm(tg_ref[...], g2_ref[...])


def _rw_proj_call(xr, xk, xv, tw, ta, tg, wr, wk, wv, w2, a2, g2, w0, a0, jl, tm, tn):
    m, d = xr.shape
    dl, dg = tw.shape[1], tg.shape[1]
    nr = w0.shape[0]
    xspec = pl.BlockSpec((tm, d), lambda i, j: (i, 0))
    wspec = pl.BlockSpec((None, d, tn), lambda i, j: (jl, 0, j))
    vspec = pl.BlockSpec((None, 1, tn), lambda i, j: (jl, 0, j))
    ospec = pl.BlockSpec((tm, tn), lambda i, j: (i, j))
    return pl.pallas_call(
        _rw_proj_kernel,
        grid=(m // tm, d // tn),
        in_specs=[xspec, xspec, xspec,
                  pl.BlockSpec((tm, dl), lambda i, j: (i, 0)),
                  pl.BlockSpec((tm, dl), lambda i, j: (i, 0)),
                  pl.BlockSpec((tm, dg), lambda i, j: (i, 0)),
                  wspec, wspec, wspec,
                  pl.BlockSpec((None, dl, tn), lambda i, j: (jl, 0, j)),
                  pl.BlockSpec((None, dl, tn), lambda i, j: (jl, 0, j)),
                  pl.BlockSpec((None, dg, tn), lambda i, j: (jl, 0, j)),
                  vspec, vspec],
        out_specs=[ospec] * 6,
        out_shape=[jax.ShapeDtypeStruct((m, d), F32)] * 6,
        compiler_params=_params(2),
    )(xr, xk, xv, tw, ta, tg, wr, wk, wv, w2, a2, g2,
      w0.reshape(nr, 1, d), a0.reshape(nr, 1, d))


def _iota2(shape, axis):
    return lax.broadcasted_iota(jnp.int32, shape, axis)


def _same_block(shape, row_block, lane_block):
    r = lax.shift_right_logical(_iota2(shape, 0), int(math.log2(row_block)))
    c = lax.shift_right_logical(_iota2(shape, 1), int(math.log2(lane_block)))
    return r == c


def _bd_rows(x, mask):
    return jnp.where(mask, jnp.concatenate([x] * HEADS_PER_GROUP, axis=0), 0.0)


def _split_mm(x, ones):
    hi = x.astype(BF16)
    lo = (x - hi.astype(F32)).astype(BF16)
    return (lax.dot_general(hi, ones, NN, preferred_element_type=F32)
            + lax.dot_general(lo, ones, NN, preferred_element_type=F32))


def _each(fn, *lists):
    return [fn(*xs) for xs in zip(*lists)]


def _wkv_chunk(r, k, v, a_, b_, lw, cum, s_bd, c):
    hs = HEAD_SIZE
    tlanes = HEADS_PER_GROUP * c
    m_ch = _same_block((tlanes, GROUP_LANES), c, hs)
    m_tt = _same_block((tlanes, tlanes), c, c)
    m_ss = _same_block((GROUP_LANES, GROUP_LANES), hs, hs)
    t_row = _iota2((c, tlanes), 0)
    s_lane = jnp.bitwise_and(_iota2((c, tlanes), 1), c - 1)
    strict = s_lane < t_row
    incl = s_lane <= t_row

    def bd_ch(x):
        return _bd_rows(x, m_ch)

    def bd_tt(x):
        return _bd_rows(x, m_tt)

    def apply(a, x):
        return _mm(a, bd_ch(x))

    cl = _each(lambda x: x[c - 1:c, :], cum)
    at = _each(lambda a, x, l: a * jnp.exp(x - l), a_, cum, lw)
    rt = _each(lambda a, x: a * jnp.exp(x), r, cum)
    bt = _each(lambda a, x: a * jnp.exp(-x), b_, cum)
    kt = _each(lambda a, x: a * jnp.exp(-x), k, cum)
    bh = _each(lambda a, x, xl: a * jnp.exp(xl - x), b_, cum, cl)
    kh = _each(lambda a, x, xl: a * jnp.exp(xl - x), k, cum, cl)

    lhs = _each(lambda a, b: jnp.concatenate([a, b], axis=0), at, rt)
    ob = _each(lambda a, x: _mm(a, bd_ch(x), NT), lhs, bt)
    ok = _each(lambda a, x: _mm(a, bd_ch(x), NT), lhs, kt)
    a_ab = _each(lambda x: jnp.where(strict, x[:c], 0.0), ob)
    a_rb = _each(lambda x: jnp.where(incl, x[c:], 0.0), ob)
    a_ak = _each(lambda x: jnp.where(strict, x[:c], 0.0), ok)
    a_rk = _each(lambda x: jnp.where(incl, x[c:], 0.0), ok)
    akv = _each(lambda a, b, x: apply(jnp.concatenate([a, b], axis=0), x), a_ak, a_rk, v)
    av = _each(lambda x: x[:c], akv)
    y_kv = _each(lambda x: x[c:], akv)

    eye = jnp.where(s_lane == t_row, 1.0, 0.0)
    tinv = _each(lambda x: eye + x, a_ab)
    npow = _each(lambda x: _mm(x, bd_tt(x)), a_ab)
    for _ in range(int(math.log2(c)) - 2):
        both = _each(lambda t, x: _mm(jnp.concatenate([t, x], axis=0), bd_tt(x)), tinv, npow)
        tinv = _each(lambda t, z: t + z[:c], tinv, both)
        npow = _each(lambda z: z[c:], both)
    tinv = _each(lambda t, x: t + _mm(t, bd_tt(x)), tinv, npow)

    wt = _each(apply, tinv, at)
    w = _each(apply, tinv, av)
    qt = _each(lambda x, a, y: x + apply(a, y), rt, a_rb, wt)
    yi = _each(lambda a, x, y: apply(a, x) + y, a_rb, w, y_kv)
    g_off = _each(lambda x, y: jnp.where(m_ss, _mm(x, y, TN), 0.0), wt, bh)
    h_t = _each(lambda x, y, p, q: jnp.where(
        m_ss, _mm(jnp.concatenate([x, p], axis=0), jnp.concatenate([y, q], axis=0), TN), 0.0),
                w, bh, v, kh)

    y = _each(lambda q, s, x: _mm(q, s, NT) + x, qt, s_bd, yi)
    s_new = _each(lambda s, xl, g, h: s * jnp.exp(xl) + _mm(s, g) + h, s_bd, cl, g_off, h_t)
    return y, s_new


def _scan_kernel(*refs, c, d, nq, has_state, lane_packed):
    if has_state:
        (r_ref, k_ref, v_ref, w_ref, a_ref, g_ref, p_ref, s0_ref, y_ref, so_ref, sbd_ref) = refs
    else:
        (r_ref, k_ref, v_ref, w_ref, a_ref, g_ref, p_ref, y_ref, so_ref, sbd_ref) = refs
    ci = pl.program_id(1)
    hs = HEAD_SIZE
    ng = d // GROUP_LANES
    chains = [(q, g) for q in range(nq) for g in range(ng)]
    n_ch = len(chains)
    m_ss = _same_block((GROUP_LANES, GROUP_LANES), hs, hs)
    ones_bd = jnp.where(m_ss, 1.0, 0.0).astype(BF16)

    def lanes(g):
        return slice(g * GROUP_LANES, (g + 1) * GROUP_LANES)

    def rd(ref, q, g):
        if lane_packed:
            return ref[:, q * d + g * GROUP_LANES:q * d + (g + 1) * GROUP_LANES]
        return ref[q, :, lanes(g)]

    def par(row, g):
        return p_ref[row:row + 1, lanes(g)]

    @pl.when(ci == 0)
    def _():
        for i, (q, g) in enumerate(chains):
            if has_state:
                heads = [s0_ref[q, HEADS_PER_GROUP * g + h] for h in range(HEADS_PER_GROUP)]
                sbd_ref[i] = _bd_rows(jnp.concatenate(heads, axis=1), m_ss)
            else:
                sbd_ref[i] = jnp.zeros((GROUP_LANES, GROUP_LANES), F32)

    def log_decay(wq):
        z = -wq
        softplus = jnp.maximum(z, 0.0) + jnp.log(1.0 + jnp.exp(-jnp.abs(z)))
        lw = -jnp.exp(-softplus - 0.5)
        row = _iota2(lw.shape, 0)
        cum = lw
        step = 1
        while step < c:
            cum = cum + jnp.where(row >= step, pltpu.roll(cum, step, 0), 0.0)
            step *= 2
        return lw, cum

    if lane_packed:
        lw_all, cum_all = log_decay(w_ref[...])
        lw = [lw_all[:, q * d + g * GROUP_LANES:q * d + (g + 1) * GROUP_LANES] for q, g in chains]
        cum = [cum_all[:, q * d + g * GROUP_LANES:q * d + (g + 1) * GROUP_LANES] for q, g in chains]
    else:
        per_q = [log_decay(w_ref[q]) for q in range(nq)]
        lw = [per_q[q][0][:, lanes(g)] for q, g in chains]
        cum = [per_q[q][1][:, lanes(g)] for q, g in chains]

    def seg_sum(xs, split=True):
        x = jnp.concatenate(xs, axis=0)
        y = _split_mm(x, ones_bd) if split else _mm(x, ones_bd)
        return [y[i * c:(i + 1) * c] for i in range(len(xs))]

    r = [rd(r_ref, q, g) for q, g in chains]
    k = [rd(k_ref, q, g) for q, g in chains]
    v = [rd(v_ref, q, g) for q, g in chains]
    a_sig = [rd(a_ref, q, g) for q, g in chains]
    kk = [x * par(0, g) for x, (q, g) in zip(k, chains)]
    k = [x * (1.0 + (a - 1.0) * par(1, g)) for x, a, (q, g) in zip(k, a_sig, chains)]
    kk = _each(lambda x, n: x / jnp.maximum(jnp.sqrt(n), NORM_EPS), kk,
               seg_sum(_each(lambda x: x * x, kk)))
    bonus = _each(lambda s, z: s * z,
                  seg_sum([x * y * par(2, g) for x, y, (q, g) in zip(r, k, chains)], split=False),
                  v)
    y, s_new = _wkv_chunk(r, k, v, _each(lambda x: -x, kk), _each(lambda x, a: x * a, kk, a_sig),
                          lw, cum, [sbd_ref[i] for i in range(n_ch)], c)
    for i in range(n_ch):
        sbd_ref[i] = s_new[i]

    mu = _each(lambda x: x * (1.0 / hs), seg_sum(y))
    dy = _each(lambda x, m: x - m, y, mu)
    var = _each(lambda x: x * (1.0 / hs), seg_sum(_each(lambda x: x * x, dy), split=False))
    for i, (q, g) in enumerate(chains):
        yn = dy[i] * lax.rsqrt(var[i] + GN_EPS) * par(3, g) + par(4, g)
        out = ((yn + bonus[i]) * rd(g_ref, q, g)).astype(BF16)
        if lane_packed:
            y_ref[:, q * d + g * GROUP_LANES:q * d + (g + 1) * GROUP_LANES] = out
        else:
            y_ref[q, :, lanes(g)] = out

    @pl.when(ci == pl.num_programs(1) - 1)
    def _():
        for i, (q, g) in enumerate(chains):
            s = sbd_ref[i]
            s = s[0:hs] + s[hs:2 * hs] + s[2 * hs:3 * hs] + s[3 * hs:4 * hs]
            for h in range(HEADS_PER_GROUP):
                so_ref[q, HEADS_PER_GROUP * g + h] = s[:, h * hs:(h + 1) * hs]


def _scan_call(r, k, v, w, a, g, p, s0, nb, t, c, nq, time_major):
    m, d = r.shape
    nc = t // c
    ng = d // GROUP_LANES
    nh = d // HEAD_SIZE
    has_state = s0 is not None
    if time_major:
        r, k, v, w, a, g = (z.reshape(t, nb * d) for z in (r, k, v, w, a, g))
        xspec = pl.BlockSpec((c, nq * d), lambda b, ci: (ci, b))
    else:
        r, k, v, w, a, g = (z.reshape(nb, t, d) for z in (r, k, v, w, a, g))
        xspec = pl.BlockSpec((nq, c, d), lambda b, ci: (b, ci, 0))
    sspec = pl.BlockSpec((nq, nh, HEAD_SIZE, HEAD_SIZE), lambda b, ci: (b, 0, 0, 0))
    in_specs = [xspec] * 6 + [pl.BlockSpec(p.shape, lambda b, ci: (0, 0))]
    args = [r, k, v, w, a, g, p]
    if has_state:
        in_specs.append(sspec)
        args.append(s0)
    y, s_fin = pl.pallas_call(
        functools.partial(_scan_kernel, c=c, d=d, nq=nq, has_state=has_state,
                          lane_packed=time_major),
        grid=(nb // nq, nc),
        in_specs=in_specs,
        out_specs=[xspec, sspec],
        out_shape=[jax.ShapeDtypeStruct(r.shape, BF16),
                   jax.ShapeDtypeStruct((nb, nh, HEAD_SIZE, HEAD_SIZE), F32)],
        scratch_shapes=[pltpu.VMEM((nq * ng, GROUP_LANES, GROUP_LANES), F32)],
        compiler_params=_params(2),
    )(*args)
    return y.reshape(m, d), s_fin


def _trunk(x3, mod, conv_state, shift_state, wkv_state, wts, *, time_major, tile, out_tile,
           prep_tile, chunk, scan_seqs):
    d = x3.shape[2]
    nb, t = (x3.shape[1], x3.shape[0]) if time_major else (x3.shape[0], x3.shape[1])
    depth = mod.shape[0]
    lay = _Layout(x3.shape, time_major, tile)
    out_lay = _Layout(x3.shape, time_major, out_tile)
    prep_lay = _Layout(x3.shape, time_major, prep_tile)
    m = nb * t
    new_conv, new_shift, new_wkv = [], [], []

    subs = [(l, sub) for l in range(depth) for sub in range(3)]

    def takes_h(l, sub):
        return not (sub == 1 and l % 2 == 1)

    def nxt_of(l, sub):
        i = subs.index((l, sub)) + 1
        if i < len(subs) and takes_h(*subs[i]):
            return (wts["norm_pre"],) + subs[i]
        return None

    def ffn(x3, h3, l, s, sub):
        w_out = wts["ffn_w_out_bf16"].get((l, s))
        act, cast = _ffn_in_call(h3, lay, wts["ffn_w_in"], l, s, 512,
                                 wts["ffn_w_out"] if w_out is None else None)
        if w_out is None:
            w_out = wts["ffn_w_out_bf16"][(l, s)] = cast
        return _out_proj_call(x3, out_lay, mod, wts["norm_post"], act, w_out[None, None],
                              l, (0, 0), sub, HALF_STEP, nxt_of(l, sub))

    h3 = _pre_call(x3, out_lay, mod, wts["norm_pre"], 0, 0)
    for l in range(depth):
        jl = l // 2
        x3, h3 = ffn(x3, h3, l, 0, 0)
        if l % 2 == 0:
            st = conv_state[jl].transpose(1, 0, 2) if time_major else None
            bz, cs = _conv_call(h3, lay, wts["conv_w_in"], wts["conv_w"], st, jl, 256)
            x3, h3 = _out_proj_call(x3, out_lay, mod, wts["norm_post"], bz, wts["conv_w_out"],
                                    l, (jl, 0), 1, 1.0, nxt_of(l, 1))
            if time_major:
                new_conv.append(cs.transpose(1, 0, 2))
            else:
                tps = lay.tiles_per_seq
                new_conv.append(cs[tps - 1::tps])
        else:
            st = shift_state[jl] if time_major else None
            xr, xk, xv, tw, ta, tg, ss = _rw_prep_call(
                x3, prep_lay, mod, wts["norm_pre"], wts["rw_mix"], wts["rw_w1"], wts["rw_a1"],
                wts["rw_g1"], st, l, jl, 1)
            flat = lambda z: z.reshape(m, z.shape[2])
            r, k, v, w, a, g = _rw_proj_call(
                flat(xr), flat(xk), flat(xv), flat(tw), flat(ta), flat(tg),
                wts["rw_wr"], wts["rw_wk"], wts["rw_wv"], wts["rw_w2"], wts["rw_a2"],
                wts["rw_g2"], wts["rw_w0"], wts["rw_a0"], jl, lay.tile_rows, 256)
            p = jnp.concatenate([
                wts["rw_kk"][jl][None], wts["rw_ka"][jl][None], wts["rw_rk"][jl].reshape(1, d),
                wts["rw_lnw"][jl][None], wts["rw_lnb"][jl][None],
                jnp.zeros((3, d), F32)], axis=0)
            s0 = wkv_state[jl] if time_major else None
            yg, s_fin = _scan_call(r, k, v, w, a, g, p, s0, nb, t, chunk, scan_seqs, time_major)
            x3, h3 = _out_proj_call(x3, out_lay, mod, wts["norm_post"], yg.reshape(x3.shape),
                                    wts["rw_wo"], l, (jl, 0), 1, 1.0, nxt_of(l, 1))
            new_shift.append(ss.reshape(nb, d))
            new_wkv.append(s_fin)
        x3, h3 = ffn(x3, h3, l, 1, 2)
    def stack(xs):
        return xs[0][None] if len(xs) == 1 else jnp.stack(xs)

    return x3, stack(new_conv), stack(new_shift), stack(new_wkv)


def kernel(x_prompt, x_sample, state_conv, state_shift, state_wkv, c_prompt, c_sample, mod_w, mod_b, norm_pre, norm_post, ffn_w_in, ffn_w_out, conv_w_in, conv_w, conv_w_out, rw_mix, rw_w0, rw_w1, rw_w2, rw_a0, rw_a1, rw_a2, rw_g1, rw_g2, rw_kk, rw_ka, rw_rk, rw_wr, rw_wk, rw_wv, rw_wo, rw_lnw, rw_lnb):
    b, t, d = x_prompt.shape
    sb, st, _ = x_sample.shape
    depth = mod_w.shape[0]
    n_sub = norm_pre.shape[1]
    wts = dict(norm_pre=norm_pre.reshape(depth, n_sub, 1, d),
               norm_post=norm_post.reshape(depth, n_sub, 1, d), ffn_w_in=ffn_w_in,
               ffn_w_out=ffn_w_out, ffn_w_out_bf16={},
               conv_w_out=conv_w_out.astype(BF16)[:, None],
               rw_wo=rw_wo.astype(BF16)[:, None],
               conv_w_in=conv_w_in, conv_w=conv_w, rw_mix=rw_mix,
               rw_w0=rw_w0, rw_w1=rw_w1, rw_w2=rw_w2, rw_a0=rw_a0, rw_a1=rw_a1, rw_a2=rw_a2,
               rw_g1=rw_g1, rw_g2=rw_g2, rw_kk=rw_kk, rw_ka=rw_ka, rw_rk=rw_rk, rw_wr=rw_wr,
               rw_wk=rw_wk, rw_wv=rw_wv, rw_lnw=rw_lnw, rw_lnb=rw_lnb)

    n_c = b + sb
    pad = (-n_c) % 8
    c_all = jnp.concatenate([c_prompt, c_sample, jnp.zeros((pad, d), F32)], axis=0)
    mod_all = _mod_call(c_all, mod_w, mod_b)
    mod_p = (mod_all[:, :b].reshape(depth, b, N_MOD, d).transpose(0, 2, 1, 3)
             .reshape(depth, N_MOD, b, 1, d))
    mod_s = mod_all[:, b:n_c].reshape(depth, sb, N_MOD, d).transpose(0, 2, 1, 3)

    y_p, conv_p, shift_p, wkv_p = _trunk(
        x_prompt, mod_p, None, None, None, wts, time_major=False,
        tile=min(t, 1024), out_tile=min(t, 512), prep_tile=min(t, 256), chunk=min(t, 64),
        scan_seqs=math.gcd(b, 2))
    y_s, conv_s, shift_s, wkv_s = _trunk(
        x_sample.transpose(1, 0, 2), mod_s, state_conv, state_shift, state_wkv, wts,
        time_major=True, tile=sb, out_tile=min(sb, 32), prep_tile=min(sb, 32), chunk=st,
        scan_seqs=math.gcd(sb, 2))
    return (y_p, y_s.transpose(1, 0, 2), conv_p, shift_p, wkv_p, conv_s, shift_s, wkv_s)
```

```python
import functools
import math

import jax
import jax.numpy as jnp
from jax import lax
from jax.experimental import pallas as pl
from jax.experimental.pallas import tpu as pltpu

F32 = jnp.float32
BF16 = jnp.bfloat16

RMS_EPS = 1e-6
GN_EPS = 64e-5
NORM_EPS = 1e-12
HALF_STEP = 0.5
HEAD_SIZE = 64
N_MOD = 9

MXU_WIDTH_V7X = 256
GROUP_LANES = MXU_WIDTH_V7X
HEADS_PER_GROUP = GROUP_LANES // HEAD_SIZE
VMEM_LIMIT_V7X = 60 * 2**20
ROW_PIECE = 256

NN = (((1,), (0,)), ((), ()))
NT = (((1,), (1,)), ((), ()))
TN = (((0,), (0,)), ((), ()))


def _mm(a, b, dims=NN):
    return lax.dot_general(a.astype(BF16), b.astype(BF16), dims, preferred_element_type=F32)


def _sigmoid(x):
    return 1.0 / (1.0 + jnp.exp(-x))


def _params(n_axes):
    return pltpu.CompilerParams(dimension_semantics=("arbitrary",) * n_axes,
                                vmem_limit_bytes=VMEM_LIMIT_V7X)


def _rms(x, g):
    return x * lax.rsqrt(jnp.mean(x * x, axis=-1, keepdims=True) + RMS_EPS) * g


def _modulated_pre(x, m_ref, g):
    return _rms(x, g) * (1.0 + m_ref[1]) + m_ref[0]


def _gated_post(x, y, m_ref, g, res_w):
    return x + (res_w * m_ref[2]) * _rms(y, g)


def _pieces(lead, rows):
    if lead == 1:
        n = min(ROW_PIECE, rows)
        return [(slice(0, 1), slice(r0, r0 + n)) for r0 in range(0, rows, n)]
    n = min(lead, max(1, ROW_PIECE // rows))
    return [(slice(a0, a0 + n), slice(0, rows)) for a0 in range(0, lead, n)]


def _piece_rows(x_ref, ls, rs):
    shp = x_ref[ls, rs, :].shape
    return shp, shp[0] * shp[1]


def _shift_time(u, k, fill, time_major):
    lead, rows, n = u.shape
    if time_major:
        return jnp.concatenate([fill, u[:lead - k]], axis=0)
    out = pltpu.roll(u.reshape(rows, n), k, 0)
    row = lax.broadcasted_iota(jnp.int32, (rows, n), 0)
    for t in range(k):
        out = jnp.where(row == t, fill[t:t + 1, :], out)
    return out.reshape(1, rows, n)


class _Layout:
    def __init__(self, shape, time_major, rows_per_tile):
        self.time_major = time_major
        self.shape = shape
        lead, rows, d = shape
        if time_major:
            self.block = (lead, rows_per_tile, d)
            self.tiles_per_seq = 1
            self.n_tiles = rows // rows_per_tile
        else:
            self.block = (1, rows_per_tile, d)
            self.tiles_per_seq = rows // rows_per_tile
            self.n_tiles = lead * self.tiles_per_seq
        self.tile_rows = self.block[0] * self.block[1]

    def xmap(self, i, *_):
        if self.time_major:
            return (0, i, 0)
        return (i // self.tiles_per_seq, i % self.tiles_per_seq, 0)

    def x_spec(self, **kw):
        return pl.BlockSpec(self.block, self.xmap, **kw)

    def mod_spec(self, mod, l, sub):
        d = self.shape[2]
        if self.time_major:
            return pl.BlockSpec((None, 3, self.block[1], d), lambda i, *_: (l, sub, i, 0))
        tps = self.tiles_per_seq
        return pl.BlockSpec((None, 3, None, 1, d), lambda i, *_: (l, sub, i // tps, 0, 0))


def _gain_spec(gains, l, sub):
    return pl.BlockSpec((None, None, 1, gains.shape[3]), lambda i, *_: (l, sub, 0, 0))


def _mod_kernel(c_ref, w_ref, b_ref, o_ref):
    c = c_ref[...]
    o_ref[...] = _mm(c * _sigmoid(c), w_ref[...]) + b_ref[...]


def _mod_call(c_all, mod_w, mod_b):
    depth, d, n = mod_w.shape
    nbp = c_all.shape[0]
    tn = math.gcd(n, 1024)
    return pl.pallas_call(
        _mod_kernel,
        grid=(depth, n // tn),
        in_specs=[
            pl.BlockSpec((nbp, d), lambda l, j: (0, 0)),
            pl.BlockSpec((None, d, tn), lambda l, j: (l, 0, j)),
            pl.BlockSpec((None, 1, tn), lambda l, j: (l, 0, j)),
        ],
        out_specs=pl.BlockSpec((None, nbp, tn), lambda l, j: (l, 0, j)),
        out_shape=jax.ShapeDtypeStruct((depth, nbp, n), F32),
        compiler_params=_params(2),
    )(c_all, mod_w, mod_b.reshape(depth, 1, n))


def _out_proj_kernel(*refs, res_w, emit_next):
    if emit_next:
        x_ref, m_ref, gpost_ref, a_ref, w_ref, mn_ref, gn_ref, o_ref, h_ref = refs
    else:
        x_ref, m_ref, gpost_ref, a_ref, w_ref, o_ref = refs
    lead, rows, _ = x_ref.shape
    for ls, rs in _pieces(lead, rows):
        shp, n = _piece_rows(x_ref, ls, rs)
        a = a_ref[ls, rs, :]
        y = lax.dot_general(a.reshape(n, a.shape[2]), w_ref[...], NN,
                            preferred_element_type=F32).reshape(shp)
        o = _gated_post(x_ref[ls, rs, :], y, m_ref, gpost_ref[...], res_w)
        o_ref[ls, rs, :] = o
        if emit_next:
            h_ref[ls, rs, :] = _modulated_pre(o, mn_ref, gn_ref[...]).astype(BF16)


def _out_proj_call(x3, lay, mod, norm_post, act, w, l, widx, sub, res_w, nxt=None):
    k, d = w.shape[2], w.shape[3]
    in_specs = [
        lay.x_spec(),
        lay.mod_spec(mod, l, sub),
        _gain_spec(norm_post, l, sub),
        pl.BlockSpec((lay.block[0], lay.block[1], k), lay.xmap),
        pl.BlockSpec((None, None, k, d), lambda i: (widx[0], widx[1], 0, 0),
                     pipeline_mode=pl.Buffered(1)),
    ]
    args = [x3, mod, norm_post, act, w]
    out_specs = [lay.x_spec()]
    out_shape = [jax.ShapeDtypeStruct(x3.shape, F32)]
    if nxt is not None:
        norm_pre, ln, subn = nxt
        in_specs += [lay.mod_spec(mod, ln, subn), _gain_spec(norm_pre, ln, subn)]
        args += [mod, norm_pre]
        out_specs.append(lay.x_spec())
        out_shape.append(jax.ShapeDtypeStruct(x3.shape, BF16))
    out = pl.pallas_call(
        functools.partial(_out_proj_kernel, res_w=res_w, emit_next=nxt is not None),
        grid=(lay.n_tiles,),
        in_specs=in_specs,
        out_specs=out_specs,
        out_shape=out_shape,
        compiler_params=_params(1),
    )(*args)
    return (out[0], out[1]) if nxt is not None else (out[0], None)


def _pre_kernel(x_ref, m_ref, g_ref, h_ref):
    lead, rows, _ = x_ref.shape
    for ls, rs in _pieces(lead, rows):
        h_ref[ls, rs, :] = _modulated_pre(x_ref[ls, rs, :], m_ref, g_ref[...]).astype(BF16)


def _pre_call(x3, lay, mod, norm_pre, l, sub):
    return pl.pallas_call(
        _pre_kernel,
        grid=(lay.n_tiles,),
        in_specs=[lay.x_spec(), lay.mod_spec(mod, l, sub), _gain_spec(norm_pre, l, sub)],
        out_specs=lay.x_spec(),
        out_shape=jax.ShapeDtypeStruct(x3.shape, BF16),
        compiler_params=_params(1),
    )(x3, mod, norm_pre)


def _cast_weights_once(i, pairs):
    @pl.when(i == 0)
    def _():
        for src, dst in pairs:
            dst[...] = src[...].astype(BF16)


def _ffn_in_kernel(*refs, cast_w_out):
    if cast_w_out:
        h_ref, wg_ref, wu_ref, wo_ref, a_ref, wob_ref, wgb_ref, wub_ref = refs
        _cast_weights_once(pl.program_id(1), [(wo_ref, wob_ref)])
    else:
        h_ref, wg_ref, wu_ref, a_ref, wgb_ref, wub_ref = refs
    _cast_weights_once(pl.program_id(1), [(wg_ref, wgb_ref), (wu_ref, wub_ref)])
    lead, rows, d = h_ref.shape
    for ls, rs in _pieces(lead, rows):
        shp, n = _piece_rows(h_ref, ls, rs)
        h = h_ref[ls, rs, :].reshape(n, d)
        gt = lax.dot_general(h, wgb_ref[...], NN, preferred_element_type=F32)
        up = lax.dot_general(h, wub_ref[...], NN, preferred_element_type=F32)
        a_ref[ls, rs, :] = (gt * _sigmoid(gt) * up).astype(BF16).reshape(shp[0], shp[1], -1)


def _ffn_in_call(h3, lay, w_in, l, s, tf, w_out=None):
    lead, rows, d = h3.shape
    f = w_in.shape[3] // 2
    nj = f // tf
    in_specs = [
        pl.BlockSpec(lay.block, lambda j, i: lay.xmap(i)),
        pl.BlockSpec((None, None, d, tf), lambda j, i: (l, s, 0, j)),
        pl.BlockSpec((None, None, d, tf), lambda j, i: (l, s, 0, nj + j)),
    ]
    args = [h3, w_in, w_in]
    out_specs = [pl.BlockSpec((lay.block[0], lay.block[1], tf),
                              lambda j, i: lay.xmap(i)[:2] + (j,))]
    out_shape = [jax.ShapeDtypeStruct((lead, rows, f), BF16)]
    if w_out is not None:
        in_specs.append(pl.BlockSpec((None, None, tf, d), lambda j, i: (l, s, j, 0)))
        args.append(w_out)
        out_specs.append(pl.BlockSpec((tf, d), lambda j, i: (j, 0)))
        out_shape.append(jax.ShapeDtypeStruct((f, d), BF16))
    out = pl.pallas_call(
        functools.partial(_ffn_in_kernel, cast_w_out=w_out is not None),
        grid=(nj, lay.n_tiles),
        in_specs=in_specs,
        out_specs=out_specs,
        out_shape=out_shape,
        scratch_shapes=[pltpu.VMEM((d, tf), BF16), pltpu.VMEM((d, tf), BF16)],
        compiler_params=_params(2),
    )(*args)
    return (out[0], out[1]) if w_out is not None else (out[0], None)


def _conv_kernel(*refs, time_major, tiles_per_seq):
    if time_major:
        (h_ref, wb_ref, wc_ref, wx_ref, cw_ref, st_ref,
         o_ref, so_ref, wbb_ref, wcb_ref, wxb_ref) = refs
    else:
        (h_ref, wb_ref, wc_ref, wx_ref, cw_ref,
         o_ref, so_ref, wbb_ref, wcb_ref, wxb_ref, carry_ref) = refs
    i = pl.program_id(1)
    _cast_weights_once(i, [(wb_ref, wbb_ref), (wc_ref, wcb_ref), (wx_ref, wxb_ref)])
    lead, rows, d = h_ref.shape
    tm = lead * rows

    h = h_ref[...].reshape(tm, d)

    def proj(w_ref):
        return lax.dot_general(h, w_ref[...], NN, preferred_element_type=F32)

    bg = proj(wbb_ref)
    u2 = proj(wcb_ref) * proj(wxb_ref)
    tn = u2.shape[1]
    u = u2.reshape(lead, rows, tn)
    if time_major:
        st = st_ref[...]
        fill1, fill2 = st[1:2], st
        so_ref[...] = u[lead - 2:lead]
    else:
        @pl.when(i % tiles_per_seq == 0)
        def _():
            carry_ref[...] = jnp.zeros((8, tn), F32)
        prev = carry_ref[...]
        fill1, fill2 = prev[7:8, :], prev[6:8, :]
        carry_ref[...] = u2[tm - 8:tm, :]
        so_ref[...] = u[:, rows - 2:rows, :]
    cw = cw_ref[...]
    z = (cw[0:1, :] * _shift_time(u, 2, fill2, time_major)
         + cw[1:2, :] * _shift_time(u, 1, fill1, time_major) + cw[2:3, :] * u)
    o_ref[...] = (bg.reshape(lead, rows, tn) * z).astype(BF16)


def _conv_call(h3, lay, w_in, cw, state, jl, tn):
    lead_n, rows_n, d = h3.shape
    dc = cw.shape[2]
    nj = dc // tn
    in_specs = [
        pl.BlockSpec(lay.block, lambda j, i: lay.xmap(i)),
        pl.BlockSpec((None, d, tn), lambda j, i: (jl, 0, j)),
        pl.BlockSpec((None, d, tn), lambda j, i: (jl, 0, nj + j)),
        pl.BlockSpec((None, d, tn), lambda j, i: (jl, 0, 2 * nj + j)),
        pl.BlockSpec((None, cw.shape[1], tn), lambda j, i: (jl, 0, j)),
    ]
    args = [h3, w_in, w_in, w_in, cw]
    scratch = [pltpu.VMEM((d, tn), BF16)] * 3
    if lay.time_major:
        nb = lay.block[1]
        in_specs.append(pl.BlockSpec((2, nb, tn), lambda j, i: (0, i, j)))
        args.append(state)
        so_spec = pl.BlockSpec((2, nb, tn), lambda j, i: (0, i, j))
        so_shape = jax.ShapeDtypeStruct((2, h3.shape[1], dc), F32)
    else:
        scratch.append(pltpu.VMEM((8, tn), F32))
        so_spec = pl.BlockSpec((1, 2, tn), lambda j, i: (i, 0, j))
        so_shape = jax.ShapeDtypeStruct((lay.n_tiles, 2, dc), F32)
    return pl.pallas_call(
        functools.partial(_conv_kernel, time_major=lay.time_major,
                          tiles_per_seq=lay.tiles_per_seq),
        grid=(nj, lay.n_tiles),
        in_specs=in_specs,
        out_specs=[pl.BlockSpec((lay.block[0], lay.block[1], tn),
                                lambda j, i: lay.xmap(i)[:2] + (j,)), so_spec],
        out_shape=[jax.ShapeDtypeStruct((lead_n, rows_n, dc), BF16), so_shape],
        scratch_shapes=scratch,
        compiler_params=_params(2),
    )(*args)


def _rw_prep_kernel(*refs, time_major, tiles_per_seq):
    if time_major:
        (x_ref, m_ref, gpre_ref, mix_ref, w1_ref, a1_ref, g1_ref, st_ref,
         xr_ref, xk_ref, xv_ref, tw_ref, ta_ref, tg_ref, so_ref) = refs
    else:
        (x_ref, m_ref, gpre_ref, mix_ref, w1_ref, a1_ref, g1_ref,
         xr_ref, xk_ref, xv_ref, tw_ref, ta_ref, tg_ref, so_ref, carry_ref) = refs
    i = pl.program_id(0)
    lead, rows, d = x_ref.shape
    n = lead * rows
    h = _modulated_pre(x_ref[...], m_ref, gpre_ref[...])
    if time_major:
        fill = st_ref[...][None]
        so_ref[...] = h[lead - 1]
    else:
        @pl.when(i % tiles_per_seq == 0)
        def _():
            carry_ref[...] = jnp.zeros((8, d), F32)
        fill = carry_ref[7:8, :]
        carry_ref[...] = h[0, rows - 8:rows, :]
        so_ref[...] = h[:, rows - 1:rows, :]
    xx = _shift_time(h, 1, fill, time_major) - h
    mix = mix_ref[...]

    def mixed(k):
        return h + xx * mix[k:k + 1, :]

    def low_rank(k, w_ref):
        return _mm(mixed(k).reshape(n, d), w_ref[...]).reshape(lead, rows, w_ref.shape[1])

    xr_ref[...] = mixed(0).astype(BF16)
    tw_ref[...] = jnp.tanh(low_rank(1, w1_ref))
    xk_ref[...] = mixed(2).astype(BF16)
    xv_ref[...] = mixed(3).astype(BF16)
    ta_ref[...] = low_rank(4, a1_ref)
    tg_ref[...] = _sigmoid(low_rank(5, g1_ref))


def _rw_prep_call(x3, lay, mod, norm_pre, mix, w1, a1, g1, state, l, jl, sub):
    lead, rows, d = x3.shape
    dl, dg = w1.shape[2], g1.shape[2]
    in_specs = [
        lay.x_spec(),
        lay.mod_spec(mod, l, sub),
        _gain_spec(norm_pre, l, sub),
        pl.BlockSpec((None, mix.shape[1], d), lambda i: (jl, 0, 0)),
        pl.BlockSpec((None, d, dl), lambda i: (jl, 0, 0)),
        pl.BlockSpec((None, d, dl), lambda i: (jl, 0, 0)),
        pl.BlockSpec((None, d, dg), lambda i: (jl, 0, 0)),
    ]
    args = [x3, mod, norm_pre, mix, w1, a1, g1]
    scratch = []
    blk = lay.block
    if lay.time_major:
        in_specs.append(pl.BlockSpec((blk[1], d), lambda i: (i, 0)))
        args.append(state)
        so_spec = pl.BlockSpec((blk[1], d), lambda i: (i, 0))
        so_shape = jax.ShapeDtypeStruct((rows, d), F32)
    else:
        scratch.append(pltpu.VMEM((8, d), F32))
        tps = lay.tiles_per_seq
        so_spec = pl.BlockSpec((1, 1, d), lambda i: (i // tps, 0, 0))
        so_shape = jax.ShapeDtypeStruct((lead, 1, d), F32)

    def ospec(width):
        return pl.BlockSpec((blk[0], blk[1], width), lay.xmap)

    return pl.pallas_call(
        functools.partial(_rw_prep_kernel, time_major=lay.time_major,
                          tiles_per_seq=lay.tiles_per_seq),
        grid=(lay.n_tiles,),
        in_specs=in_specs,
        out_specs=[ospec(d)] * 3 + [ospec(dl), ospec(dl), ospec(dg), so_spec],
        out_shape=[jax.ShapeDtypeStruct((lead, rows, d), BF16)] * 3
        + [jax.ShapeDtypeStruct((lead, rows, dl), F32)] * 2
        + [jax.ShapeDtypeStruct((lead, rows, dg), F32), so_shape],
        scratch_shapes=scratch,
        compiler_params=_params(1),
    )(*args)


def _rw_proj_kernel(xr_ref, xk_ref, xv_ref, tw_ref, ta_ref, tg_ref, wr_ref, wk_ref, wv_ref,
                    w2_ref, a2_ref, g2_ref, w0_ref, a0_ref,
                    r_ref, k_ref, v_ref, w_ref, a_ref, g_ref, wrb_ref, wkb_ref, wvb_ref):
    _cast_weights_once(pl.program_id(1), [(wr_ref, wrb_ref), (wk_ref, wkb_ref), (wv_ref, wvb_ref)])

    def proj(x_ref, wb_ref):
        return lax.dot_general(x_ref[...], wb_ref[...], NN, preferred_element_type=F32)

    r_ref[...] = proj(xr_ref, wrb_ref)
    k_ref[...] = proj(xk_ref, wkb_ref)
    v_ref[...] = proj(xv_ref, wvb_ref)
    w_ref[...] = w0_ref[...] + _mm(tw_ref[...], w2_ref[...])
    a_ref[...] = _sigmoid(a0_ref[...] + _mm(ta_ref[...], a2_ref[...]))
    g_ref[...] = _mm(tg_ref[...], g2_ref[...])


def _rw_proj_call(xr, xk, xv, tw, ta, tg, wr, wk, wv, w2, a2, g2, w0, a0, jl, tm, tn):
    m, d = xr.shape
    dl, dg = tw.shape[1], tg.shape[1]
    nr = w0.shape[0]
    xspec = pl.BlockSpec((tm, d), lambda j, i: (i, 0))
    wspec = pl.BlockSpec((None, d, tn), lambda j, i: (jl, 0, j))
    vspec = pl.BlockSpec((None, 1, tn), lambda j, i: (jl, 0, j))
    ospec = pl.BlockSpec((tm, tn), lambda j, i: (i, j))
    return pl.pallas_call(
        _rw_proj_kernel,
        grid=(d // tn, m // tm),
        in_specs=[xspec, xspec, xspec,
                  pl.BlockSpec((tm, dl), lambda j, i: (i, 0)),
                  pl.BlockSpec((tm, dl), lambda j, i: (i, 0)),
                  pl.BlockSpec((tm, dg), lambda j, i: (i, 0)),
                  wspec, wspec, wspec,
                  pl.BlockSpec((None, dl, tn), lambda j, i: (jl, 0, j)),
                  pl.BlockSpec((None, dl, tn), lambda j, i: (jl, 0, j)),
                  pl.BlockSpec((None, dg, tn), lambda j, i: (jl, 0, j)),
                  vspec, vspec],
        out_specs=[ospec] * 6,
        out_shape=[jax.ShapeDtypeStruct((m, d), F32)] * 6,
        scratch_shapes=[pltpu.VMEM((d, tn), BF16)] * 3,
        compiler_params=_params(2),
    )(xr, xk, xv, tw, ta, tg, wr, wk, wv, w2, a2, g2,
      w0.reshape(nr, 1, d), a0.reshape(nr, 1, d))


def _iota2(shape, axis):
    return lax.broadcasted_iota(jnp.int32, shape, axis)


def _same_block(shape, row_block, lane_block):
    r = lax.shift_right_logical(_iota2(shape, 0), int(math.log2(row_block)))
    c = lax.shift_right_logical(_iota2(shape, 1), int(math.log2(lane_block)))
    return r == c


def _bd_rows(x, mask):
    return jnp.where(mask, jnp.concatenate([x] * HEADS_PER_GROUP, axis=0), 0.0)


def _split_mm(x, ones):
    hi = x.astype(BF16)
    lo = (x - hi.astype(F32)).astype(BF16)
    return (lax.dot_general(hi, ones, NN, preferred_element_type=F32)
            + lax.dot_general(lo, ones, NN, preferred_element_type=F32))


def _each(fn, *lists):
    return [fn(*xs) for xs in zip(*lists)]


def _wkv_chunk(r, k, v, a_, b_, lw, cum, s_bd, c):
    hs = HEAD_SIZE
    tlanes = HEADS_PER_GROUP * c
    m_ch = _same_block((tlanes, GROUP_LANES), c, hs)
    m_tt = _same_block((tlanes, tlanes), c, c)
    m_ss = _same_block((GROUP_LANES, GROUP_LANES), hs, hs)
    t_row = _iota2((c, tlanes), 0)
    s_lane = jnp.bitwise_and(_iota2((c, tlanes), 1), c - 1)
    strict = s_lane < t_row
    incl = s_lane <= t_row

    def bd_ch(x):
        return _bd_rows(x, m_ch)

    def bd_tt(x):
        return _bd_rows(x, m_tt)

    def apply(a, x):
        return _mm(a, bd_ch(x))

    cl = _each(lambda x: x[c - 1:c, :], cum)
    at = _each(lambda a, x, l: a * jnp.exp(x - l), a_, cum, lw)
    rt = _each(lambda a, x: a * jnp.exp(x), r, cum)
    bt = _each(lambda a, x: a * jnp.exp(-x), b_, cum)
    kt = _each(lambda a, x: a * jnp.exp(-x), k, cum)
    bh = _each(lambda a, x, xl: a * jnp.exp(xl - x), b_, cum, cl)
    kh = _each(lambda a, x, xl: a * jnp.exp(xl - x), k, cum, cl)

    lhs = _each(lambda a, b: jnp.concatenate([a, b], axis=0), at, rt)
    ob = _each(lambda a, x: _mm(a, bd_ch(x), NT), lhs, bt)
    ok = _each(lambda a, x: _mm(a, bd_ch(x), NT), lhs, kt)
    a_ab = _each(lambda x: jnp.where(strict, x[:c], 0.0), ob)
    a_rb = _each(lambda x: jnp.where(incl, x[c:], 0.0), ob)
    a_ak = _each(lambda x: jnp.where(strict, x[:c], 0.0), ok)
    a_rk = _each(lambda x: jnp.where(incl, x[c:], 0.0), ok)
    akv = _each(lambda a, b, x: apply(jnp.concatenate([a, b], axis=0), x), a_ak, a_rk, v)
    av = _each(lambda x: x[:c], akv)
    y_kv = _each(lambda x: x[c:], akv)

    eye = jnp.where(s_lane == t_row, 1.0, 0.0)
    tinv = _each(lambda x: eye + x, a_ab)
    npow = _each(lambda x: _mm(x, bd_tt(x)), a_ab)
    for _ in range(int(math.log2(c)) - 2):
        both = _each(lambda t, x: _mm(jnp.concatenate([t, x], axis=0), bd_tt(x)), tinv, npow)
        tinv = _each(lambda t, z: t + z[:c], tinv, both)
        npow = _each(lambda z: z[c:], both)
    tinv = _each(lambda t, x: t + _mm(t, bd_tt(x)), tinv, npow)

    wt = _each(apply, tinv, at)
    w = _each(apply, tinv, av)
    qt = _each(lambda x, a, y: x + apply(a, y), rt, a_rb, wt)
    yi = _each(lambda a, x, y: apply(a, x) + y, a_rb, w, y_kv)
    g_off = _each(lambda x, y: jnp.where(m_ss, _mm(x, y, TN), 0.0), wt, bh)
    h_t = _each(lambda x, y, p, q: jnp.where(
        m_ss, _mm(jnp.concatenate([x, p], axis=0), jnp.concatenate([y, q], axis=0), TN), 0.0),
                w, bh, v, kh)

    y = _each(lambda q, s, x: _mm(q, s, NT) + x, qt, s_bd, yi)
    s_new = _each(lambda s, xl, g, h: s * jnp.exp(xl) + _mm(s, g) + h, s_bd, cl, g_off, h_t)
    return y, s_new


def _scan_kernel(*refs, c, d, nq, has_state, lane_packed):
    if has_state:
        (r_ref, k_ref, v_ref, w_ref, a_ref, g_ref, p_ref, s0_ref, y_ref, so_ref, sbd_ref) = refs
    else:
        (r_ref, k_ref, v_ref, w_ref, a_ref, g_ref, p_ref, y_ref, so_ref, sbd_ref) = refs
    ci = pl.program_id(1)
    hs = HEAD_SIZE
    ng = d // GROUP_LANES
    chains = [(q, g) for q in range(nq) for g in range(ng)]
    n_ch = len(chains)
    m_ss = _same_block((GROUP_LANES, GROUP_LANES), hs, hs)
    ones_bd = jnp.where(m_ss, 1.0, 0.0).astype(BF16)

    def lanes(g):
        return slice(g * GROUP_LANES, (g + 1) * GROUP_LANES)

    def rd(ref, q, g):
        if lane_packed:
            return ref[:, q * d + g * GROUP_LANES:q * d + (g + 1) * GROUP_LANES]
        return ref[q, :, lanes(g)]

    def par(row, g):
        return p_ref[row:row + 1, lanes(g)]

    @pl.when(ci == 0)
    def _():
        for i, (q, g) in enumerate(chains):
            if has_state:
                heads = [s0_ref[q, HEADS_PER_GROUP * g + h] for h in range(HEADS_PER_GROUP)]
                sbd_ref[i] = _bd_rows(jnp.concatenate(heads, axis=1), m_ss)
            else:
                sbd_ref[i] = jnp.zeros((GROUP_LANES, GROUP_LANES), F32)

    def log_decay(wq):
        z = -wq
        softplus = jnp.maximum(z, 0.0) + jnp.log(1.0 + jnp.exp(-jnp.abs(z)))
        lw = -jnp.exp(-softplus - 0.5)
        row = _iota2(lw.shape, 0)
        cum = lw
        step = 1
        while step < c:
            cum = cum + jnp.where(row >= step, pltpu.roll(cum, step, 0), 0.0)
            step *= 2
        return lw, cum

    if lane_packed:
        lw_all, cum_all = log_decay(w_ref[...])
        lw = [lw_all[:, q * d + g * GROUP_LANES:q * d + (g + 1) * GROUP_LANES] for q, g in chains]
        cum = [cum_all[:, q * d + g * GROUP_LANES:q * d + (g + 1) * GROUP_LANES] for q, g in chains]
    else:
        per_q = [log_decay(w_ref[q]) for q in range(nq)]
        lw = [per_q[q][0][:, lanes(g)] for q, g in chains]
        cum = [per_q[q][1][:, lanes(g)] for q, g in chains]

    def seg_sum(xs, split=True):
        x = jnp.concatenate(xs, axis=0)
        y = _split_mm(x, ones_bd) if split else _mm(x, ones_bd)
        return [y[i * c:(i + 1) * c] for i in range(len(xs))]

    r = [rd(r_ref, q, g) for q, g in chains]
    k = [rd(k_ref, q, g) for q, g in chains]
    v = [rd(v_ref, q, g) for q, g in chains]
    a_sig = [rd(a_ref, q, g) for q, g in chains]
    kk = [x * par(0, g) for x, (q, g) in zip(k, chains)]
    k = [x * (1.0 + (a - 1.0) * par(1, g)) for x, a, (q, g) in zip(k, a_sig, chains)]
    kk = _each(lambda x, n: x / jnp.maximum(jnp.sqrt(n), NORM_EPS), kk,
               seg_sum(_each(lambda x: x * x, kk)))
    bonus = _each(lambda s, z: s * z,
                  seg_sum([x * y * par(2, g) for x, y, (q, g) in zip(r, k, chains)], split=False),
                  v)
    y, s_new = _wkv_chunk(r, k, v, _each(lambda x: -x, kk), _each(lambda x, a: x * a, kk, a_sig),
                          lw, cum, [sbd_ref[i] for i in range(n_ch)], c)
    for i in range(n_ch):
        sbd_ref[i] = s_new[i]

    mu = _each(lambda x: x * (1.0 / hs), seg_sum(y))
    dy = _each(lambda x, m: x - m, y, mu)
    var = _each(lambda x: x * (1.0 / hs), seg_sum(_each(lambda x: x * x, dy), split=False))
    for i, (q, g) in enumerate(chains):
        yn = dy[i] * lax.rsqrt(var[i] + GN_EPS) * par(3, g) + par(4, g)
        out = ((yn + bonus[i]) * rd(g_ref, q, g)).astype(BF16)
        if lane_packed:
            y_ref[:, q * d + g * GROUP_LANES:q * d + (g + 1) * GROUP_LANES] = out
        else:
            y_ref[q, :, lanes(g)] = out

    @pl.when(ci == pl.num_programs(1) - 1)
    def _():
        for i, (q, g) in enumerate(chains):
            s = sbd_ref[i]
            s = s[0:hs] + s[hs:2 * hs] + s[2 * hs:3 * hs] + s[3 * hs:4 * hs]
            for h in range(HEADS_PER_GROUP):
                so_ref[q, HEADS_PER_GROUP * g + h] = s[:, h * hs:(h + 1) * hs]


def _scan_call(r, k, v, w, a, g, p, s0, nb, t, c, nq, time_major):
    m, d = r.shape
    nc = t // c
    ng = d // GROUP_LANES
    nh = d // HEAD_SIZE
    has_state = s0 is not None
    if time_major:
        r, k, v, w, a, g = (z.reshape(t, nb * d) for z in (r, k, v, w, a, g))
        xspec = pl.BlockSpec((c, nq * d), lambda b, ci: (ci, b))
    else:
        r, k, v, w, a, g = (z.reshape(nb, t, d) for z in (r, k, v, w, a, g))
        xspec = pl.BlockSpec((nq, c, d), lambda b, ci: (b, ci, 0))
    sspec = pl.BlockSpec((nq, nh, HEAD_SIZE, HEAD_SIZE), lambda b, ci: (b, 0, 0, 0))
    in_specs = [xspec] * 6 + [pl.BlockSpec(p.shape, lambda b, ci: (0, 0))]
    args = [r, k, v, w, a, g, p]
    if has_state:
        in_specs.append(sspec)
        args.append(s0)
    y, s_fin = pl.pallas_call(
        functools.partial(_scan_kernel, c=c, d=d, nq=nq, has_state=has_state,
                          lane_packed=time_major),
        grid=(nb // nq, nc),
        in_specs=in_specs,
        out_specs=[xspec, sspec],
        out_shape=[jax.ShapeDtypeStruct(r.shape, BF16),
                   jax.ShapeDtypeStruct((nb, nh, HEAD_SIZE, HEAD_SIZE), F32)],
        scratch_shapes=[pltpu.VMEM((nq * ng, GROUP_LANES, GROUP_LANES), F32)],
        compiler_params=_params(2),
    )(*args)
    return y.reshape(m, d), s_fin


def _trunk(x3, mod, conv_state, shift_state, wkv_state, wts, *, time_major, tile, out_tile,
           prep_tile, chunk, scan_seqs):
    d = x3.shape[2]
    nb, t = (x3.shape[1], x3.shape[0]) if time_major else (x3.shape[0], x3.shape[1])
    depth = mod.shape[0]
    lay = _Layout(x3.shape, time_major, tile)
    out_lay = _Layout(x3.shape, time_major, out_tile)
    prep_lay = _Layout(x3.shape, time_major, prep_tile)
    m = nb * t
    new_conv, new_shift, new_wkv = [], [], []

    subs = [(l, sub) for l in range(depth) for sub in range(3)]

    def takes_h(l, sub):
        return not (sub == 1 and l % 2 == 1)

    def nxt_of(l, sub):
        i = subs.index((l, sub)) + 1
        if i < len(subs) and takes_h(*subs[i]):
            return (wts["norm_pre"],) + subs[i]
        return None

    def ffn(x3, h3, l, s, sub):
        w_out = wts["ffn_w_out_bf16"].get((l, s))
        act, cast = _ffn_in_call(h3, lay, wts["ffn_w_in"], l, s, 512,
                                 wts["ffn_w_out"] if w_out is None else None)
        if w_out is None:
            w_out = wts["ffn_w_out_bf16"][(l, s)] = cast
        return _out_proj_call(x3, out_lay, mod, wts["norm_post"], act, w_out[None, None],
                              l, (0, 0), sub, HALF_STEP, nxt_of(l, sub))

    h3 = _pre_call(x3, out_lay, mod, wts["norm_pre"], 0, 0)
    for l in range(depth):
        jl = l // 2
        x3, h3 = ffn(x3, h3, l, 0, 0)
        if l % 2 == 0:
            st = conv_state[jl].transpose(1, 0, 2) if time_major else None
            bz, cs = _conv_call(h3, lay, wts["conv_w_in"], wts["conv_w"], st, jl, 256)
            x3, h3 = _out_proj_call(x3, out_lay, mod, wts["norm_post"], bz, wts["conv_w_out"],
                                    l, (jl, 0), 1, 1.0, nxt_of(l, 1))
            if time_major:
                new_conv.append(cs.transpose(1, 0, 2))
            else:
                tps = lay.tiles_per_seq
                new_conv.append(cs[tps - 1::tps])
        else:
            st = shift_state[jl] if time_major else None
            xr, xk, xv, tw, ta, tg, ss = _rw_prep_call(
                x3, prep_lay, mod, wts["norm_pre"], wts["rw_mix"], wts["rw_w1"], wts["rw_a1"],
                wts["rw_g1"], st, l, jl, 1)
            flat = lambda z: z.reshape(m, z.shape[2])
            r, k, v, w, a, g = _rw_proj_call(
                flat(xr), flat(xk), flat(xv), flat(tw), flat(ta), flat(tg),
                wts["rw_wr"], wts["rw_wk"], wts["rw_wv"], wts["rw_w2"], wts["rw_a2"],
                wts["rw_g2"], wts["rw_w0"], wts["rw_a0"], jl, min(m, 512), min(d, 512))
            p = jnp.concatenate([
                wts["rw_kk"][jl][None], wts["rw_ka"][jl][None], wts["rw_rk"][jl].reshape(1, d),
                wts["rw_lnw"][jl][None], wts["rw_lnb"][jl][None],
                jnp.zeros((3, d), F32)], axis=0)
            s0 = wkv_state[jl] if time_major else None
            yg, s_fin = _scan_call(r, k, v, w, a, g, p, s0, nb, t, chunk, scan_seqs, time_major)
            x3, h3 = _out_proj_call(x3, out_lay, mod, wts["norm_post"], yg.reshape(x3.shape),
                                    wts["rw_wo"], l, (jl, 0), 1, 1.0, nxt_of(l, 1))
            new_shift.append(ss.reshape(nb, d))
            new_wkv.append(s_fin)
        x3, h3 = ffn(x3, h3, l, 1, 2)
    def stack(xs):
        return xs[0][None] if len(xs) == 1 else jnp.stack(xs)

    return x3, stack(new_conv), stack(new_shift), stack(new_wkv)


def kernel(x_prompt, x_sample, state_conv, state_shift, state_wkv, c_prompt, c_sample, mod_w, mod_b, norm_pre, norm_post, ffn_w_in, ffn_w_out, conv_w_in, conv_w, conv_w_out, rw_mix, rw_w0, rw_w1, rw_w2, rw_a0, rw_a1, rw_a2, rw_g1, rw_g2, rw_kk, rw_ka, rw_rk, rw_wr, rw_wk, rw_wv, rw_wo, rw_lnw, rw_lnb):
    b, t, d = x_prompt.shape
    sb, st, _ = x_sample.shape
    depth = mod_w.shape[0]
    n_sub = norm_pre.shape[1]
    wts = dict(norm_pre=norm_pre.reshape(depth, n_sub, 1, d),
               norm_post=norm_post.reshape(depth, n_sub, 1, d), ffn_w_in=ffn_w_in,
               ffn_w_out=ffn_w_out, ffn_w_out_bf16={},
               conv_w_out=conv_w_out.astype(BF16)[:, None],
               rw_wo=rw_wo.astype(BF16)[:, None],
               conv_w_in=conv_w_in, conv_w=conv_w, rw_mix=rw_mix,
               rw_w0=rw_w0, rw_w1=rw_w1, rw_w2=rw_w2, rw_a0=rw_a0, rw_a1=rw_a1, rw_a2=rw_a2,
               rw_g1=rw_g1, rw_g2=rw_g2, rw_kk=rw_kk, rw_ka=rw_ka, rw_rk=rw_rk, rw_wr=rw_wr,
               rw_wk=rw_wk, rw_wv=rw_wv, rw_lnw=rw_lnw, rw_lnb=rw_lnb)

    n_c = b + sb
    pad = (-n_c) % 8
    c_all = jnp.concatenate([c_prompt, c_sample, jnp.zeros((pad, d), F32)], axis=0)
    mod_all = _mod_call(c_all, mod_w, mod_b)
    mod_p = (mod_all[:, :b].reshape(depth, b, N_MOD, d).transpose(0, 2, 1, 3)
             .reshape(depth, N_MOD, b, 1, d))
    mod_s = mod_all[:, b:n_c].reshape(depth, sb, N_MOD, d).transpose(0, 2, 1, 3)

    y_p, conv_p, shift_p, wkv_p = _trunk(
        x_prompt, mod_p, None, None, None, wts, time_major=False,
        tile=min(t, 1024), out_tile=min(t, 512), prep_tile=min(t, 256), chunk=min(t, 64),
        scan_seqs=math.gcd(b, 2))
    y_s, conv_s, shift_s, wkv_s = _trunk(
        x_sample.transpose(1, 0, 2), mod_s, state_conv, state_shift, state_wkv, wts,
        time_major=True, tile=sb, out_tile=min(sb, 32), prep_tile=min(sb, 32), chunk=st,
        scan_seqs=math.gcd(sb, 2))
    return (y_p, y_s.transpose(1, 0, 2), conv_p, shift_p, wkv_p, conv_s, shift_s, wkv_s)
```

```python
import functools
import math
from typing import NamedTuple

import jax
import jax.numpy as jnp
from jax import lax
from jax.experimental import pallas as pl
from jax.experimental.pallas import tpu as pltpu

F32 = jnp.float32
BF16 = jnp.bfloat16

RMS_EPS = 1e-6
GN_EPS = 64e-5
NORM_EPS = 1e-12
HALF_STEP = 0.5
HEAD_SIZE = 64
N_MOD = 9

MXU_WIDTH_V7X = 256
GROUP_LANES = MXU_WIDTH_V7X
HEADS_PER_GROUP = GROUP_LANES // HEAD_SIZE
VMEM_LIMIT_V7X = 60 * 2**20
ROW_PIECE = 256

NN = (((1,), (0,)), ((), ()))
NT = (((1,), (1,)), ((), ()))
TN = (((0,), (0,)), ((), ()))


def _mm(a, b, dims=NN):
    return lax.dot_general(a.astype(BF16), b.astype(BF16), dims, preferred_element_type=F32)


def _sigmoid(x):
    return 1.0 / (1.0 + jnp.exp(-x))


def _params(n_axes):
    return pltpu.CompilerParams(dimension_semantics=("arbitrary",) * n_axes,
                                vmem_limit_bytes=VMEM_LIMIT_V7X)


def _rms(x, g):
    return x * lax.rsqrt(jnp.mean(x * x, axis=-1, keepdims=True) + RMS_EPS) * g


def _modulated_pre(x, m_ref, g):
    return _rms(x, g) * (1.0 + m_ref[1]) + m_ref[0]


def _gated_post(x, y, m_ref, g, res_w):
    return x + (res_w * m_ref[2]) * _rms(y, g)


def _pieces(lead, rows):
    if lead == 1:
        n = min(ROW_PIECE, rows)
        return [(slice(0, 1), slice(r0, r0 + n)) for r0 in range(0, rows, n)]
    n = min(lead, max(1, ROW_PIECE // rows))
    return [(slice(a0, a0 + n), slice(0, rows)) for a0 in range(0, lead, n)]


def _piece_rows(x_ref, ls, rs):
    shp = x_ref[ls, rs, :].shape
    return shp, shp[0] * shp[1]


def _shift_time(u, k, fill, time_major):
    lead, rows, n = u.shape
    if time_major:
        return jnp.concatenate([fill, u[:lead - k]], axis=0)
    out = pltpu.roll(u.reshape(rows, n), k, 0)
    row = lax.broadcasted_iota(jnp.int32, (rows, n), 0)
    for t in range(k):
        out = jnp.where(row == t, fill[t:t + 1, :], out)
    return out.reshape(1, rows, n)


class _Layout:
    def __init__(self, shape, time_major, rows_per_tile):
        self.time_major = time_major
        self.shape = shape
        lead, rows, d = shape
        if time_major:
            self.block = (lead, rows_per_tile, d)
            self.tiles_per_seq = 1
            self.n_tiles = rows // rows_per_tile
        else:
            self.block = (1, rows_per_tile, d)
            self.tiles_per_seq = rows // rows_per_tile
            self.n_tiles = lead * self.tiles_per_seq
        self.tile_rows = self.block[0] * self.block[1]

    def xmap(self, i, *_):
        if self.time_major:
            return (0, i, 0)
        return (i // self.tiles_per_seq, i % self.tiles_per_seq, 0)

    def x_spec(self, **kw):
        return pl.BlockSpec(self.block, self.xmap, **kw)

    def mod_spec(self, mod, l, sub):
        d = self.shape[2]
        if self.time_major:
            return pl.BlockSpec((None, 3, self.block[1], d), lambda i, *_: (l, sub, i, 0))
        tps = self.tiles_per_seq
        return pl.BlockSpec((None, 3, None, 1, d), lambda i, *_: (l, sub, i // tps, 0, 0))


def _gain_spec(gains, l, sub):
    return pl.BlockSpec((None, None, 1, gains.shape[3]), lambda i, *_: (l, sub, 0, 0))


def _mod_kernel(c_ref, w_ref, b_ref, o_ref):
    c = c_ref[...]
    o_ref[...] = _mm(c * _sigmoid(c), w_ref[...]) + b_ref[...]


def _mod_call(c_all, mod_w, mod_b):
    depth, d, n = mod_w.shape
    nbp = c_all.shape[0]
    tn = math.gcd(n, 1024)
    return pl.pallas_call(
        _mod_kernel,
        grid=(depth, n // tn),
        in_specs=[
            pl.BlockSpec((nbp, d), lambda l, j: (0, 0)),
            pl.BlockSpec((None, d, tn), lambda l, j: (l, 0, j)),
            pl.BlockSpec((None, 1, tn), lambda l, j: (l, 0, j)),
        ],
        out_specs=pl.BlockSpec((None, nbp, tn), lambda l, j: (l, 0, j)),
        out_shape=jax.ShapeDtypeStruct((depth, nbp, n), F32),
        compiler_params=_params(2),
    )(c_all, mod_w, mod_b.reshape(depth, 1, n))


def _out_proj_kernel(*refs, res_w, emit_next):
    if emit_next:
        x_ref, m_ref, gpost_ref, a_ref, w_ref, mn_ref, gn_ref, o_ref, h_ref = refs
    else:
        x_ref, m_ref, gpost_ref, a_ref, w_ref, o_ref = refs
    lead, rows, _ = x_ref.shape
    for ls, rs in _pieces(lead, rows):
        shp, n = _piece_rows(x_ref, ls, rs)
        a = a_ref[ls, rs, :]
        y = lax.dot_general(a.reshape(n, a.shape[2]), w_ref[...], NN,
                            preferred_element_type=F32).reshape(shp)
        o = _gated_post(x_ref[ls, rs, :], y, m_ref, gpost_ref[...], res_w)
        o_ref[ls, rs, :] = o
        if emit_next:
            h_ref[ls, rs, :] = _modulated_pre(o, mn_ref, gn_ref[...]).astype(BF16)


def _out_proj_call(x3, lay, mod, norm_post, act, w, l, widx, sub, res_w, nxt=None):
    k, d = w.shape[2], w.shape[3]
    in_specs = [
        lay.x_spec(),
        lay.mod_spec(mod, l, sub),
        _gain_spec(norm_post, l, sub),
        pl.BlockSpec((lay.block[0], lay.block[1], k), lay.xmap),
        pl.BlockSpec((None, None, k, d), lambda i: (widx[0], widx[1], 0, 0),
                     pipeline_mode=pl.Buffered(1)),
    ]
    args = [x3, mod, norm_post, act, w]
    out_specs = [lay.x_spec()]
    out_shape = [jax.ShapeDtypeStruct(x3.shape, F32)]
    if nxt is not None:
        norm_pre, ln, subn = nxt
        in_specs += [lay.mod_spec(mod, ln, subn), _gain_spec(norm_pre, ln, subn)]
        args += [mod, norm_pre]
        out_specs.append(lay.x_spec())
        out_shape.append(jax.ShapeDtypeStruct(x3.shape, BF16))
    out = pl.pallas_call(
        functools.partial(_out_proj_kernel, res_w=res_w, emit_next=nxt is not None),
        grid=(lay.n_tiles,),
        in_specs=in_specs,
        out_specs=out_specs,
        out_shape=out_shape,
        compiler_params=_params(1),
    )(*args)
    return (out[0], out[1]) if nxt is not None else (out[0], None)


def _pre_kernel(x_ref, m_ref, g_ref, h_ref):
    lead, rows, _ = x_ref.shape
    for ls, rs in _pieces(lead, rows):
        h_ref[ls, rs, :] = _modulated_pre(x_ref[ls, rs, :], m_ref, g_ref[...]).astype(BF16)


def _pre_call(x3, lay, mod, norm_pre, l, sub):
    return pl.pallas_call(
        _pre_kernel,
        grid=(lay.n_tiles,),
        in_specs=[lay.x_spec(), lay.mod_spec(mod, l, sub), _gain_spec(norm_pre, l, sub)],
        out_specs=lay.x_spec(),
        out_shape=jax.ShapeDtypeStruct(x3.shape, BF16),
        compiler_params=_params(1),
    )(x3, mod, norm_pre)


def _cast_weights_once(i, pairs):
    @pl.when(i == 0)
    def _():
        for src, dst in pairs:
            dst[...] = src[...].astype(BF16)


def _ffn_in_kernel(*refs, cast_w_out, n_row_tiles):
    if cast_w_out:
        h_ref, wg_ref, wu_ref, wo_ref, a_ref, wob_ref, wgb_ref, wub_ref = refs
        share = wo_ref.shape[0] // n_row_tiles
        rows_i = pl.ds(pl.multiple_of(pl.program_id(1) * share, share), share)
        wob_ref[rows_i, :] = wo_ref[rows_i, :].astype(BF16)
    else:
        h_ref, wg_ref, wu_ref, a_ref, wgb_ref, wub_ref = refs
    _cast_weights_once(pl.program_id(1), [(wg_ref, wgb_ref), (wu_ref, wub_ref)])
    lead, rows, d = h_ref.shape
    for ls, rs in _pieces(lead, rows):
        shp, n = _piece_rows(h_ref, ls, rs)
        h = h_ref[ls, rs, :].reshape(n, d)
        gt = lax.dot_general(h, wgb_ref[...], NN, preferred_element_type=F32)
        up = lax.dot_general(h, wub_ref[...], NN, preferred_element_type=F32)
        a_ref[ls, rs, :] = (gt * _sigmoid(gt) * up).astype(BF16).reshape(shp[0], shp[1], -1)


def _ffn_in_call(h3, lay, w_in, l, s, tf, w_out=None):
    lead, rows, d = h3.shape
    f = w_in.shape[3] // 2
    nj = f // tf
    in_specs = [
        pl.BlockSpec(lay.block, lambda j, i: lay.xmap(i)),
        pl.BlockSpec((None, None, d, tf), lambda j, i: (l, s, 0, j)),
        pl.BlockSpec((None, None, d, tf), lambda j, i: (l, s, 0, nj + j)),
    ]
    args = [h3, w_in, w_in]
    out_specs = [pl.BlockSpec((lay.block[0], lay.block[1], tf),
                              lambda j, i: lay.xmap(i)[:2] + (j,))]
    out_shape = [jax.ShapeDtypeStruct((lead, rows, f), BF16)]
    if w_out is not None:
        in_specs.append(pl.BlockSpec((None, None, tf, d), lambda j, i: (l, s, j, 0)))
        args.append(w_out)
        out_specs.append(pl.BlockSpec((tf, d), lambda j, i: (j, 0)))
        out_shape.append(jax.ShapeDtypeStruct((f, d), BF16))
    out = pl.pallas_call(
        functools.partial(_ffn_in_kernel, cast_w_out=w_out is not None,
                          n_row_tiles=lay.n_tiles),
        grid=(nj, lay.n_tiles),
        in_specs=in_specs,
        out_specs=out_specs,
        out_shape=out_shape,
        scratch_shapes=[pltpu.VMEM((d, tf), BF16), pltpu.VMEM((d, tf), BF16)],
        compiler_params=_params(2),
    )(*args)
    return (out[0], out[1]) if w_out is not None else (out[0], None)


def _conv_kernel(*refs, time_major, tiles_per_seq):
    if time_major:
        (h_ref, wb_ref, wc_ref, wx_ref, cw_ref, st_ref,
         o_ref, so_ref, wbb_ref, wcb_ref, wxb_ref) = refs
    else:
        (h_ref, wb_ref, wc_ref, wx_ref, cw_ref,
         o_ref, so_ref, wbb_ref, wcb_ref, wxb_ref, carry_ref) = refs
    i = pl.program_id(1)
    _cast_weights_once(i, [(wb_ref, wbb_ref), (wc_ref, wcb_ref), (wx_ref, wxb_ref)])
    lead, rows, d = h_ref.shape
    tm = lead * rows

    h = h_ref[...].reshape(tm, d)

    def proj(w_ref):
        return lax.dot_general(h, w_ref[...], NN, preferred_element_type=F32)

    bg = proj(wbb_ref)
    u2 = proj(wcb_ref) * proj(wxb_ref)
    tn = u2.shape[1]
    u = u2.reshape(lead, rows, tn)
    if time_major:
        st = st_ref[...]
        fill1, fill2 = st[1:2], st
        so_ref[...] = u[lead - 2:lead]
    else:
        @pl.when(i % tiles_per_seq == 0)
        def _():
            carry_ref[...] = jnp.zeros((8, tn), F32)
        prev = carry_ref[...]
        fill1, fill2 = prev[7:8, :], prev[6:8, :]
        carry_ref[...] = u2[tm - 8:tm, :]
        so_ref[...] = u[:, rows - 2:rows, :]
    cw = cw_ref[...]
    z = (cw[0:1, :] * _shift_time(u, 2, fill2, time_major)
         + cw[1:2, :] * _shift_time(u, 1, fill1, time_major) + cw[2:3, :] * u)
    o_ref[...] = (bg.reshape(lead, rows, tn) * z).astype(BF16)


def _conv_call(h3, lay, w_in, cw, state, jl, tn):
    lead_n, rows_n, d = h3.shape
    dc = cw.shape[2]
    nj = dc // tn
    in_specs = [
        pl.BlockSpec(lay.block, lambda j, i: lay.xmap(i)),
        pl.BlockSpec((None, d, tn), lambda j, i: (jl, 0, j)),
        pl.BlockSpec((None, d, tn), lambda j, i: (jl, 0, nj + j)),
        pl.BlockSpec((None, d, tn), lambda j, i: (jl, 0, 2 * nj + j)),
        pl.BlockSpec((None, cw.shape[1], tn), lambda j, i: (jl, 0, j)),
    ]
    args = [h3, w_in, w_in, w_in, cw]
    scratch = [pltpu.VMEM((d, tn), BF16)] * 3
    if lay.time_major:
        nb = lay.block[1]
        in_specs.append(pl.BlockSpec((2, nb, tn), lambda j, i: (0, i, j)))
        args.append(state)
        so_spec = pl.BlockSpec((2, nb, tn), lambda j, i: (0, i, j))
        so_shape = jax.ShapeDtypeStruct((2, h3.shape[1], dc), F32)
    else:
        scratch.append(pltpu.VMEM((8, tn), F32))
        so_spec = pl.BlockSpec((1, 2, tn), lambda j, i: (i, 0, j))
        so_shape = jax.ShapeDtypeStruct((lay.n_tiles, 2, dc), F32)
    return pl.pallas_call(
        functools.partial(_conv_kernel, time_major=lay.time_major,
                          tiles_per_seq=lay.tiles_per_seq),
        grid=(nj, lay.n_tiles),
        in_specs=in_specs,
        out_specs=[pl.BlockSpec((lay.block[0], lay.block[1], tn),
                                lambda j, i: lay.xmap(i)[:2] + (j,)), so_spec],
        out_shape=[jax.ShapeDtypeStruct((lead_n, rows_n, dc), BF16), so_shape],
        scratch_shapes=scratch,
        compiler_params=_params(2),
    )(*args)


def _rw_prep_kernel(*refs, time_major, tiles_per_seq):
    if time_major:
        (x_ref, m_ref, gpre_ref, mix_ref, w1_ref, a1_ref, g1_ref, st_ref,
         xr_ref, xk_ref, xv_ref, tw_ref, ta_ref, tg_ref, so_ref) = refs
    else:
        (x_ref, m_ref, gpre_ref, mix_ref, w1_ref, a1_ref, g1_ref,
         xr_ref, xk_ref, xv_ref, tw_ref, ta_ref, tg_ref, so_ref, carry_ref) = refs
    i = pl.program_id(0)
    lead, rows, d = x_ref.shape
    n = lead * rows
    h = _modulated_pre(x_ref[...], m_ref, gpre_ref[...])
    if time_major:
        fill = st_ref[...][None]
        so_ref[...] = h[lead - 1]
    else:
        @pl.when(i % tiles_per_seq == 0)
        def _():
            carry_ref[...] = jnp.zeros((8, d), F32)
        fill = carry_ref[7:8, :]
        carry_ref[...] = h[0, rows - 8:rows, :]
        so_ref[...] = h[:, rows - 1:rows, :]
    xx = _shift_time(h, 1, fill, time_major) - h
    mix = mix_ref[...]

    def mixed(k):
        return h + xx * mix[k:k + 1, :]

    def low_rank(k, w_ref):
        return _mm(mixed(k).reshape(n, d), w_ref[...]).reshape(lead, rows, w_ref.shape[1])

    xr_ref[...] = mixed(0).astype(BF16)
    tw_ref[...] = jnp.tanh(low_rank(1, w1_ref))
    xk_ref[...] = mixed(2).astype(BF16)
    xv_ref[...] = mixed(3).astype(BF16)
    ta_ref[...] = low_rank(4, a1_ref)
    tg_ref[...] = _sigmoid(low_rank(5, g1_ref))


def _rw_prep_call(x3, lay, mod, norm_pre, mix, w1, a1, g1, state, l, jl, sub):
    lead, rows, d = x3.shape
    dl, dg = w1.shape[2], g1.shape[2]
    in_specs = [
        lay.x_spec(),
        lay.mod_spec(mod, l, sub),
        _gain_spec(norm_pre, l, sub),
        pl.BlockSpec((None, mix.shape[1], d), lambda i: (jl, 0, 0)),
        pl.BlockSpec((None, d, dl), lambda i: (jl, 0, 0)),
        pl.BlockSpec((None, d, dl), lambda i: (jl, 0, 0)),
        pl.BlockSpec((None, d, dg), lambda i: (jl, 0, 0)),
    ]
    args = [x3, mod, norm_pre, mix, w1, a1, g1]
    scratch = []
    blk = lay.block
    if lay.time_major:
        in_specs.append(pl.BlockSpec((blk[1], d), lambda i: (i, 0)))
        args.append(state)
        so_spec = pl.BlockSpec((blk[1], d), lambda i: (i, 0))
        so_shape = jax.ShapeDtypeStruct((rows, d), F32)
    else:
        scratch.append(pltpu.VMEM((8, d), F32))
        tps = lay.tiles_per_seq
        so_spec = pl.BlockSpec((1, 1, d), lambda i: (i // tps, 0, 0))
        so_shape = jax.ShapeDtypeStruct((lead, 1, d), F32)

    def ospec(width):
        return pl.BlockSpec((blk[0], blk[1], width), lay.xmap)

    return pl.pallas_call(
        functools.partial(_rw_prep_kernel, time_major=lay.time_major,
                          tiles_per_seq=lay.tiles_per_seq),
        grid=(lay.n_tiles,),
        in_specs=in_specs,
        out_specs=[ospec(d)] * 3 + [ospec(dl), ospec(dl), ospec(dg), so_spec],
        out_shape=[jax.ShapeDtypeStruct((lead, rows, d), BF16)] * 3
        + [jax.ShapeDtypeStruct((lead, rows, dl), F32)] * 2
        + [jax.ShapeDtypeStruct((lead, rows, dg), F32), so_shape],
        scratch_shapes=scratch,
        compiler_params=_params(1),
    )(*args)


def _rw_proj_kernel(xr_ref, xk_ref, xv_ref, tw_ref, ta_ref, tg_ref, wr_ref, wk_ref, wv_ref,
                    w2_ref, a2_ref, g2_ref, w0_ref, a0_ref,
                    r_ref, k_ref, v_ref, w_ref, a_ref, g_ref, wrb_ref, wkb_ref, wvb_ref,
                    *, cols_outer):
    pairs = [(wr_ref, wrb_ref), (wk_ref, wkb_ref), (wv_ref, wvb_ref)]
    if cols_outer:
        _cast_weights_once(pl.program_id(1), pairs)
    else:
        for src, dst in pairs:
            dst[...] = src[...].astype(BF16)

    def proj(x_ref, wb_ref):
        return lax.dot_general(x_ref[...], wb_ref[...], NN, preferred_element_type=F32)

    r_ref[...] = proj(xr_ref, wrb_ref)
    k_ref[...] = proj(xk_ref, wkb_ref)
    v_ref[...] = proj(xv_ref, wvb_ref)
    w_ref[...] = w0_ref[...] + _mm(tw_ref[...], w2_ref[...])
    a_ref[...] = _sigmoid(a0_ref[...] + _mm(ta_ref[...], a2_ref[...]))
    g_ref[...] = _mm(tg_ref[...], g2_ref[...])


def _rw_proj_call(xr, xk, xv, tw, ta, tg, wr, wk, wv, w2, a2, g2, w0, a0, jl, tm, tn):
    m, d = xr.shape
    dl, dg = tw.shape[1], tg.shape[1]
    nr = w0.shape[0]
    cols_outer = m // tm > 2
    grid = (d // tn, m // tm) if cols_outer else (m // tm, d // tn)

    def ij(f):
        return (lambda j, i: f(i, j)) if cols_outer else f

    xspec = pl.BlockSpec((tm, d), ij(lambda i, j: (i, 0)))
    wspec = pl.BlockSpec((None, d, tn), ij(lambda i, j: (jl, 0, j)))
    vspec = pl.BlockSpec((None, 1, tn), ij(lambda i, j: (jl, 0, j)))
    ospec = pl.BlockSpec((tm, tn), ij(lambda i, j: (i, j)))
    return pl.pallas_call(
        functools.partial(_rw_proj_kernel, cols_outer=cols_outer),
        grid=grid,
        in_specs=[xspec, xspec, xspec,
                  pl.BlockSpec((tm, dl), ij(lambda i, j: (i, 0))),
                  pl.BlockSpec((tm, dl), ij(lambda i, j: (i, 0))),
                  pl.BlockSpec((tm, dg), ij(lambda i, j: (i, 0))),
                  wspec, wspec, wspec,
                  pl.BlockSpec((None, dl, tn), ij(lambda i, j: (jl, 0, j))),
                  pl.BlockSpec((None, dl, tn), ij(lambda i, j: (jl, 0, j))),
                  pl.BlockSpec((None, dg, tn), ij(lambda i, j: (jl, 0, j))),
                  vspec, vspec],
        out_specs=[ospec] * 6,
        out_shape=[jax.ShapeDtypeStruct((m, d), F32)] * 6,
        scratch_shapes=[pltpu.VMEM((d, tn), BF16)] * 3,
        compiler_params=_params(2),
    )(xr, xk, xv, tw, ta, tg, wr, wk, wv, w2, a2, g2,
      w0.reshape(nr, 1, d), a0.reshape(nr, 1, d))


def _iota2(shape, axis):
    return lax.broadcasted_iota(jnp.int32, shape, axis)


def _same_block(shape, row_block, lane_block):
    r = lax.shift_right_logical(_iota2(shape, 0), int(math.log2(row_block)))
    c = lax.shift_right_logical(_iota2(shape, 1), int(math.log2(lane_block)))
    return r == c


def _bd_rows(x, mask):
    return jnp.where(mask, jnp.concatenate([x] * HEADS_PER_GROUP, axis=0), 0.0)


def _split_mm(x, ones):
    hi = x.astype(BF16)
    lo = (x - hi.astype(F32)).astype(BF16)
    return (lax.dot_general(hi, ones, NN, preferred_element_type=F32)
            + lax.dot_general(lo, ones, NN, preferred_element_type=F32))


def _each(fn, *lists):
    return [fn(*xs) for xs in zip(*lists)]


def _wkv_chunk(r, k, v, a_, b_, lw, cum, s_bd, c):
    hs = HEAD_SIZE
    tlanes = HEADS_PER_GROUP * c
    m_ch = _same_block((tlanes, GROUP_LANES), c, hs)
    m_tt = _same_block((tlanes, tlanes), c, c)
    m_ss = _same_block((GROUP_LANES, GROUP_LANES), hs, hs)
    t_row = _iota2((c, tlanes), 0)
    s_lane = jnp.bitwise_and(_iota2((c, tlanes), 1), c - 1)
    strict = s_lane < t_row
    incl = s_lane <= t_row

    def bd_ch(x):
        return _bd_rows(x, m_ch)

    def bd_tt(x):
        return _bd_rows(x, m_tt)

    def apply(a, x):
        return _mm(a, bd_ch(x))

    cl = _each(lambda x: x[c - 1:c, :], cum)
    at = _each(lambda a, x, l: a * jnp.exp(x - l), a_, cum, lw)
    rt = _each(lambda a, x: a * jnp.exp(x), r, cum)
    bt = _each(lambda a, x: a * jnp.exp(-x), b_, cum)
    kt = _each(lambda a, x: a * jnp.exp(-x), k, cum)
    bh = _each(lambda a, x, xl: a * jnp.exp(xl - x), b_, cum, cl)
    kh = _each(lambda a, x, xl: a * jnp.exp(xl - x), k, cum, cl)

    lhs = _each(lambda a, b: jnp.concatenate([a, b], axis=0), at, rt)
    ob = _each(lambda a, x: _mm(a, bd_ch(x), NT), lhs, bt)
    ok = _each(lambda a, x: _mm(a, bd_ch(x), NT), lhs, kt)
    a_ab = _each(lambda x: jnp.where(strict, x[:c], 0.0), ob)
    a_rb = _each(lambda x: jnp.where(incl, x[c:], 0.0), ob)
    a_ak = _each(lambda x: jnp.where(strict, x[:c], 0.0), ok)
    a_rk = _each(lambda x: jnp.where(incl, x[c:], 0.0), ok)
    akv = _each(lambda a, b, x: apply(jnp.concatenate([a, b], axis=0), x), a_ak, a_rk, v)
    av = _each(lambda x: x[:c], akv)
    y_kv = _each(lambda x: x[c:], akv)

    eye = jnp.where(s_lane == t_row, 1.0, 0.0)
    tinv = _each(lambda x: eye + x, a_ab)
    npow = _each(lambda x: _mm(x, bd_tt(x)), a_ab)
    for _ in range(int(math.log2(c)) - 2):
        both = _each(lambda t, x: _mm(jnp.concatenate([t, x], axis=0), bd_tt(x)), tinv, npow)
        tinv = _each(lambda t, z: t + z[:c], tinv, both)
        npow = _each(lambda z: z[c:], both)
    tinv = _each(lambda t, x: t + _mm(t, bd_tt(x)), tinv, npow)

    wt = _each(apply, tinv, at)
    w = _each(apply, tinv, av)
    qt = _each(lambda x, a, y: x + apply(a, y), rt, a_rb, wt)
    yi = _each(lambda a, x, y: apply(a, x) + y, a_rb, w, y_kv)
    g_off = _each(lambda x, y: jnp.where(m_ss, _mm(x, y, TN), 0.0), wt, bh)
    h_t = _each(lambda x, y, p, q: jnp.where(
        m_ss, _mm(jnp.concatenate([x, p], axis=0), jnp.concatenate([y, q], axis=0), TN), 0.0),
                w, bh, v, kh)

    y = _each(lambda q, s, x: _mm(q, s, NT) + x, qt, s_bd, yi)
    s_new = _each(lambda s, xl, g, h: s * jnp.exp(xl) + _mm(s, g) + h, s_bd, cl, g_off, h_t)
    return y, s_new


def _scan_kernel(*refs, c, d, nq, has_state, lane_packed):
    if has_state:
        (r_ref, k_ref, v_ref, w_ref, a_ref, g_ref, p_ref, s0_ref, y_ref, so_ref, sbd_ref) = refs
    else:
        (r_ref, k_ref, v_ref, w_ref, a_ref, g_ref, p_ref, y_ref, so_ref, sbd_ref) = refs
    ci = pl.program_id(1)
    hs = HEAD_SIZE
    ng = d // GROUP_LANES
    chains = [(q, g) for q in range(nq) for g in range(ng)]
    n_ch = len(chains)
    m_ss = _same_block((GROUP_LANES, GROUP_LANES), hs, hs)
    ones_bd = jnp.where(m_ss, 1.0, 0.0).astype(BF16)

    def lanes(g):
        return slice(g * GROUP_LANES, (g + 1) * GROUP_LANES)

    def rd(ref, q, g):
        if lane_packed:
            return ref[:, q * d + g * GROUP_LANES:q * d + (g + 1) * GROUP_LANES]
        return ref[q, :, lanes(g)]

    def par(row, g):
        return p_ref[row:row + 1, lanes(g)]

    @pl.when(ci == 0)
    def _():
        for i, (q, g) in enumerate(chains):
            if has_state:
                heads = [s0_ref[q, HEADS_PER_GROUP * g + h] for h in range(HEADS_PER_GROUP)]
                sbd_ref[i] = _bd_rows(jnp.concatenate(heads, axis=1), m_ss)
            else:
                sbd_ref[i] = jnp.zeros((GROUP_LANES, GROUP_LANES), F32)

    def log_decay(wq):
        z = -wq
        softplus = jnp.maximum(z, 0.0) + jnp.log(1.0 + jnp.exp(-jnp.abs(z)))
        lw = -jnp.exp(-softplus - 0.5)
        row = _iota2(lw.shape, 0)
        cum = lw
        step = 1
        while step < c:
            cum = cum + jnp.where(row >= step, pltpu.roll(cum, step, 0), 0.0)
            step *= 2
        return lw, cum

    if lane_packed:
        lw_all, cum_all = log_decay(w_ref[...])
        lw = [lw_all[:, q * d + g * GROUP_LANES:q * d + (g + 1) * GROUP_LANES] for q, g in chains]
        cum = [cum_all[:, q * d + g * GROUP_LANES:q * d + (g + 1) * GROUP_LANES] for q, g in chains]
    else:
        per_q = [log_decay(w_ref[q]) for q in range(nq)]
        lw = [per_q[q][0][:, lanes(g)] for q, g in chains]
        cum = [per_q[q][1][:, lanes(g)] for q, g in chains]

    def seg_sum(xs, split=True):
        x = jnp.concatenate(xs, axis=0)
        y = _split_mm(x, ones_bd) if split else _mm(x, ones_bd)
        return [y[i * c:(i + 1) * c] for i in range(len(xs))]

    r = [rd(r_ref, q, g) for q, g in chains]
    k = [rd(k_ref, q, g) for q, g in chains]
    v = [rd(v_ref, q, g) for q, g in chains]
    a_sig = [rd(a_ref, q, g) for q, g in chains]
    kk = [x * par(0, g) for x, (q, g) in zip(k, chains)]
    k = [x * (1.0 + (a - 1.0) * par(1, g)) for x, a, (q, g) in zip(k, a_sig, chains)]
    kk = _each(lambda x, n: x / jnp.maximum(jnp.sqrt(n), NORM_EPS), kk,
               seg_sum(_each(lambda x: x * x, kk)))
    bonus = _each(lambda s, z: s * z,
                  seg_sum([x * y * par(2, g) for x, y, (q, g) in zip(r, k, chains)], split=False),
                  v)
    y, s_new = _wkv_chunk(r, k, v, _each(lambda x: -x, kk), _each(lambda x, a: x * a, kk, a_sig),
                          lw, cum, [sbd_ref[i] for i in range(n_ch)], c)
    for i in range(n_ch):
        sbd_ref[i] = s_new[i]

    mu = _each(lambda x: x * (1.0 / hs), seg_sum(y))
    dy = _each(lambda x, m: x - m, y, mu)
    var = _each(lambda x: x * (1.0 / hs), seg_sum(_each(lambda x: x * x, dy), split=False))
    for i, (q, g) in enumerate(chains):
        yn = dy[i] * lax.rsqrt(var[i] + GN_EPS) * par(3, g) + par(4, g)
        out = ((yn + bonus[i]) * rd(g_ref, q, g)).astype(BF16)
        if lane_packed:
            y_ref[:, q * d + g * GROUP_LANES:q * d + (g + 1) * GROUP_LANES] = out
        else:
            y_ref[q, :, lanes(g)] = out

    @pl.when(ci == pl.num_programs(1) - 1)
    def _():
        for i, (q, g) in enumerate(chains):
            s = sbd_ref[i]
            s = s[0:hs] + s[hs:2 * hs] + s[2 * hs:3 * hs] + s[3 * hs:4 * hs]
            for h in range(HEADS_PER_GROUP):
                so_ref[q, HEADS_PER_GROUP * g + h] = s[:, h * hs:(h + 1) * hs]


def _scan_call(r, k, v, w, a, g, p, s0, nb, t, c, nq, time_major):
    m, d = r.shape
    nc = t // c
    ng = d // GROUP_LANES
    nh = d // HEAD_SIZE
    has_state = s0 is not None
    if time_major:
        r, k, v, w, a, g = (z.reshape(t, nb * d) for z in (r, k, v, w, a, g))
        xspec = pl.BlockSpec((c, nq * d), lambda b, ci: (ci, b))
    else:
        r, k, v, w, a, g = (z.reshape(nb, t, d) for z in (r, k, v, w, a, g))
        xspec = pl.BlockSpec((nq, c, d), lambda b, ci: (b, ci, 0))
    sspec = pl.BlockSpec((nq, nh, HEAD_SIZE, HEAD_SIZE), lambda b, ci: (b, 0, 0, 0))
    in_specs = [xspec] * 6 + [pl.BlockSpec(p.shape, lambda b, ci: (0, 0))]
    args = [r, k, v, w, a, g, p]
    if has_state:
        in_specs.append(sspec)
        args.append(s0)
    y, s_fin = pl.pallas_call(
        functools.partial(_scan_kernel, c=c, d=d, nq=nq, has_state=has_state,
                          lane_packed=time_major),
        grid=(nb // nq, nc),
        in_specs=in_specs,
        out_specs=[xspec, sspec],
        out_shape=[jax.ShapeDtypeStruct(r.shape, BF16),
                   jax.ShapeDtypeStruct((nb, nh, HEAD_SIZE, HEAD_SIZE), F32)],
        scratch_shapes=[pltpu.VMEM((nq * ng, GROUP_LANES, GROUP_LANES), F32)],
        compiler_params=_params(2),
    )(*args)
    return y.reshape(m, d), s_fin


class _Tiles(NamedTuple):
    ffn: int
    conv: int
    out: int
    prep: int
    chunk: int
    scan_seqs: int


def _tiles(time_major, nb, t):
    if time_major:
        return _Tiles(ffn=nb, conv=nb, out=min(nb, 32), prep=min(nb, 32), chunk=t,
                      scan_seqs=math.gcd(nb, 2))
    return _Tiles(ffn=min(t, 2048), conv=min(t, 1024), out=min(t, 512), prep=min(t, 256),
                  chunk=min(t, 64), scan_seqs=math.gcd(nb, 2))


def _trunk(x3, mod, conv_state, shift_state, wkv_state, wts, *, time_major):
    d = x3.shape[2]
    nb, t = (x3.shape[1], x3.shape[0]) if time_major else (x3.shape[0], x3.shape[1])
    depth = mod.shape[0]
    tiles = _tiles(time_major, nb, t)
    chunk, scan_seqs = tiles.chunk, tiles.scan_seqs
    ffn_lay = _Layout(x3.shape, time_major, tiles.ffn)
    lay = _Layout(x3.shape, time_major, tiles.conv)
    out_lay = _Layout(x3.shape, time_major, tiles.out)
    prep_lay = _Layout(x3.shape, time_major, tiles.prep)
    m = nb * t
    new_conv, new_shift, new_wkv = [], [], []

    subs = [(l, sub) for l in range(depth) for sub in range(3)]

    def takes_h(l, sub):
        return not (sub == 1 and l % 2 == 1)

    def nxt_of(l, sub):
        i = subs.index((l, sub)) + 1
        if i < len(subs) and takes_h(*subs[i]):
            return (wts["norm_pre"],) + subs[i]
        return None

    def ffn(x3, h3, l, s, sub):
        w_out = wts["ffn_w_out_bf16"].get((l, s))
        act, cast = _ffn_in_call(h3, ffn_lay, wts["ffn_w_in"], l, s, 512,
                                 wts["ffn_w_out"] if w_out is None else None)
        if w_out is None:
            w_out = wts["ffn_w_out_bf16"][(l, s)] = cast
        return _out_proj_call(x3, out_lay, mod, wts["norm_post"], act, w_out[None, None],
                              l, (0, 0), sub, HALF_STEP, nxt_of(l, sub))

    h3 = _pre_call(x3, out_lay, mod, wts["norm_pre"], 0, 0)
    for l in range(depth):
        jl = l // 2
        x3, h3 = ffn(x3, h3, l, 0, 0)
        if l % 2 == 0:
            st = conv_state[jl].transpose(1, 0, 2) if time_major else None
            bz, cs = _conv_call(h3, lay, wts["conv_w_in"], wts["conv_w"], st, jl, 256)
            x3, h3 = _out_proj_call(x3, out_lay, mod, wts["norm_post"], bz, wts["conv_w_out"],
                                    l, (jl, 0), 1, 1.0, nxt_of(l, 1))
            if time_major:
                new_conv.append(cs.transpose(1, 0, 2))
            else:
                tps = lay.tiles_per_seq
                new_conv.append(cs[tps - 1::tps])
        else:
            st = shift_state[jl] if time_major else None
            xr, xk, xv, tw, ta, tg, ss = _rw_prep_call(
                x3, prep_lay, mod, wts["norm_pre"], wts["rw_mix"], wts["rw_w1"], wts["rw_a1"],
                wts["rw_g1"], st, l, jl, 1)
            flat = lambda z: z.reshape(m, z.shape[2])
            r, k, v, w, a, g = _rw_proj_call(
                flat(xr), flat(xk), flat(xv), flat(tw), flat(ta), flat(tg),
                wts["rw_wr"], wts["rw_wk"], wts["rw_wv"], wts["rw_w2"], wts["rw_a2"],
                wts["rw_g2"], wts["rw_w0"], wts["rw_a0"], jl,
                *((512, min(d, 512)) if m > 1024 else (m, min(d, 256))))
            p = jnp.concatenate([
                wts["rw_kk"][jl][None], wts["rw_ka"][jl][None], wts["rw_rk"][jl].reshape(1, d),
                wts["rw_lnw"][jl][None], wts["rw_lnb"][jl][None],
                jnp.zeros((3, d), F32)], axis=0)
            s0 = wkv_state[jl] if time_major else None
            yg, s_fin = _scan_call(r, k, v, w, a, g, p, s0, nb, t, chunk, scan_seqs, time_major)
            x3, h3 = _out_proj_call(x3, out_lay, mod, wts["norm_post"], yg.reshape(x3.shape),
                                    wts["rw_wo"], l, (jl, 0), 1, 1.0, nxt_of(l, 1))
            new_shift.append(ss.reshape(nb, d))
            new_wkv.append(s_fin)
        x3, h3 = ffn(x3, h3, l, 1, 2)
    def stack(xs):
        return xs[0][None] if len(xs) == 1 else jnp.stack(xs)

    return x3, stack(new_conv), stack(new_shift), stack(new_wkv)


def kernel(x_prompt, x_sample, state_conv, state_shift, state_wkv, c_prompt, c_sample, mod_w, mod_b, norm_pre, norm_post, ffn_w_in, ffn_w_out, conv_w_in, conv_w, conv_w_out, rw_mix, rw_w0, rw_w1, rw_w2, rw_a0, rw_a1, rw_a2, rw_g1, rw_g2, rw_kk, rw_ka, rw_rk, rw_wr, rw_wk, rw_wv, rw_wo, rw_lnw, rw_lnb):
    b, t, d = x_prompt.shape
    sb, st, _ = x_sample.shape
    depth = mod_w.shape[0]
    n_sub = norm_pre.shape[1]
    wts = dict(norm_pre=norm_pre.reshape(depth, n_sub, 1, d),
               norm_post=norm_post.reshape(depth, n_sub, 1, d), ffn_w_in=ffn_w_in,
               ffn_w_out=ffn_w_out, ffn_w_out_bf16={},
               conv_w_out=conv_w_out.astype(BF16)[:, None],
               rw_wo=rw_wo.astype(BF16)[:, None],
               conv_w_in=conv_w_in, conv_w=conv_w, rw_mix=rw_mix,
               rw_w0=rw_w0, rw_w1=rw_w1, rw_w2=rw_w2, rw_a0=rw_a0, rw_a1=rw_a1, rw_a2=rw_a2,
               rw_g1=rw_g1, rw_g2=rw_g2, rw_kk=rw_kk, rw_ka=rw_ka, rw_rk=rw_rk, rw_wr=rw_wr,
               rw_wk=rw_wk, rw_wv=rw_wv, rw_lnw=rw_lnw, rw_lnb=rw_lnb)

    n_c = b + sb
    pad = (-n_c) % 8
    c_all = jnp.concatenate([c_prompt, c_sample, jnp.zeros((pad, d), F32)], axis=0)
    mod_all = _mod_call(c_all, mod_w, mod_b)
    mod_p = (mod_all[:, :b].reshape(depth, b, N_MOD, d).transpose(0, 2, 1, 3)
             .reshape(depth, N_MOD, b, 1, d))
    mod_s = mod_all[:, b:n_c].reshape(depth, sb, N_MOD, d).transpose(0, 2, 1, 3)

    y_p, conv_p, shift_p, wkv_p = _trunk(
        x_prompt, mod_p, None, None, None, wts, time_major=False)
    y_s, conv_s, shift_s, wkv_s = _trunk(
        x_sample.transpose(1, 0, 2), mod_s, state_conv, state_shift, state_wkv, wts,
        time_major=True)
    return (y_p, y_s.transpose(1, 0, 2), conv_p, shift_p, wkv_p, conv_s, shift_s, wkv_s)
```

```python
import functools
import math
from typing import NamedTuple

import jax
import jax.numpy as jnp
from jax import lax
from jax.experimental import pallas as pl
from jax.experimental.pallas import tpu as pltpu

F32 = jnp.float32
BF16 = jnp.bfloat16

RMS_EPS = 1e-6
GN_EPS = 64e-5
NORM_EPS = 1e-12
HALF_STEP = 0.5
HEAD_SIZE = 64
N_MOD = 9

MXU_WIDTH_V7X = 256
GROUP_LANES = MXU_WIDTH_V7X
HEADS_PER_GROUP = GROUP_LANES // HEAD_SIZE
VMEM_LIMIT_V7X = 60 * 2**20
ROW_PIECE = 256

NN = (((1,), (0,)), ((), ()))
NT = (((1,), (1,)), ((), ()))
TN = (((0,), (0,)), ((), ()))


def _mm(a, b, dims=NN):
    return lax.dot_general(a.astype(BF16), b.astype(BF16), dims, preferred_element_type=F32)


def _sigmoid(x):
    return 1.0 / (1.0 + jnp.exp(-x))


def _params(n_axes):
    return pltpu.CompilerParams(dimension_semantics=("arbitrary",) * n_axes,
                                vmem_limit_bytes=VMEM_LIMIT_V7X)


def _rms(x, g):
    return x * lax.rsqrt(jnp.mean(x * x, axis=-1, keepdims=True) + RMS_EPS) * g


def _modulated_pre(x, m_ref, g):
    return _rms(x, g) * (1.0 + m_ref[1]) + m_ref[0]


def _gated_post(x, y, m_ref, g, res_w):
    return x + (res_w * m_ref[2]) * _rms(y, g)


def _pieces(lead, rows):
    if lead == 1:
        n = min(ROW_PIECE, rows)
        return [(slice(0, 1), slice(r0, r0 + n)) for r0 in range(0, rows, n)]
    n = min(lead, max(1, ROW_PIECE // rows))
    return [(slice(a0, a0 + n), slice(0, rows)) for a0 in range(0, lead, n)]


def _piece_rows(x_ref, ls, rs):
    shp = x_ref[ls, rs, :].shape
    return shp, shp[0] * shp[1]


def _shift_time(u, k, fill, time_major):
    lead, rows, n = u.shape
    if time_major:
        return jnp.concatenate([fill, u[:lead - k]], axis=0)
    out = pltpu.roll(u.reshape(rows, n), k, 0)
    row = lax.broadcasted_iota(jnp.int32, (rows, n), 0)
    for t in range(k):
        out = jnp.where(row == t, fill[t:t + 1, :], out)
    return out.reshape(1, rows, n)


class _Layout:
    def __init__(self, shape, time_major, rows_per_tile):
        self.time_major = time_major
        self.shape = shape
        lead, rows, d = shape
        if time_major:
            self.block = (lead, rows_per_tile, d)
            self.tiles_per_seq = 1
            self.n_tiles = rows // rows_per_tile
        else:
            self.block = (1, rows_per_tile, d)
            self.tiles_per_seq = rows // rows_per_tile
            self.n_tiles = lead * self.tiles_per_seq
        self.tile_rows = self.block[0] * self.block[1]

    def xmap(self, i, *_):
        if self.time_major:
            return (0, i, 0)
        return (i // self.tiles_per_seq, i % self.tiles_per_seq, 0)

    def x_spec(self, **kw):
        return pl.BlockSpec(self.block, self.xmap, **kw)

    def mod_spec(self, mod, l, sub):
        d = self.shape[2]
        if self.time_major:
            return pl.BlockSpec((None, 3, self.block[1], d), lambda i, *_: (l, sub, i, 0))
        tps = self.tiles_per_seq
        return pl.BlockSpec((None, 3, None, 1, d), lambda i, *_: (l, sub, i // tps, 0, 0))


def _gain_spec(gains, l, sub):
    return pl.BlockSpec((None, None, 1, gains.shape[3]), lambda i, *_: (l, sub, 0, 0))


def _mod_kernel(c_ref, w_ref, b_ref, o_ref):
    c = c_ref[...]
    o_ref[...] = _mm(c * _sigmoid(c), w_ref[...]) + b_ref[...]


def _mod_call(c_all, mod_w, mod_b):
    depth, d, n = mod_w.shape
    nbp = c_all.shape[0]
    tn = math.gcd(n, 1024)
    return pl.pallas_call(
        _mod_kernel,
        grid=(depth, n // tn),
        in_specs=[
            pl.BlockSpec((nbp, d), lambda l, j: (0, 0)),
            pl.BlockSpec((None, d, tn), lambda l, j: (l, 0, j)),
            pl.BlockSpec((None, 1, tn), lambda l, j: (l, 0, j)),
        ],
        out_specs=pl.BlockSpec((None, nbp, tn), lambda l, j: (l, 0, j)),
        out_shape=jax.ShapeDtypeStruct((depth, nbp, n), F32),
        compiler_params=_params(2),
    )(c_all, mod_w, mod_b.reshape(depth, 1, n))


def _out_proj_kernel(*refs, res_w, emit_next):
    if emit_next:
        x_ref, m_ref, gpost_ref, a_ref, w_ref, mn_ref, gn_ref, o_ref, h_ref = refs
    else:
        x_ref, m_ref, gpost_ref, a_ref, w_ref, o_ref = refs
    lead, rows, _ = x_ref.shape
    for ls, rs in _pieces(lead, rows):
        shp, n = _piece_rows(x_ref, ls, rs)
        a = a_ref[ls, rs, :]
        y = lax.dot_general(a.reshape(n, a.shape[2]), w_ref[...], NN,
                            preferred_element_type=F32).reshape(shp)
        o = _gated_post(x_ref[ls, rs, :], y, m_ref, gpost_ref[...], res_w)
        o_ref[ls, rs, :] = o
        if emit_next:
            h_ref[ls, rs, :] = _modulated_pre(o, mn_ref, gn_ref[...]).astype(BF16)


def _out_proj_call(x3, lay, mod, norm_post, act, w, l, widx, sub, res_w, nxt=None):
    k, d = w.shape[2], w.shape[3]
    in_specs = [
        lay.x_spec(),
        lay.mod_spec(mod, l, sub),
        _gain_spec(norm_post, l, sub),
        pl.BlockSpec((lay.block[0], lay.block[1], k), lay.xmap),
        pl.BlockSpec((None, None, k, d), lambda i: (widx[0], widx[1], 0, 0),
                     pipeline_mode=pl.Buffered(1)),
    ]
    args = [x3, mod, norm_post, act, w]
    out_specs = [lay.x_spec()]
    out_shape = [jax.ShapeDtypeStruct(x3.shape, F32)]
    if nxt is not None:
        norm_pre, ln, subn = nxt
        in_specs += [lay.mod_spec(mod, ln, subn), _gain_spec(norm_pre, ln, subn)]
        args += [mod, norm_pre]
        out_specs.append(lay.x_spec())
        out_shape.append(jax.ShapeDtypeStruct(x3.shape, BF16))
    out = pl.pallas_call(
        functools.partial(_out_proj_kernel, res_w=res_w, emit_next=nxt is not None),
        grid=(lay.n_tiles,),
        in_specs=in_specs,
        out_specs=out_specs,
        out_shape=out_shape,
        compiler_params=_params(1),
    )(*args)
    return (out[0], out[1]) if nxt is not None else (out[0], None)


def _pre_kernel(x_ref, m_ref, g_ref, h_ref):
    lead, rows, _ = x_ref.shape
    for ls, rs in _pieces(lead, rows):
        h_ref[ls, rs, :] = _modulated_pre(x_ref[ls, rs, :], m_ref, g_ref[...]).astype(BF16)


def _pre_call(x3, lay, mod, norm_pre, l, sub):
    return pl.pallas_call(
        _pre_kernel,
        grid=(lay.n_tiles,),
        in_specs=[lay.x_spec(), lay.mod_spec(mod, l, sub), _gain_spec(norm_pre, l, sub)],
        out_specs=lay.x_spec(),
        out_shape=jax.ShapeDtypeStruct(x3.shape, BF16),
        compiler_params=_params(1),
    )(x3, mod, norm_pre)


def _cast_weights_once(i, pairs):
    @pl.when(i == 0)
    def _():
        for src, dst in pairs:
            dst[...] = src[...].astype(BF16)


def _ffn_in_kernel(*refs, first_use, n_row_tiles):
    if first_use:
        h_ref, wg_ref, wu_ref, wo_ref, a_ref, wgb_ref, wub_ref, wob_ref = refs
        _cast_weights_once(pl.program_id(1), [(wg_ref, wgb_ref), (wu_ref, wub_ref)])
        share = wo_ref.shape[0] // n_row_tiles
        rows_i = pl.ds(pl.multiple_of(pl.program_id(1) * share, share), share)
        wob_ref[rows_i, :] = wo_ref[rows_i, :].astype(BF16)
    else:
        h_ref, wgb_ref, wub_ref, a_ref = refs
    lead, rows, d = h_ref.shape
    for ls, rs in _pieces(lead, rows):
        shp, n = _piece_rows(h_ref, ls, rs)
        h = h_ref[ls, rs, :].reshape(n, d)
        gt = lax.dot_general(h, wgb_ref[...], NN, preferred_element_type=F32)
        up = lax.dot_general(h, wub_ref[...], NN, preferred_element_type=F32)
        a_ref[ls, rs, :] = (gt * _sigmoid(gt) * up).astype(BF16).reshape(shp[0], shp[1], -1)


def _ffn_in_call(h3, lay, tf, f32_weights=None, bf16_weights=None):
    lead, rows, d = h3.shape
    first_use = f32_weights is not None
    h_spec = pl.BlockSpec(lay.block, lambda j, i: lay.xmap(i))
    blocked_spec = pl.BlockSpec((None, d, tf), lambda j, i: (j, 0, 0))
    if first_use:
        w_in, w_out, l, s = f32_weights
        f = w_out.shape[2]
        nj = f // tf
        assert tf % lay.n_tiles == 0
        in_specs = [h_spec,
                    pl.BlockSpec((None, None, d, tf), lambda j, i: (l, s, 0, j)),
                    pl.BlockSpec((None, None, d, tf), lambda j, i: (l, s, 0, nj + j)),
                    pl.BlockSpec((None, None, tf, d), lambda j, i: (l, s, j, 0))]
        args = [h3, w_in, w_in, w_out]
        extra_specs = [blocked_spec, blocked_spec, pl.BlockSpec((tf, d), lambda j, i: (j, 0))]
        extra_shape = [jax.ShapeDtypeStruct((nj, d, tf), BF16)] * 2 + [
            jax.ShapeDtypeStruct((f, d), BF16)]
    else:
        nj = bf16_weights[0].shape[0]
        f = nj * tf
        in_specs = [h_spec, blocked_spec, blocked_spec]
        args = [h3, *bf16_weights]
        extra_specs, extra_shape = [], []
    out = pl.pallas_call(
        functools.partial(_ffn_in_kernel, first_use=first_use, n_row_tiles=lay.n_tiles),
        grid=(nj, lay.n_tiles),
        in_specs=in_specs,
        out_specs=[pl.BlockSpec((lay.block[0], lay.block[1], tf),
                                lambda j, i: lay.xmap(i)[:2] + (j,))] + extra_specs,
        out_shape=[jax.ShapeDtypeStruct((lead, rows, f), BF16)] + extra_shape,
        compiler_params=_params(2),
    )(*args)
    return out[0], tuple(out[1:])


def _conv_kernel(*refs, time_major, tiles_per_seq):
    if time_major:
        (h_ref, wb_ref, wc_ref, wx_ref, cw_ref, st_ref,
         o_ref, so_ref, wbb_ref, wcb_ref, wxb_ref) = refs
    else:
        (h_ref, wb_ref, wc_ref, wx_ref, cw_ref,
         o_ref, so_ref, wbb_ref, wcb_ref, wxb_ref, carry_ref) = refs
    i = pl.program_id(1)
    _cast_weights_once(i, [(wb_ref, wbb_ref), (wc_ref, wcb_ref), (wx_ref, wxb_ref)])
    lead, rows, d = h_ref.shape
    tm = lead * rows

    h = h_ref[...].reshape(tm, d)

    def proj(w_ref):
        return lax.dot_general(h, w_ref[...], NN, preferred_element_type=F32)

    bg = proj(wbb_ref)
    u2 = proj(wcb_ref) * proj(wxb_ref)
    tn = u2.shape[1]
    u = u2.reshape(lead, rows, tn)
    if time_major:
        st = st_ref[...]
        fill1, fill2 = st[1:2], st
        so_ref[...] = u[lead - 2:lead]
    else:
        @pl.when(i % tiles_per_seq == 0)
        def _():
            carry_ref[...] = jnp.zeros((8, tn), F32)
        prev = carry_ref[...]
        fill1, fill2 = prev[7:8, :], prev[6:8, :]
        carry_ref[...] = u2[tm - 8:tm, :]
        so_ref[...] = u[:, rows - 2:rows, :]
    cw = cw_ref[...]
    z = (cw[0:1, :] * _shift_time(u, 2, fill2, time_major)
         + cw[1:2, :] * _shift_time(u, 1, fill1, time_major) + cw[2:3, :] * u)
    o_ref[...] = (bg.reshape(lead, rows, tn) * z).astype(BF16)


def _conv_call(h3, lay, w_in, cw, state, jl, tn):
    lead_n, rows_n, d = h3.shape
    dc = cw.shape[2]
    nj = dc // tn
    in_specs = [
        pl.BlockSpec(lay.block, lambda j, i: lay.xmap(i)),
        pl.BlockSpec((None, d, tn), lambda j, i: (jl, 0, j)),
        pl.BlockSpec((None, d, tn), lambda j, i: (jl, 0, nj + j)),
        pl.BlockSpec((None, d, tn), lambda j, i: (jl, 0, 2 * nj + j)),
        pl.BlockSpec((None, cw.shape[1], tn), lambda j, i: (jl, 0, j)),
    ]
    args = [h3, w_in, w_in, w_in, cw]
    scratch = [pltpu.VMEM((d, tn), BF16)] * 3
    if lay.time_major:
        nb = lay.block[1]
        in_specs.append(pl.BlockSpec((2, nb, tn), lambda j, i: (0, i, j)))
        args.append(state)
        so_spec = pl.BlockSpec((2, nb, tn), lambda j, i: (0, i, j))
        so_shape = jax.ShapeDtypeStruct((2, h3.shape[1], dc), F32)
    else:
        scratch.append(pltpu.VMEM((8, tn), F32))
        so_spec = pl.BlockSpec((1, 2, tn), lambda j, i: (i, 0, j))
        so_shape = jax.ShapeDtypeStruct((lay.n_tiles, 2, dc), F32)
    return pl.pallas_call(
        functools.partial(_conv_kernel, time_major=lay.time_major,
                          tiles_per_seq=lay.tiles_per_seq),
        grid=(nj, lay.n_tiles),
        in_specs=in_specs,
        out_specs=[pl.BlockSpec((lay.block[0], lay.block[1], tn),
                                lambda j, i: lay.xmap(i)[:2] + (j,)), so_spec],
        out_shape=[jax.ShapeDtypeStruct((lead_n, rows_n, dc), BF16), so_shape],
        scratch_shapes=scratch,
        compiler_params=_params(2),
    )(*args)


def _rw_prep_kernel(*refs, time_major, tiles_per_seq):
    if time_major:
        (x_ref, m_ref, gpre_ref, mix_ref, w1_ref, a1_ref, g1_ref, st_ref,
         xr_ref, xk_ref, xv_ref, tw_ref, ta_ref, tg_ref, so_ref) = refs
    else:
        (x_ref, m_ref, gpre_ref, mix_ref, w1_ref, a1_ref, g1_ref,
         xr_ref, xk_ref, xv_ref, tw_ref, ta_ref, tg_ref, so_ref, carry_ref) = refs
    i = pl.program_id(0)
    lead, rows, d = x_ref.shape
    n = lead * rows
    h = _modulated_pre(x_ref[...], m_ref, gpre_ref[...])
    if time_major:
        fill = st_ref[...][None]
        so_ref[...] = h[lead - 1]
    else:
        @pl.when(i % tiles_per_seq == 0)
        def _():
            carry_ref[...] = jnp.zeros((8, d), F32)
        fill = carry_ref[7:8, :]
        carry_ref[...] = h[0, rows - 8:rows, :]
        so_ref[...] = h[:, rows - 1:rows, :]
    xx = _shift_time(h, 1, fill, time_major) - h
    mix = mix_ref[...]

    def mixed(k):
        return h + xx * mix[k:k + 1, :]

    def low_rank(k, w_ref):
        return _mm(mixed(k).reshape(n, d), w_ref[...]).reshape(lead, rows, w_ref.shape[1])

    xr_ref[...] = mixed(0).astype(BF16)
    tw_ref[...] = jnp.tanh(low_rank(1, w1_ref))
    xk_ref[...] = mixed(2).astype(BF16)
    xv_ref[...] = mixed(3).astype(BF16)
    ta_ref[...] = low_rank(4, a1_ref)
    tg_ref[...] = _sigmoid(low_rank(5, g1_ref))


def _rw_prep_call(x3, lay, mod, norm_pre, mix, w1, a1, g1, state, l, jl, sub):
    lead, rows, d = x3.shape
    dl, dg = w1.shape[2], g1.shape[2]
    in_specs = [
        lay.x_spec(),
        lay.mod_spec(mod, l, sub),
        _gain_spec(norm_pre, l, sub),
        pl.BlockSpec((None, mix.shape[1], d), lambda i: (jl, 0, 0)),
        pl.BlockSpec((None, d, dl), lambda i: (jl, 0, 0)),
        pl.BlockSpec((None, d, dl), lambda i: (jl, 0, 0)),
        pl.BlockSpec((None, d, dg), lambda i: (jl, 0, 0)),
    ]
    args = [x3, mod, norm_pre, mix, w1, a1, g1]
    scratch = []
    blk = lay.block
    if lay.time_major:
        in_specs.append(pl.BlockSpec((blk[1], d), lambda i: (i, 0)))
        args.append(state)
        so_spec = pl.BlockSpec((blk[1], d), lambda i: (i, 0))
        so_shape = jax.ShapeDtypeStruct((rows, d), F32)
    else:
        scratch.append(pltpu.VMEM((8, d), F32))
        tps = lay.tiles_per_seq
        so_spec = pl.BlockSpec((1, 1, d), lambda i: (i // tps, 0, 0))
        so_shape = jax.ShapeDtypeStruct((lead, 1, d), F32)

    def ospec(width):
        return pl.BlockSpec((blk[0], blk[1], width), lay.xmap)

    return pl.pallas_call(
        functools.partial(_rw_prep_kernel, time_major=lay.time_major,
                          tiles_per_seq=lay.tiles_per_seq),
        grid=(lay.n_tiles,),
        in_specs=in_specs,
        out_specs=[ospec(d)] * 3 + [ospec(dl), ospec(dl), ospec(dg), so_spec],
        out_shape=[jax.ShapeDtypeStruct((lead, rows, d), BF16)] * 3
        + [jax.ShapeDtypeStruct((lead, rows, dl), F32)] * 2
        + [jax.ShapeDtypeStruct((lead, rows, dg), F32), so_shape],
        scratch_shapes=scratch,
        compiler_params=_params(1),
    )(*args)


def _rw_proj_kernel(xr_ref, xk_ref, xv_ref, tw_ref, ta_ref, tg_ref, wr_ref, wk_ref, wv_ref,
                    w2_ref, a2_ref, g2_ref, w0_ref, a0_ref,
                    r_ref, k_ref, v_ref, w_ref, a_ref, g_ref, wrb_ref, wkb_ref, wvb_ref,
                    *, cols_outer):
    pairs = [(wr_ref, wrb_ref), (wk_ref, wkb_ref), (wv_ref, wvb_ref)]
    if cols_outer:
        _cast_weights_once(pl.program_id(1), pairs)
    else:
        for src, dst in pairs:
            dst[...] = src[...].astype(BF16)

    def proj(x_ref, wb_ref):
        return lax.dot_general(x_ref[...], wb_ref[...], NN, preferred_element_type=F32)

    r_ref[...] = proj(xr_ref, wrb_ref)
    k_ref[...] = proj(xk_ref, wkb_ref)
    v_ref[...] = proj(xv_ref, wvb_ref)
    w_ref[...] = w0_ref[...] + _mm(tw_ref[...], w2_ref[...])
    a_ref[...] = _sigmoid(a0_ref[...] + _mm(ta_ref[...], a2_ref[...]))
    g_ref[...] = _mm(tg_ref[...], g2_ref[...])


def _rw_proj_call(xr, xk, xv, tw, ta, tg, wr, wk, wv, w2, a2, g2, w0, a0, jl, tm, tn):
    m, d = xr.shape
    dl, dg = tw.shape[1], tg.shape[1]
    nr = w0.shape[0]
    cols_outer = m // tm > 2
    grid = (d // tn, m // tm) if cols_outer else (m // tm, d // tn)

    def ij(f):
        return (lambda j, i: f(i, j)) if cols_outer else f

    xspec = pl.BlockSpec((tm, d), ij(lambda i, j: (i, 0)))
    wspec = pl.BlockSpec((None, d, tn), ij(lambda i, j: (jl, 0, j)))
    vspec = pl.BlockSpec((None, 1, tn), ij(lambda i, j: (jl, 0, j)))
    ospec = pl.BlockSpec((tm, tn), ij(lambda i, j: (i, j)))
    return pl.pallas_call(
        functools.partial(_rw_proj_kernel, cols_outer=cols_outer),
        grid=grid,
        in_specs=[xspec, xspec, xspec,
                  pl.BlockSpec((tm, dl), ij(lambda i, j: (i, 0))),
                  pl.BlockSpec((tm, dl), ij(lambda i, j: (i, 0))),
                  pl.BlockSpec((tm, dg), ij(lambda i, j: (i, 0))),
                  wspec, wspec, wspec,
                  pl.BlockSpec((None, dl, tn), ij(lambda i, j: (jl, 0, j))),
                  pl.BlockSpec((None, dl, tn), ij(lambda i, j: (jl, 0, j))),
                  pl.BlockSpec((None, dg, tn), ij(lambda i, j: (jl, 0, j))),
                  vspec, vspec],
        out_specs=[ospec] * 6,
        out_shape=[jax.ShapeDtypeStruct((m, d), F32)] * 6,
        scratch_shapes=[pltpu.VMEM((d, tn), BF16)] * 3,
        compiler_params=_params(2),
    )(xr, xk, xv, tw, ta, tg, wr, wk, wv, w2, a2, g2,
      w0.reshape(nr, 1, d), a0.reshape(nr, 1, d))


def _iota2(shape, axis):
    return lax.broadcasted_iota(jnp.int32, shape, axis)


def _same_block(shape, row_block, lane_block):
    r = lax.shift_right_logical(_iota2(shape, 0), int(math.log2(row_block)))
    c = lax.shift_right_logical(_iota2(shape, 1), int(math.log2(lane_block)))
    return r == c


def _bd_rows(x, mask):
    return jnp.where(mask, jnp.concatenate([x] * HEADS_PER_GROUP, axis=0), 0.0)


def _split_mm(x, ones):
    hi = x.astype(BF16)
    lo = (x - hi.astype(F32)).astype(BF16)
    return (lax.dot_general(hi, ones, NN, preferred_element_type=F32)
            + lax.dot_general(lo, ones, NN, preferred_element_type=F32))


def _each(fn, *lists):
    return [fn(*xs) for xs in zip(*lists)]


def _wkv_chunk(r, k, v, a_, b_, lw, cum, s_bd, c):
    hs = HEAD_SIZE
    tlanes = HEADS_PER_GROUP * c
    m_ch = _same_block((tlanes, GROUP_LANES), c, hs)
    m_tt = _same_block((tlanes, tlanes), c, c)
    m_ss = _same_block((GROUP_LANES, GROUP_LANES), hs, hs)
    t_row = _iota2((c, tlanes), 0)
    s_lane = jnp.bitwise_and(_iota2((c, tlanes), 1), c - 1)
    strict = s_lane < t_row
    incl = s_lane <= t_row

    def bd_ch(x):
        return _bd_rows(x, m_ch)

    def bd_tt(x):
        return _bd_rows(x, m_tt)

    def apply(a, x):
        return _mm(a, bd_ch(x))

    cl = _each(lambda x: x[c - 1:c, :], cum)
    at = _each(lambda a, x, l: a * jnp.exp(x - l), a_, cum, lw)
    rt = _each(lambda a, x: a * jnp.exp(x), r, cum)
    bt = _each(lambda a, x: a * jnp.exp(-x), b_, cum)
    kt = _each(lambda a, x: a * jnp.exp(-x), k, cum)
    bh = _each(lambda a, x, xl: a * jnp.exp(xl - x), b_, cum, cl)
    kh = _each(lambda a, x, xl: a * jnp.exp(xl - x), k, cum, cl)

    lhs = _each(lambda a, b: jnp.concatenate([a, b], axis=0), at, rt)
    ob = _each(lambda a, x: _mm(a, bd_ch(x), NT), lhs, bt)
    ok = _each(lambda a, x: _mm(a, bd_ch(x), NT), lhs, kt)
    a_ab = _each(lambda x: jnp.where(strict, x[:c], 0.0), ob)
    a_rb = _each(lambda x: jnp.where(incl, x[c:], 0.0), ob)
    a_ak = _each(lambda x: jnp.where(strict, x[:c], 0.0), ok)
    a_rk = _each(lambda x: jnp.where(incl, x[c:], 0.0), ok)
    akv = _each(lambda a, b, x: apply(jnp.concatenate([a, b], axis=0), x), a_ak, a_rk, v)
    av = _each(lambda x: x[:c], akv)
    y_kv = _each(lambda x: x[c:], akv)

    eye = jnp.where(s_lane == t_row, 1.0, 0.0)
    tinv = _each(lambda x: eye + x, a_ab)
    npow = _each(lambda x: _mm(x, bd_tt(x)), a_ab)
    for _ in range(int(math.log2(c)) - 2):
        both = _each(lambda t, x: _mm(jnp.concatenate([t, x], axis=0), bd_tt(x)), tinv, npow)
        tinv = _each(lambda t, z: t + z[:c], tinv, both)
        npow = _each(lambda z: z[c:], both)
    tinv = _each(lambda t, x: t + _mm(t, bd_tt(x)), tinv, npow)

    wt = _each(apply, tinv, at)
    w = _each(apply, tinv, av)
    qt = _each(lambda x, a, y: x + apply(a, y), rt, a_rb, wt)
    yi = _each(lambda a, x, y: apply(a, x) + y, a_rb, w, y_kv)
    g_off = _each(lambda x, y: jnp.where(m_ss, _mm(x, y, TN), 0.0), wt, bh)
    h_t = _each(lambda x, y, p, q: jnp.where(
        m_ss, _mm(jnp.concatenate([x, p], axis=0), jnp.concatenate([y, q], axis=0), TN), 0.0),
                w, bh, v, kh)

    y = _each(lambda q, s, x: _mm(q, s, NT) + x, qt, s_bd, yi)
    s_new = _each(lambda s, xl, g, h: s * jnp.exp(xl) + _mm(s, g) + h, s_bd, cl, g_off, h_t)
    return y, s_new


def _scan_kernel(*refs, c, d, nq, has_state, lane_packed):
    if has_state:
        (r_ref, k_ref, v_ref, w_ref, a_ref, g_ref, p_ref, s0_ref, y_ref, so_ref, sbd_ref) = refs
    else:
        (r_ref, k_ref, v_ref, w_ref, a_ref, g_ref, p_ref, y_ref, so_ref, sbd_ref) = refs
    ci = pl.program_id(1)
    hs = HEAD_SIZE
    ng = d // GROUP_LANES
    chains = [(q, g) for q in range(nq) for g in range(ng)]
    n_ch = len(chains)
    m_ss = _same_block((GROUP_LANES, GROUP_LANES), hs, hs)
    ones_bd = jnp.where(m_ss, 1.0, 0.0).astype(BF16)

    def lanes(g):
        return slice(g * GROUP_LANES, (g + 1) * GROUP_LANES)

    def rd(ref, q, g):
        if lane_packed:
            return ref[:, q * d + g * GROUP_LANES:q * d + (g + 1) * GROUP_LANES]
        return ref[q, :, lanes(g)]

    def par(row, g):
        return p_ref[row:row + 1, lanes(g)]

    @pl.when(ci == 0)
    def _():
        for i, (q, g) in enumerate(chains):
            if has_state:
                heads = [s0_ref[q, HEADS_PER_GROUP * g + h] for h in range(HEADS_PER_GROUP)]
                sbd_ref[i] = _bd_rows(jnp.concatenate(heads, axis=1), m_ss)
            else:
                sbd_ref[i] = jnp.zeros((GROUP_LANES, GROUP_LANES), F32)

    def log_decay(wq):
        z = -wq
        softplus = jnp.maximum(z, 0.0) + jnp.log(1.0 + jnp.exp(-jnp.abs(z)))
        lw = -jnp.exp(-softplus - 0.5)
        row = _iota2(lw.shape, 0)
        cum = lw
        step = 1
        while step < c:
            cum = cum + jnp.where(row >= step, pltpu.roll(cum, step, 0), 0.0)
            step *= 2
        return lw, cum

    if lane_packed:
        lw_all, cum_all = log_decay(w_ref[...])
        lw = [lw_all[:, q * d + g * GROUP_LANES:q * d + (g + 1) * GROUP_LANES] for q, g in chains]
        cum = [cum_all[:, q * d + g * GROUP_LANES:q * d + (g + 1) * GROUP_LANES] for q, g in chains]
    else:
        per_q = [log_decay(w_ref[q]) for q in range(nq)]
        lw = [per_q[q][0][:, lanes(g)] for q, g in chains]
        cum = [per_q[q][1][:, lanes(g)] for q, g in chains]

    def seg_sum(xs, split=True):
        x = jnp.concatenate(xs, axis=0)
        y = _split_mm(x, ones_bd) if split else _mm(x, ones_bd)
        return [y[i * c:(i + 1) * c] for i in range(len(xs))]

    r = [rd(r_ref, q, g) for q, g in chains]
    k = [rd(k_ref, q, g) for q, g in chains]
    v = [rd(v_ref, q, g) for q, g in chains]
    a_sig = [rd(a_ref, q, g) for q, g in chains]
    kk = [x * par(0, g) for x, (q, g) in zip(k, chains)]
    k = [x * (1.0 + (a - 1.0) * par(1, g)) for x, a, (q, g) in zip(k, a_sig, chains)]
    kk = _each(lambda x, n: x / jnp.maximum(jnp.sqrt(n), NORM_EPS), kk,
               seg_sum(_each(lambda x: x * x, kk)))
    bonus = _each(lambda s, z: s * z,
                  seg_sum([x * y * par(2, g) for x, y, (q, g) in zip(r, k, chains)], split=False),
                  v)
    y, s_new = _wkv_chunk(r, k, v, _each(lambda x: -x, kk), _each(lambda x, a: x * a, kk, a_sig),
                          lw, cum, [sbd_ref[i] for i in range(n_ch)], c)
    for i in range(n_ch):
        sbd_ref[i] = s_new[i]

    mu = _each(lambda x: x * (1.0 / hs), seg_sum(y))
    dy = _each(lambda x, m: x - m, y, mu)
    var = _each(lambda x: x * (1.0 / hs), seg_sum(_each(lambda x: x * x, dy), split=False))
    for i, (q, g) in enumerate(chains):
        yn = dy[i] * lax.rsqrt(var[i] + GN_EPS) * par(3, g) + par(4, g)
        out = ((yn + bonus[i]) * rd(g_ref, q, g)).astype(BF16)
        if lane_packed:
            y_ref[:, q * d + g * GROUP_LANES:q * d + (g + 1) * GROUP_LANES] = out
        else:
            y_ref[q, :, lanes(g)] = out

    @pl.when(ci == pl.num_programs(1) - 1)
    def _():
        for i, (q, g) in enumerate(chains):
            s = sbd_ref[i]
            s = s[0:hs] + s[hs:2 * hs] + s[2 * hs:3 * hs] + s[3 * hs:4 * hs]
            for h in range(HEADS_PER_GROUP):
                so_ref[q, HEADS_PER_GROUP * g + h] = s[:, h * hs:(h + 1) * hs]


def _scan_call(r, k, v, w, a, g, p, s0, nb, t, c, nq, time_major):
    m, d = r.shape
    nc = t // c
    ng = d // GROUP_LANES
    nh = d // HEAD_SIZE
    has_state = s0 is not None
    if time_major:
        r, k, v, w, a, g = (z.reshape(t, nb * d) for z in (r, k, v, w, a, g))
        xspec = pl.BlockSpec((c, nq * d), lambda b, ci: (ci, b))
    else:
        r, k, v, w, a, g = (z.reshape(nb, t, d) for z in (r, k, v, w, a, g))
        xspec = pl.BlockSpec((nq, c, d), lambda b, ci: (b, ci, 0))
    sspec = pl.BlockSpec((nq, nh, HEAD_SIZE, HEAD_SIZE), lambda b, ci: (b, 0, 0, 0))
    in_specs = [xspec] * 6 + [pl.BlockSpec(p.shape, lambda b, ci: (0, 0))]
    args = [r, k, v, w, a, g, p]
    if has_state:
        in_specs.append(sspec)
        args.append(s0)
    y, s_fin = pl.pallas_call(
        functools.partial(_scan_kernel, c=c, d=d, nq=nq, has_state=has_state,
                          lane_packed=time_major),
        grid=(nb // nq, nc),
        in_specs=in_specs,
        out_specs=[xspec, sspec],
        out_shape=[jax.ShapeDtypeStruct(r.shape, BF16),
                   jax.ShapeDtypeStruct((nb, nh, HEAD_SIZE, HEAD_SIZE), F32)],
        scratch_shapes=[pltpu.VMEM((nq * ng, GROUP_LANES, GROUP_LANES), F32)],
        compiler_params=_params(2),
    )(*args)
    return y.reshape(m, d), s_fin


class _Tiles(NamedTuple):
    ffn: int
    conv: int
    out: int
    prep: int
    chunk: int
    scan_seqs: int


def _tiles(time_major, nb, t):
    if time_major:
        return _Tiles(ffn=nb, conv=nb, out=min(nb, 32), prep=min(nb, 32), chunk=t,
                      scan_seqs=math.gcd(nb, 2))
    return _Tiles(ffn=min(t, 2048), conv=min(t, 2048), out=min(t, 512), prep=min(t, 256),
                  chunk=min(t, 64), scan_seqs=math.gcd(nb, 2))


def _trunk(x3, mod, conv_state, shift_state, wkv_state, wts, *, time_major):
    d = x3.shape[2]
    nb, t = (x3.shape[1], x3.shape[0]) if time_major else (x3.shape[0], x3.shape[1])
    depth = mod.shape[0]
    tiles = _tiles(time_major, nb, t)
    chunk, scan_seqs = tiles.chunk, tiles.scan_seqs
    ffn_lay = _Layout(x3.shape, time_major, tiles.ffn)
    lay = _Layout(x3.shape, time_major, tiles.conv)
    out_lay = _Layout(x3.shape, time_major, tiles.out)
    prep_lay = _Layout(x3.shape, time_major, tiles.prep)
    m = nb * t
    new_conv, new_shift, new_wkv = [], [], []

    subs = [(l, sub) for l in range(depth) for sub in range(3)]

    def takes_h(l, sub):
        return not (sub == 1 and l % 2 == 1)

    def nxt_of(l, sub):
        i = subs.index((l, sub)) + 1
        if i < len(subs) and takes_h(*subs[i]):
            return (wts["norm_pre"],) + subs[i]
        return None

    def ffn(x3, h3, l, s, sub):
        cached = wts["ffn_bf16"].get((l, s))
        if cached is None:
            act, cached = _ffn_in_call(h3, ffn_lay, 512, f32_weights=(
                wts["ffn_w_in"], wts["ffn_w_out"], l, s))
            wts["ffn_bf16"][(l, s)] = cached
        else:
            act, _ = _ffn_in_call(h3, ffn_lay, 512, bf16_weights=cached[:2])
        w_out = cached[2]
        return _out_proj_call(x3, out_lay, mod, wts["norm_post"], act, w_out[None, None],
                              l, (0, 0), sub, HALF_STEP, nxt_of(l, sub))

    h3 = _pre_call(x3, out_lay, mod, wts["norm_pre"], 0, 0)
    for l in range(depth):
        jl = l // 2
        x3, h3 = ffn(x3, h3, l, 0, 0)
        if l % 2 == 0:
            st = conv_state[jl].transpose(1, 0, 2) if time_major else None
            bz, cs = _conv_call(h3, lay, wts["conv_w_in"], wts["conv_w"], st, jl, 256)
            x3, h3 = _out_proj_call(x3, out_lay, mod, wts["norm_post"], bz, wts["conv_w_out"],
                                    l, (jl, 0), 1, 1.0, nxt_of(l, 1))
            if time_major:
                new_conv.append(cs.transpose(1, 0, 2))
            else:
                tps = lay.tiles_per_seq
                new_conv.append(cs[tps - 1::tps])
        else:
            st = shift_state[jl] if time_major else None
            xr, xk, xv, tw, ta, tg, ss = _rw_prep_call(
                x3, prep_lay, mod, wts["norm_pre"], wts["rw_mix"], wts["rw_w1"], wts["rw_a1"],
                wts["rw_g1"], st, l, jl, 1)
            flat = lambda z: z.reshape(m, z.shape[2])
            r, k, v, w, a, g = _rw_proj_call(
                flat(xr), flat(xk), flat(xv), flat(tw), flat(ta), flat(tg),
                wts["rw_wr"], wts["rw_wk"], wts["rw_wv"], wts["rw_w2"], wts["rw_a2"],
                wts["rw_g2"], wts["rw_w0"], wts["rw_a0"], jl,
                *((512, min(d, 512)) if m > 1024 else (m, min(d, 256))))
            p = jnp.concatenate([
                wts["rw_kk"][jl][None], wts["rw_ka"][jl][None], wts["rw_rk"][jl].reshape(1, d),
                wts["rw_lnw"][jl][None], wts["rw_lnb"][jl][None],
                jnp.zeros((3, d), F32)], axis=0)
            s0 = wkv_state[jl] if time_major else None
            yg, s_fin = _scan_call(r, k, v, w, a, g, p, s0, nb, t, chunk, scan_seqs, time_major)
            x3, h3 = _out_proj_call(x3, out_lay, mod, wts["norm_post"], yg.reshape(x3.shape),
                                    wts["rw_wo"], l, (jl, 0), 1, 1.0, nxt_of(l, 1))
            new_shift.append(ss.reshape(nb, d))
            new_wkv.append(s_fin)
        x3, h3 = ffn(x3, h3, l, 1, 2)
    def stack(xs):
        return xs[0][None] if len(xs) == 1 else jnp.stack(xs)

    return x3, stack(new_conv), stack(new_shift), stack(new_wkv)


def kernel(x_prompt, x_sample, state_conv, state_shift, state_wkv, c_prompt, c_sample, mod_w, mod_b, norm_pre, norm_post, ffn_w_in, ffn_w_out, conv_w_in, conv_w, conv_w_out, rw_mix, rw_w0, rw_w1, rw_w2, rw_a0, rw_a1, rw_a2, rw_g1, rw_g2, rw_kk, rw_ka, rw_rk, rw_wr, rw_wk, rw_wv, rw_wo, rw_lnw, rw_lnb):
    b, t, d = x_prompt.shape
    sb, st, _ = x_sample.shape
    depth = mod_w.shape[0]
    n_sub = norm_pre.shape[1]
    wts = dict(norm_pre=norm_pre.reshape(depth, n_sub, 1, d),
               norm_post=norm_post.reshape(depth, n_sub, 1, d), ffn_w_in=ffn_w_in,
               ffn_w_out=ffn_w_out, ffn_bf16={},
               conv_w_out=conv_w_out.astype(BF16)[:, None],
               rw_wo=rw_wo.astype(BF16)[:, None],
               conv_w_in=conv_w_in, conv_w=conv_w, rw_mix=rw_mix,
               rw_w0=rw_w0, rw_w1=rw_w1, rw_w2=rw_w2, rw_a0=rw_a0, rw_a1=rw_a1, rw_a2=rw_a2,
               rw_g1=rw_g1, rw_g2=rw_g2, rw_kk=rw_kk, rw_ka=rw_ka, rw_rk=rw_rk, rw_wr=rw_wr,
               rw_wk=rw_wk, rw_wv=rw_wv, rw_lnw=rw_lnw, rw_lnb=rw_lnb)

    n_c = b + sb
    pad = (-n_c) % 8
    c_all = jnp.concatenate([c_prompt, c_sample, jnp.zeros((pad, d), F32)], axis=0)
    mod_all = _mod_call(c_all, mod_w, mod_b)
    mod_p = (mod_all[:, :b].reshape(depth, b, N_MOD, d).transpose(0, 2, 1, 3)
             .reshape(depth, N_MOD, b, 1, d))
    mod_s = mod_all[:, b:n_c].reshape(depth, sb, N_MOD, d).transpose(0, 2, 1, 3)

    y_p, conv_p, shift_p, wkv_p = _trunk(
        x_prompt, mod_p, None, None, None, wts, time_major=False)
    y_s, conv_s, shift_s, wkv_s = _trunk(
        x_sample.transpose(1, 0, 2), mod_s, state_conv, state_shift, state_wkv, wts,
        time_major=True)
    return (y_p, y_s.transpose(1, 0, 2), conv_p, shift_p, wkv_p, conv_s, shift_s, wkv_s)
```

```python
import functools
import math
from typing import NamedTuple

import jax
import jax.numpy as jnp
from jax import lax
from jax.experimental import pallas as pl
from jax.experimental.pallas import tpu as pltpu

F32 = jnp.float32
BF16 = jnp.bfloat16

RMS_EPS = 1e-6
GN_EPS = 64e-5
NORM_EPS = 1e-12
HALF_STEP = 0.5
HEAD_SIZE = 64
N_MOD = 9

MXU_WIDTH_V7X = 256
GROUP_LANES = MXU_WIDTH_V7X
HEADS_PER_GROUP = GROUP_LANES // HEAD_SIZE
VMEM_LIMIT_V7X = 60 * 2**20
ROW_PIECE = 256
INV_BASE = 16

NN = (((1,), (0,)), ((), ()))
NT = (((1,), (1,)), ((), ()))
TN = (((0,), (0,)), ((), ()))


def _mm(a, b, dims=NN):
    return lax.dot_general(a.astype(BF16), b.astype(BF16), dims, preferred_element_type=F32)


def _sigmoid(x):
    return 1.0 / (1.0 + jnp.exp(-x))


def _params(n_axes):
    return pltpu.CompilerParams(dimension_semantics=("arbitrary",) * n_axes,
                                vmem_limit_bytes=VMEM_LIMIT_V7X)


def _rms(x, g):
    return x * lax.rsqrt(jnp.mean(x * x, axis=-1, keepdims=True) + RMS_EPS) * g


def _modulated_pre(x, m_ref, g):
    return _rms(x, g) * (1.0 + m_ref[1]) + m_ref[0]


def _gated_post(x, y, m_ref, g, res_w):
    return x + (res_w * m_ref[2]) * _rms(y, g)


def _pieces(lead, rows):
    if lead == 1:
        n = min(ROW_PIECE, rows)
        return [(slice(0, 1), slice(r0, r0 + n)) for r0 in range(0, rows, n)]
    n = min(lead, max(1, ROW_PIECE // rows))
    return [(slice(a0, a0 + n), slice(0, rows)) for a0 in range(0, lead, n)]


def _piece_rows(x_ref, ls, rs):
    shp = x_ref[ls, rs, :].shape
    return shp, shp[0] * shp[1]


def _shift_time(u, k, fill, time_major):
    lead, rows, n = u.shape
    if time_major:
        return jnp.concatenate([fill, u[:lead - k]], axis=0)
    out = pltpu.roll(u.reshape(rows, n), k, 0)
    row = lax.broadcasted_iota(jnp.int32, (rows, n), 0)
    for t in range(k):
        out = jnp.where(row == t, fill[t:t + 1, :], out)
    return out.reshape(1, rows, n)


class _Layout:
    def __init__(self, shape, time_major, rows_per_tile):
        self.time_major = time_major
        self.shape = shape
        lead, rows, d = shape
        if time_major:
            self.block = (lead, rows_per_tile, d)
            self.tiles_per_seq = 1
            self.n_tiles = rows // rows_per_tile
        else:
            self.block = (1, rows_per_tile, d)
            self.tiles_per_seq = rows // rows_per_tile
            self.n_tiles = lead * self.tiles_per_seq
        self.tile_rows = self.block[0] * self.block[1]

    def xmap(self, i, *_):
        if self.time_major:
            return (0, i, 0)
        return (i // self.tiles_per_seq, i % self.tiles_per_seq, 0)

    def x_spec(self, **kw):
        return pl.BlockSpec(self.block, self.xmap, **kw)

    def mod_spec(self, mod, l, sub):
        d = self.shape[2]
        if self.time_major:
            return pl.BlockSpec((None, 3, self.block[1], d), lambda i, *_: (l, sub, i, 0))
        tps = self.tiles_per_seq
        return pl.BlockSpec((None, 3, None, 1, d), lambda i, *_: (l, sub, i // tps, 0, 0))


def _gain_spec(gains, l, sub):
    return pl.BlockSpec((None, None, 1, gains.shape[3]), lambda i, *_: (l, sub, 0, 0))


def _mod_kernel(c_ref, w_ref, b_ref, o_ref):
    c = c_ref[...]
    o_ref[...] = _mm(c * _sigmoid(c), w_ref[...]) + b_ref[...]


def _mod_call(c_all, mod_w, mod_b):
    depth, d, n = mod_w.shape
    nbp = c_all.shape[0]
    tn = math.gcd(n, 1024)
    return pl.pallas_call(
        _mod_kernel,
        grid=(depth, n // tn),
        in_specs=[
            pl.BlockSpec((nbp, d), lambda l, j: (0, 0)),
            pl.BlockSpec((None, d, tn), lambda l, j: (l, 0, j)),
            pl.BlockSpec((None, 1, tn), lambda l, j: (l, 0, j)),
        ],
        out_specs=pl.BlockSpec((None, nbp, tn), lambda l, j: (l, 0, j)),
        out_shape=jax.ShapeDtypeStruct((depth, nbp, n), F32),
        compiler_params=_params(2),
    )(c_all, mod_w, mod_b.reshape(depth, 1, n))


def _out_proj_kernel(*refs, res_w, emit_next):
    if emit_next:
        x_ref, m_ref, gpost_ref, a_ref, w_ref, mn_ref, gn_ref, o_ref, h_ref = refs
    else:
        x_ref, m_ref, gpost_ref, a_ref, w_ref, o_ref = refs
    lead, rows, _ = x_ref.shape
    for ls, rs in _pieces(lead, rows):
        shp, n = _piece_rows(x_ref, ls, rs)
        a = a_ref[ls, rs, :]
        y = lax.dot_general(a.reshape(n, a.shape[2]), w_ref[...], NN,
                            preferred_element_type=F32).reshape(shp)
        o = _gated_post(x_ref[ls, rs, :], y, m_ref, gpost_ref[...], res_w)
        o_ref[ls, rs, :] = o
        if emit_next:
            h_ref[ls, rs, :] = _modulated_pre(o, mn_ref, gn_ref[...]).astype(BF16)


def _out_proj_call(x3, lay, mod, norm_post, act, w, l, widx, sub, res_w, nxt=None):
    k, d = w.shape[2], w.shape[3]
    in_specs = [
        lay.x_spec(),
        lay.mod_spec(mod, l, sub),
        _gain_spec(norm_post, l, sub),
        pl.BlockSpec((lay.block[0], lay.block[1], k), lay.xmap),
        pl.BlockSpec((None, None, k, d), lambda i: (widx[0], widx[1], 0, 0),
                     pipeline_mode=pl.Buffered(1)),
    ]
    args = [x3, mod, norm_post, act, w]
    out_specs = [lay.x_spec()]
    out_shape = [jax.ShapeDtypeStruct(x3.shape, F32)]
    if nxt is not None:
        norm_pre, ln, subn = nxt
        in_specs += [lay.mod_spec(mod, ln, subn), _gain_spec(norm_pre, ln, subn)]
        args += [mod, norm_pre]
        out_specs.append(lay.x_spec())
        out_shape.append(jax.ShapeDtypeStruct(x3.shape, BF16))
    out = pl.pallas_call(
        functools.partial(_out_proj_kernel, res_w=res_w, emit_next=nxt is not None),
        grid=(lay.n_tiles,),
        in_specs=in_specs,
        out_specs=out_specs,
        out_shape=out_shape,
        compiler_params=_params(1),
    )(*args)
    return (out[0], out[1]) if nxt is not None else (out[0], None)


def _pre_kernel(x_ref, m_ref, g_ref, h_ref):
    lead, rows, _ = x_ref.shape
    for ls, rs in _pieces(lead, rows):
        h_ref[ls, rs, :] = _modulated_pre(x_ref[ls, rs, :], m_ref, g_ref[...]).astype(BF16)


def _pre_call(x3, lay, mod, norm_pre, l, sub):
    return pl.pallas_call(
        _pre_kernel,
        grid=(lay.n_tiles,),
        in_specs=[lay.x_spec(), lay.mod_spec(mod, l, sub), _gain_spec(norm_pre, l, sub)],
        out_specs=lay.x_spec(),
        out_shape=jax.ShapeDtypeStruct(x3.shape, BF16),
        compiler_params=_params(1),
    )(x3, mod, norm_pre)


def _cast_weights_once(i, pairs):
    @pl.when(i == 0)
    def _():
        for src, dst in pairs:
            dst[...] = src[...].astype(BF16)


def _ffn_in_kernel(*refs, first_use, n_row_tiles):
    if first_use:
        h_ref, wg_ref, wu_ref, wo_ref, a_ref, wgb_ref, wub_ref, wob_ref = refs
        _cast_weights_once(pl.program_id(1), [(wg_ref, wgb_ref), (wu_ref, wub_ref)])
        share = wo_ref.shape[0] // n_row_tiles
        rows_i = pl.ds(pl.multiple_of(pl.program_id(1) * share, share), share)
        wob_ref[rows_i, :] = wo_ref[rows_i, :].astype(BF16)
    else:
        h_ref, wgb_ref, wub_ref, a_ref = refs
    lead, rows, d = h_ref.shape
    for ls, rs in _pieces(lead, rows):
        shp, n = _piece_rows(h_ref, ls, rs)
        h = h_ref[ls, rs, :].reshape(n, d)
        gt = lax.dot_general(h, wgb_ref[...], NN, preferred_element_type=F32)
        up = lax.dot_general(h, wub_ref[...], NN, preferred_element_type=F32)
        a_ref[ls, rs, :] = (gt * _sigmoid(gt) * up).astype(BF16).reshape(shp[0], shp[1], -1)


def _ffn_in_call(h3, lay, tf, f32_weights=None, bf16_weights=None):
    lead, rows, d = h3.shape
    first_use = f32_weights is not None
    h_spec = pl.BlockSpec(lay.block, lambda j, i: lay.xmap(i))
    blocked_spec = pl.BlockSpec((None, d, tf), lambda j, i: (j, 0, 0))
    if first_use:
        w_in, w_out, l, s = f32_weights
        f = w_out.shape[2]
        nj = f // tf
        assert tf % lay.n_tiles == 0
        in_specs = [h_spec,
                    pl.BlockSpec((None, None, d, tf), lambda j, i: (l, s, 0, j)),
                    pl.BlockSpec((None, None, d, tf), lambda j, i: (l, s, 0, nj + j)),
                    pl.BlockSpec((None, None, tf, d), lambda j, i: (l, s, j, 0))]
        args = [h3, w_in, w_in, w_out]
        extra_specs = [blocked_spec, blocked_spec, pl.BlockSpec((tf, d), lambda j, i: (j, 0))]
        extra_shape = [jax.ShapeDtypeStruct((nj, d, tf), BF16)] * 2 + [
            jax.ShapeDtypeStruct((f, d), BF16)]
    else:
        nj = bf16_weights[0].shape[0]
        f = nj * tf
        in_specs = [h_spec, blocked_spec, blocked_spec]
        args = [h3, *bf16_weights]
        extra_specs, extra_shape = [], []
    out = pl.pallas_call(
        functools.partial(_ffn_in_kernel, first_use=first_use, n_row_tiles=lay.n_tiles),
        grid=(nj, lay.n_tiles),
        in_specs=in_specs,
        out_specs=[pl.BlockSpec((lay.block[0], lay.block[1], tf),
                                lambda j, i: lay.xmap(i)[:2] + (j,))] + extra_specs,
        out_shape=[jax.ShapeDtypeStruct((lead, rows, f), BF16)] + extra_shape,
        compiler_params=_params(2),
    )(*args)
    return out[0], tuple(out[1:])


def _conv_kernel(*refs, time_major, tiles_per_seq):
    if time_major:
        (h_ref, wb_ref, wc_ref, wx_ref, cw_ref, st_ref,
         o_ref, so_ref, wbb_ref, wcb_ref, wxb_ref) = refs
    else:
        (h_ref, wb_ref, wc_ref, wx_ref, cw_ref,
         o_ref, so_ref, wbb_ref, wcb_ref, wxb_ref, carry_ref) = refs
    i = pl.program_id(1)
    _cast_weights_once(i, [(wb_ref, wbb_ref), (wc_ref, wcb_ref), (wx_ref, wxb_ref)])
    lead, rows, d = h_ref.shape
    tm = lead * rows

    h = h_ref[...].reshape(tm, d)

    def proj(w_ref):
        return lax.dot_general(h, w_ref[...], NN, preferred_element_type=F32)

    bg = proj(wbb_ref)
    u2 = proj(wcb_ref) * proj(wxb_ref)
    tn = u2.shape[1]
    u = u2.reshape(lead, rows, tn)
    if time_major:
        st = st_ref[...]
        fill1, fill2 = st[1:2], st
        so_ref[...] = u[lead - 2:lead]
    else:
        @pl.when(i % tiles_per_seq == 0)
        def _():
            carry_ref[...] = jnp.zeros((8, tn), F32)
        prev = carry_ref[...]
        fill1, fill2 = prev[7:8, :], prev[6:8, :]
        carry_ref[...] = u2[tm - 8:tm, :]
        so_ref[...] = u[:, rows - 2:rows, :]
    cw = cw_ref[...]
    z = (cw[0:1, :] * _shift_time(u, 2, fill2, time_major)
         + cw[1:2, :] * _shift_time(u, 1, fill1, time_major) + cw[2:3, :] * u)
    o_ref[...] = (bg.reshape(lead, rows, tn) * z).astype(BF16)


def _conv_call(h3, lay, w_in, cw, state, jl, tn):
    lead_n, rows_n, d = h3.shape
    dc = cw.shape[2]
    nj = dc // tn
    in_specs = [
        pl.BlockSpec(lay.block, lambda j, i: lay.xmap(i)),
        pl.BlockSpec((None, d, tn), lambda j, i: (jl, 0, j)),
        pl.BlockSpec((None, d, tn), lambda j, i: (jl, 0, nj + j)),
        pl.BlockSpec((None, d, tn), lambda j, i: (jl, 0, 2 * nj + j)),
        pl.BlockSpec((None, cw.shape[1], tn), lambda j, i: (jl, 0, j)),
    ]
    args = [h3, w_in, w_in, w_in, cw]
    scratch = [pltpu.VMEM((d, tn), BF16)] * 3
    if lay.time_major:
        nb = lay.block[1]
        in_specs.append(pl.BlockSpec((2, nb, tn), lambda j, i: (0, i, j)))
        args.append(state)
        so_spec = pl.BlockSpec((2, nb, tn), lambda j, i: (0, i, j))
        so_shape = jax.ShapeDtypeStruct((2, h3.shape[1], dc), F32)
    else:
        scratch.append(pltpu.VMEM((8, tn), F32))
        so_spec = pl.BlockSpec((1, 2, tn), lambda j, i: (i, 0, j))
        so_shape = jax.ShapeDtypeStruct((lay.n_tiles, 2, dc), F32)
    return pl.pallas_call(
        functools.partial(_conv_kernel, time_major=lay.time_major,
                          tiles_per_seq=lay.tiles_per_seq),
        grid=(nj, lay.n_tiles),
        in_specs=in_specs,
        out_specs=[pl.BlockSpec((lay.block[0], lay.block[1], tn),
                                lambda j, i: lay.xmap(i)[:2] + (j,)), so_spec],
        out_shape=[jax.ShapeDtypeStruct((lead_n, rows_n, dc), BF16), so_shape],
        scratch_shapes=scratch,
        compiler_params=_params(2),
    )(*args)


def _rw_prep_kernel(*refs, time_major, tiles_per_seq):
    if time_major:
        (x_ref, m_ref, gpre_ref, mix_ref, w1_ref, a1_ref, g1_ref, st_ref,
         xr_ref, xk_ref, xv_ref, tw_ref, ta_ref, tg_ref, so_ref) = refs
    else:
        (x_ref, m_ref, gpre_ref, mix_ref, w1_ref, a1_ref, g1_ref,
         xr_ref, xk_ref, xv_ref, tw_ref, ta_ref, tg_ref, so_ref, carry_ref) = refs
    i = pl.program_id(0)
    lead, rows, d = x_ref.shape
    n = lead * rows
    h = _modulated_pre(x_ref[...], m_ref, gpre_ref[...])
    if time_major:
        fill = st_ref[...][None]
        so_ref[...] = h[lead - 1]
    else:
        @pl.when(i % tiles_per_seq == 0)
        def _():
            carry_ref[...] = jnp.zeros((8, d), F32)
        fill = carry_ref[7:8, :]
        carry_ref[...] = h[0, rows - 8:rows, :]
        so_ref[...] = h[:, rows - 1:rows, :]
    xx = _shift_time(h, 1, fill, time_major) - h
    mix = mix_ref[...]

    def mixed(k):
        return h + xx * mix[k:k + 1, :]

    def low_rank(k, w_ref):
        return _mm(mixed(k).reshape(n, d), w_ref[...]).reshape(lead, rows, w_ref.shape[1])

    xr_ref[...] = mixed(0).astype(BF16)
    tw_ref[...] = jnp.tanh(low_rank(1, w1_ref))
    xk_ref[...] = mixed(2).astype(BF16)
    xv_ref[...] = mixed(3).astype(BF16)
    ta_ref[...] = low_rank(4, a1_ref)
    tg_ref[...] = _sigmoid(low_rank(5, g1_ref))


def _rw_prep_call(x3, lay, mod, norm_pre, mix, w1, a1, g1, state, l, jl, sub):
    lead, rows, d = x3.shape
    dl, dg = w1.shape[2], g1.shape[2]
    in_specs = [
        lay.x_spec(),
        lay.mod_spec(mod, l, sub),
        _gain_spec(norm_pre, l, sub),
        pl.BlockSpec((None, mix.shape[1], d), lambda i: (jl, 0, 0)),
        pl.BlockSpec((None, d, dl), lambda i: (jl, 0, 0)),
        pl.BlockSpec((None, d, dl), lambda i: (jl, 0, 0)),
        pl.BlockSpec((None, d, dg), lambda i: (jl, 0, 0)),
    ]
    args = [x3, mod, norm_pre, mix, w1, a1, g1]
    scratch = []
    blk = lay.block
    if lay.time_major:
        in_specs.append(pl.BlockSpec((blk[1], d), lambda i: (i, 0)))
        args.append(state)
        so_spec = pl.BlockSpec((blk[1], d), lambda i: (i, 0))
        so_shape = jax.ShapeDtypeStruct((rows, d), F32)
    else:
        scratch.append(pltpu.VMEM((8, d), F32))
        tps = lay.tiles_per_seq
        so_spec = pl.BlockSpec((1, 1, d), lambda i: (i // tps, 0, 0))
        so_shape = jax.ShapeDtypeStruct((lead, 1, d), F32)

    def ospec(width):
        return pl.BlockSpec((blk[0], blk[1], width), lay.xmap)

    return pl.pallas_call(
        functools.partial(_rw_prep_kernel, time_major=lay.time_major,
                          tiles_per_seq=lay.tiles_per_seq),
        grid=(lay.n_tiles,),
        in_specs=in_specs,
        out_specs=[ospec(d)] * 3 + [ospec(dl), ospec(dl), ospec(dg), so_spec],
        out_shape=[jax.ShapeDtypeStruct((lead, rows, d), BF16)] * 3
        + [jax.ShapeDtypeStruct((lead, rows, dl), F32)] * 2
        + [jax.ShapeDtypeStruct((lead, rows, dg), F32), so_shape],
        scratch_shapes=scratch,
        compiler_params=_params(1),
    )(*args)


def _rw_proj_kernel(xr_ref, xk_ref, xv_ref, tw_ref, ta_ref, tg_ref, wr_ref, wk_ref, wv_ref,
                    w2_ref, a2_ref, g2_ref, w0_ref, a0_ref,
                    r_ref, k_ref, v_ref, w_ref, a_ref, g_ref, wrb_ref, wkb_ref, wvb_ref,
                    *, cols_outer):
    pairs = [(wr_ref, wrb_ref), (wk_ref, wkb_ref), (wv_ref, wvb_ref)]
    if cols_outer:
        _cast_weights_once(pl.program_id(1), pairs)
    else:
        for src, dst in pairs:
            dst[...] = src[...].astype(BF16)

    def proj(x_ref, wb_ref):
        return lax.dot_general(x_ref[...], wb_ref[...], NN, preferred_element_type=F32)

    r_ref[...] = proj(xr_ref, wrb_ref)
    k_ref[...] = proj(xk_ref, wkb_ref)
    v_ref[...] = proj(xv_ref, wvb_ref)
    w_ref[...] = w0_ref[...] + _mm(tw_ref[...], w2_ref[...])
    a_ref[...] = _sigmoid(a0_ref[...] + _mm(ta_ref[...], a2_ref[...]))
    g_ref[...] = _mm(tg_ref[...], g2_ref[...])


def _rw_proj_call(xr, xk, xv, tw, ta, tg, wr, wk, wv, w2, a2, g2, w0, a0, jl, tm, tn):
    m, d = xr.shape
    dl, dg = tw.shape[1], tg.shape[1]
    nr = w0.shape[0]
    cols_outer = m // tm > 2
    grid = (d // tn, m // tm) if cols_outer else (m // tm, d // tn)

    def ij(f):
        return (lambda j, i: f(i, j)) if cols_outer else f

    xspec = pl.BlockSpec((tm, d), ij(lambda i, j: (i, 0)))
    wspec = pl.BlockSpec((None, d, tn), ij(lambda i, j: (jl, 0, j)))
    vspec = pl.BlockSpec((None, 1, tn), ij(lambda i, j: (jl, 0, j)))
    ospec = pl.BlockSpec((tm, tn), ij(lambda i, j: (i, j)))
    return pl.pallas_call(
        functools.partial(_rw_proj_kernel, cols_outer=cols_outer),
        grid=grid,
        in_specs=[xspec, xspec, xspec,
                  pl.BlockSpec((tm, dl), ij(lambda i, j: (i, 0))),
                  pl.BlockSpec((tm, dl), ij(lambda i, j: (i, 0))),
                  pl.BlockSpec((tm, dg), ij(lambda i, j: (i, 0))),
                  wspec, wspec, wspec,
                  pl.BlockSpec((None, dl, tn), ij(lambda i, j: (jl, 0, j))),
                  pl.BlockSpec((None, dl, tn), ij(lambda i, j: (jl, 0, j))),
                  pl.BlockSpec((None, dg, tn), ij(lambda i, j: (jl, 0, j))),
                  vspec, vspec],
        out_specs=[ospec] * 6,
        out_shape=[jax.ShapeDtypeStruct((m, d), F32)] * 6,
        scratch_shapes=[pltpu.VMEM((d, tn), BF16)] * 3,
        compiler_params=_params(2),
    )(xr, xk, xv, tw, ta, tg, wr, wk, wv, w2, a2, g2,
      w0.reshape(nr, 1, d), a0.reshape(nr, 1, d))


def _iota2(shape, axis):
    return lax.broadcasted_iota(jnp.int32, shape, axis)


def _same_block(shape, row_block, lane_block):
    r = lax.shift_right_logical(_iota2(shape, 0), int(math.log2(row_block)))
    c = lax.shift_right_logical(_iota2(shape, 1), int(math.log2(lane_block)))
    return r == c


def _bd_rows(x, mask):
    return jnp.where(mask, jnp.concatenate([x] * HEADS_PER_GROUP, axis=0), 0.0)


def _split_mm(x, ones):
    hi = x.astype(BF16)
    lo = (x - hi.astype(F32)).astype(BF16)
    return (lax.dot_general(hi, ones, NN, preferred_element_type=F32)
            + lax.dot_general(lo, ones, NN, preferred_element_type=F32))


def _each(fn, *lists):
    return [fn(*xs) for xs in zip(*lists)]


def _wkv_chunk(r, k, v, a_, b_, lw, cum, s_bd, c):
    hs = HEAD_SIZE
    tlanes = HEADS_PER_GROUP * c
    m_ch = _same_block((tlanes, GROUP_LANES), c, hs)
    m_tt = _same_block((tlanes, tlanes), c, c)
    m_ss = _same_block((GROUP_LANES, GROUP_LANES), hs, hs)
    t_row = _iota2((c, tlanes), 0)
    s_lane = jnp.bitwise_and(_iota2((c, tlanes), 1), c - 1)
    strict = s_lane < t_row
    incl = s_lane <= t_row

    def bd_ch(x):
        return _bd_rows(x, m_ch)

    def bd_tt(x):
        return _bd_rows(x, m_tt)

    def apply(a, x):
        return _mm(a, bd_ch(x))

    cl = _each(lambda x: x[c - 1:c, :], cum)
    at = _each(lambda a, x, l: a * jnp.exp(x - l), a_, cum, lw)
    rt = _each(lambda a, x: a * jnp.exp(x), r, cum)
    bt = _each(lambda a, x: a * jnp.exp(-x), b_, cum)
    kt = _each(lambda a, x: a * jnp.exp(-x), k, cum)
    bh = _each(lambda a, x, xl: a * jnp.exp(xl - x), b_, cum, cl)
    kh = _each(lambda a, x, xl: a * jnp.exp(xl - x), k, cum, cl)

    lhs = _each(lambda a, b: jnp.concatenate([a, b], axis=0), at, rt)
    ob = _each(lambda a, x: _mm(a, bd_ch(x), NT), lhs, bt)
    ok = _each(lambda a, x: _mm(a, bd_ch(x), NT), lhs, kt)
    a_ab = _each(lambda x: jnp.where(strict, x[:c], 0.0), ob)
    a_rb = _each(lambda x: jnp.where(incl, x[c:], 0.0), ob)
    a_ak = _each(lambda x: jnp.where(strict, x[:c], 0.0), ok)
    a_rk = _each(lambda x: jnp.where(incl, x[c:], 0.0), ok)
    akv = _each(lambda a, b, x: apply(jnp.concatenate([a, b], axis=0), x), a_ak, a_rk, v)
    av = _each(lambda x: x[:c], akv)
    y_kv = _each(lambda x: x[c:], akv)

    base = min(c, INV_BASE)
    eye = jnp.where(s_lane == t_row, 1.0, 0.0)
    in_base = jnp.bitwise_and(t_row, -base) == jnp.bitwise_and(s_lane, -base)
    tinv = _each(lambda x: eye + jnp.where(in_base, x, 0.0), a_ab)
    if base > 1:
        npow = _each(lambda x: _mm(x, bd_tt(x)), _each(lambda x: jnp.where(in_base, x, 0.0), a_ab))
        for _ in range(int(math.log2(base)) - 2):
            both = _each(lambda t, x: _mm(jnp.concatenate([t, x], axis=0), bd_tt(x)), tinv, npow)
            tinv = _each(lambda t, z: t + z[:c], tinv, both)
            npow = _each(lambda z: z[c:], both)
        if base > 2:
            tinv = _each(lambda t, x: t + _mm(t, bd_tt(x)), tinv, npow)
    size = base
    while size < c:
        lower_left = ((jnp.bitwise_and(t_row, -2 * size) == jnp.bitwise_and(s_lane, -2 * size))
                      & (jnp.bitwise_and(t_row, size) != 0) & (jnp.bitwise_and(s_lane, size) == 0))
        cross = _each(lambda x, t: _mm(jnp.where(lower_left, x, 0.0), bd_tt(t)), a_ab, tinv)
        tinv = _each(lambda t, x: t + _mm(t, bd_tt(x)), tinv, cross)
        size *= 2

    wt = _each(apply, tinv, at)
    w = _each(apply, tinv, av)
    qt = _each(lambda x, a, y: x + apply(a, y), rt, a_rb, wt)
    yi = _each(lambda a, x, y: apply(a, x) + y, a_rb, w, y_kv)
    g_off = _each(lambda x, y: jnp.where(m_ss, _mm(x, y, TN), 0.0), wt, bh)
    h_t = _each(lambda x, y, p, q: jnp.where(
        m_ss, _mm(jnp.concatenate([x, p], axis=0), jnp.concatenate([y, q], axis=0), TN), 0.0),
                w, bh, v, kh)

    y = _each(lambda q, s, x: _mm(q, s, NT) + x, qt, s_bd, yi)
    s_new = _each(lambda s, xl, g, h: s * jnp.exp(xl) + _mm(s, g) + h, s_bd, cl, g_off, h_t)
    return y, s_new


def _scan_kernel(*refs, c, d, nq, has_state, lane_packed):
    if has_state:
        (r_ref, k_ref, v_ref, w_ref, a_ref, g_ref, p_ref, s0_ref, y_ref, so_ref, sbd_ref) = refs
    else:
        (r_ref, k_ref, v_ref, w_ref, a_ref, g_ref, p_ref, y_ref, so_ref, sbd_ref) = refs
    ci = pl.program_id(1)
    hs = HEAD_SIZE
    ng = d // GROUP_LANES
    chains = [(q, g) for q in range(nq) for g in range(ng)]
    n_ch = len(chains)
    m_ss = _same_block((GROUP_LANES, GROUP_LANES), hs, hs)
    ones_bd = jnp.where(m_ss, 1.0, 0.0).astype(BF16)

    def lanes(g):
        return slice(g * GROUP_LANES, (g + 1) * GROUP_LANES)

    def rd(ref, q, g):
        if lane_packed:
            return ref[:, q * d + g * GROUP_LANES:q * d + (g + 1) * GROUP_LANES]
        return ref[q, :, lanes(g)]

    def par(row, g):
        return p_ref[row:row + 1, lanes(g)]

    @pl.when(ci == 0)
    def _():
        for i, (q, g) in enumerate(chains):
            if has_state:
                heads = [s0_ref[q, HEADS_PER_GROUP * g + h] for h in range(HEADS_PER_GROUP)]
                sbd_ref[i] = _bd_rows(jnp.concatenate(heads, axis=1), m_ss)
            else:
                sbd_ref[i] = jnp.zeros((GROUP_LANES, GROUP_LANES), F32)

    def log_decay(wq):
        z = -wq
        softplus = jnp.maximum(z, 0.0) + jnp.log(1.0 + jnp.exp(-jnp.abs(z)))
        lw = -jnp.exp(-softplus - 0.5)
        row = _iota2(lw.shape, 0)
        cum = lw
        step = 1
        while step < c:
            cum = cum + jnp.where(row >= step, pltpu.roll(cum, step, 0), 0.0)
            step *= 2
        return lw, cum

    if lane_packed:
        lw_all, cum_all = log_decay(w_ref[...])
        lw = [lw_all[:, q * d + g * GROUP_LANES:q * d + (g + 1) * GROUP_LANES] for q, g in chains]
        cum = [cum_all[:, q * d + g * GROUP_LANES:q * d + (g + 1) * GROUP_LANES] for q, g in chains]
    else:
        per_q = [log_decay(w_ref[q]) for q in range(nq)]
        lw = [per_q[q][0][:, lanes(g)] for q, g in chains]
        cum = [per_q[q][1][:, lanes(g)] for q, g in chains]

    def seg_sum(xs, split=True):
        x = jnp.concatenate(xs, axis=0)
        y = _split_mm(x, ones_bd) if split else _mm(x, ones_bd)
        return [y[i * c:(i + 1) * c] for i in range(len(xs))]

    r = [rd(r_ref, q, g) for q, g in chains]
    k = [rd(k_ref, q, g) for q, g in chains]
    v = [rd(v_ref, q, g) for q, g in chains]
    a_sig = [rd(a_ref, q, g) for q, g in chains]
    kk = [x * par(0, g) for x, (q, g) in zip(k, chains)]
    k = [x * (1.0 + (a - 1.0) * par(1, g)) for x, a, (q, g) in zip(k, a_sig, chains)]
    kk = _each(lambda x, n: x / jnp.maximum(jnp.sqrt(n), NORM_EPS), kk,
               seg_sum(_each(lambda x: x * x, kk)))
    bonus = _each(lambda s, z: s * z,
                  seg_sum([x * y * par(2, g) for x, y, (q, g) in zip(r, k, chains)], split=False),
                  v)
    y, s_new = _wkv_chunk(r, k, v, _each(lambda x: -x, kk), _each(lambda x, a: x * a, kk, a_sig),
                          lw, cum, [sbd_ref[i] for i in range(n_ch)], c)
    for i in range(n_ch):
        sbd_ref[i] = s_new[i]

    mu = _each(lambda x: x * (1.0 / hs), seg_sum(y))
    dy = _each(lambda x, m: x - m, y, mu)
    var = _each(lambda x: x * (1.0 / hs), seg_sum(_each(lambda x: x * x, dy), split=False))
    for i, (q, g) in enumerate(chains):
        yn = dy[i] * lax.rsqrt(var[i] + GN_EPS) * par(3, g) + par(4, g)
        out = ((yn + bonus[i]) * rd(g_ref, q, g)).astype(BF16)
        if lane_packed:
            y_ref[:, q * d + g * GROUP_LANES:q * d + (g + 1) * GROUP_LANES] = out
        else:
            y_ref[q, :, lanes(g)] = out

    @pl.when(ci == pl.num_programs(1) - 1)
    def _():
        for i, (q, g) in enumerate(chains):
            s = sbd_ref[i]
            s = s[0:hs] + s[hs:2 * hs] + s[2 * hs:3 * hs] + s[3 * hs:4 * hs]
            for h in range(HEADS_PER_GROUP):
                so_ref[q, HEADS_PER_GROUP * g + h] = s[:, h * hs:(h + 1) * hs]


def _scan_call(r, k, v, w, a, g, p, s0, nb, t, c, nq, time_major):
    m, d = r.shape
    nc = t // c
    ng = d // GROUP_LANES
    nh = d // HEAD_SIZE
    has_state = s0 is not None
    if time_major:
        r, k, v, w, a, g = (z.reshape(t, nb * d) for z in (r, k, v, w, a, g))
        xspec = pl.BlockSpec((c, nq * d), lambda b, ci: (ci, b))
    else:
        r, k, v, w, a, g = (z.reshape(nb, t, d) for z in (r, k, v, w, a, g))
        xspec = pl.BlockSpec((nq, c, d), lambda b, ci: (b, ci, 0))
    sspec = pl.BlockSpec((nq, nh, HEAD_SIZE, HEAD_SIZE), lambda b, ci: (b, 0, 0, 0))
    in_specs = [xspec] * 6 + [pl.BlockSpec(p.shape, lambda b, ci: (0, 0))]
    args = [r, k, v, w, a, g, p]
    if has_state:
        in_specs.append(sspec)
        args.append(s0)
    y, s_fin = pl.pallas_call(
        functools.partial(_scan_kernel, c=c, d=d, nq=nq, has_state=has_state,
                          lane_packed=time_major),
        grid=(nb // nq, nc),
        in_specs=in_specs,
        out_specs=[xspec, sspec],
        out_shape=[jax.ShapeDtypeStruct(r.shape, BF16),
                   jax.ShapeDtypeStruct((nb, nh, HEAD_SIZE, HEAD_SIZE), F32)],
        scratch_shapes=[pltpu.VMEM((nq * ng, GROUP_LANES, GROUP_LANES), F32)],
        compiler_params=_params(2),
    )(*args)
    return y.reshape(m, d), s_fin


class _Tiles(NamedTuple):
    ffn: int
    conv: int
    out: int
    prep: int
    chunk: int
    scan_seqs: int


def _tiles(time_major, nb, t):
    if time_major:
        return _Tiles(ffn=nb, conv=nb, out=min(nb, 32), prep=min(nb, 32), chunk=t,
                      scan_seqs=math.gcd(nb, 2))
    return _Tiles(ffn=min(t, 2048), conv=min(t, 2048), out=min(t, 512), prep=min(t, 256),
                  chunk=min(t, 64), scan_seqs=math.gcd(nb, 2))


def _trunk(x3, mod, conv_state, shift_state, wkv_state, wts, *, time_major):
    d = x3.shape[2]
    nb, t = (x3.shape[1], x3.shape[0]) if time_major else (x3.shape[0], x3.shape[1])
    depth = mod.shape[0]
    tiles = _tiles(time_major, nb, t)
    chunk, scan_seqs = tiles.chunk, tiles.scan_seqs
    ffn_lay = _Layout(x3.shape, time_major, tiles.ffn)
    lay = _Layout(x3.shape, time_major, tiles.conv)
    out_lay = _Layout(x3.shape, time_major, tiles.out)
    prep_lay = _Layout(x3.shape, time_major, tiles.prep)
    m = nb * t
    new_conv, new_shift, new_wkv = [], [], []

    subs = [(l, sub) for l in range(depth) for sub in range(3)]

    def takes_h(l, sub):
        return not (sub == 1 and l % 2 == 1)

    def nxt_of(l, sub):
        i = subs.index((l, sub)) + 1
        if i < len(subs) and takes_h(*subs[i]):
            return (wts["norm_pre"],) + subs[i]
        return None

    def ffn(x3, h3, l, s, sub):
        cached = wts["ffn_bf16"].get((l, s))
        if cached is None:
            act, cached = _ffn_in_call(h3, ffn_lay, 512, f32_weights=(
                wts["ffn_w_in"], wts["ffn_w_out"], l, s))
            wts["ffn_bf16"][(l, s)] = cached
        else:
            act, _ = _ffn_in_call(h3, ffn_lay, 512, bf16_weights=cached[:2])
        w_out = cached[2]
        return _out_proj_call(x3, out_lay, mod, wts["norm_post"], act, w_out[None, None],
                              l, (0, 0), sub, HALF_STEP, nxt_of(l, sub))

    h3 = _pre_call(x3, out_lay, mod, wts["norm_pre"], 0, 0)
    for l in range(depth):
        jl = l // 2
        x3, h3 = ffn(x3, h3, l, 0, 0)
        if l % 2 == 0:
            st = conv_state[jl].transpose(1, 0, 2) if time_major else None
            bz, cs = _conv_call(h3, lay, wts["conv_w_in"], wts["conv_w"], st, jl, 256)
            x3, h3 = _out_proj_call(x3, out_lay, mod, wts["norm_post"], bz, wts["conv_w_out"],
                                    l, (jl, 0), 1, 1.0, nxt_of(l, 1))
            if time_major:
                new_conv.append(cs.transpose(1, 0, 2))
            else:
                tps = lay.tiles_per_seq
                new_conv.append(cs[tps - 1::tps])
        else:
            st = shift_state[jl] if time_major else None
            xr, xk, xv, tw, ta, tg, ss = _rw_prep_call(
                x3, prep_lay, mod, wts["norm_pre"], wts["rw_mix"], wts["rw_w1"], wts["rw_a1"],
                wts["rw_g1"], st, l, jl, 1)
            flat = lambda z: z.reshape(m, z.shape[2])
            r, k, v, w, a, g = _rw_proj_call(
                flat(xr), flat(xk), flat(xv), flat(tw), flat(ta), flat(tg),
                wts["rw_wr"], wts["rw_wk"], wts["rw_wv"], wts["rw_w2"], wts["rw_a2"],
                wts["rw_g2"], wts["rw_w0"], wts["rw_a0"], jl,
                *((512, min(d, 512)) if m > 1024 else (m, min(d, 256))))
            p = jnp.concatenate([
                wts["rw_kk"][jl][None], wts["rw_ka"][jl][None], wts["rw_rk"][jl].reshape(1, d),
                wts["rw_lnw"][jl][None], wts["rw_lnb"][jl][None],
                jnp.zeros((3, d), F32)], axis=0)
            s0 = wkv_state[jl] if time_major else None
            yg, s_fin = _scan_call(r, k, v, w, a, g, p, s0, nb, t, chunk, scan_seqs, time_major)
            x3, h3 = _out_proj_call(x3, out_lay, mod, wts["norm_post"], yg.reshape(x3.shape),
                                    wts["rw_wo"], l, (jl, 0), 1, 1.0, nxt_of(l, 1))
            new_shift.append(ss.reshape(nb, d))
            new_wkv.append(s_fin)
        x3, h3 = ffn(x3, h3, l, 1, 2)
    def stack(xs):
        return xs[0][None] if len(xs) == 1 else jnp.stack(xs)

    return x3, stack(new_conv), stack(new_shift), stack(new_wkv)


def kernel(x_prompt, x_sample, state_conv, state_shift, state_wkv, c_prompt, c_sample, mod_w, mod_b, norm_pre, norm_post, ffn_w_in, ffn_w_out, conv_w_in, conv_w, conv_w_out, rw_mix, rw_w0, rw_w1, rw_w2, rw_a0, rw_a1, rw_a2, rw_g1, rw_g2, rw_kk, rw_ka, rw_rk, rw_wr, rw_wk, rw_wv, rw_wo, rw_lnw, rw_lnb):
    b, t, d = x_prompt.shape
    sb, st, _ = x_sample.shape
    depth = mod_w.shape[0]
    n_sub = norm_pre.shape[1]
    wts = dict(norm_pre=norm_pre.reshape(depth, n_sub, 1, d),
               norm_post=norm_post.reshape(depth, n_sub, 1, d), ffn_w_in=ffn_w_in,
               ffn_w_out=ffn_w_out, ffn_bf16={},
               conv_w_out=conv_w_out.astype(BF16)[:, None],
               rw_wo=rw_wo.astype(BF16)[:, None],
               conv_w_in=conv_w_in, conv_w=conv_w, rw_mix=rw_mix,
               rw_w0=rw_w0, rw_w1=rw_w1, rw_w2=rw_w2, rw_a0=rw_a0, rw_a1=rw_a1, rw_a2=rw_a2,
               rw_g1=rw_g1, rw_g2=rw_g2, rw_kk=rw_kk, rw_ka=rw_ka, rw_rk=rw_rk, rw_wr=rw_wr,
               rw_wk=rw_wk, rw_wv=rw_wv, rw_lnw=rw_lnw, rw_lnb=rw_lnb)

    n_c = b + sb
    pad = (-n_c) % 8
    c_all = jnp.concatenate([c_prompt, c_sample, jnp.zeros((pad, d), F32)], axis=0)
    mod_all = _mod_call(c_all, mod_w, mod_b)
    mod_p = (mod_all[:, :b].reshape(depth, b, N_MOD, d).transpose(0, 2, 1, 3)
             .reshape(depth, N_MOD, b, 1, d))
    mod_s = mod_all[:, b:n_c].reshape(depth, sb, N_MOD, d).transpose(0, 2, 1, 3)

    y_p, conv_p, shift_p, wkv_p = _trunk(
        x_prompt, mod_p, None, None, None, wts, time_major=False)
    y_s, conv_s, shift_s, wkv_s = _trunk(
        x_sample.transpose(1, 0, 2), mod_s, state_conv, state_shift, state_wkv, wts,
        time_major=True)
    return (y_p, y_s.transpose(1, 0, 2), conv_p, shift_p, wkv_p, conv_s, shift_s, wkv_s)
```

```python
import functools
import math
from typing import NamedTuple

import jax
import jax.numpy as jnp
from jax import lax
from jax.experimental import pallas as pl
from jax.experimental.pallas import tpu as pltpu

F32 = jnp.float32
BF16 = jnp.bfloat16

RMS_EPS = 1e-6
GN_EPS = 64e-5
NORM_EPS = 1e-12
HALF_STEP = 0.5
HEAD_SIZE = 64
N_MOD = 9

MXU_WIDTH_V7X = 256
GROUP_LANES = MXU_WIDTH_V7X
HEADS_PER_GROUP = GROUP_LANES // HEAD_SIZE
VMEM_LIMIT_V7X = 60 * 2**20
ROW_PIECE = 256
INV_BASE = 16

NN = (((1,), (0,)), ((), ()))
NT = (((1,), (1,)), ((), ()))
TN = (((0,), (0,)), ((), ()))


def _mm(a, b, dims=NN):
    return lax.dot_general(a.astype(BF16), b.astype(BF16), dims, preferred_element_type=F32)


def _sigmoid(x):
    return 1.0 / (1.0 + jnp.exp(-x))


def _params(n_axes):
    return pltpu.CompilerParams(dimension_semantics=("arbitrary",) * n_axes,
                                vmem_limit_bytes=VMEM_LIMIT_V7X)


def _rms(x, g):
    return x * lax.rsqrt(jnp.mean(x * x, axis=-1, keepdims=True) + RMS_EPS) * g


def _modulated_pre(x, m_ref, g):
    return _rms(x, g) * (1.0 + m_ref[1]) + m_ref[0]


def _gated_post(x, y, m_ref, g, res_w):
    return x + (res_w * m_ref[2]) * _rms(y, g)


def _pieces(lead, rows):
    if lead == 1:
        n = min(ROW_PIECE, rows)
        return [(slice(0, 1), slice(r0, r0 + n)) for r0 in range(0, rows, n)]
    n = min(lead, max(1, ROW_PIECE // rows))
    return [(slice(a0, a0 + n), slice(0, rows)) for a0 in range(0, lead, n)]


def _piece_rows(x_ref, ls, rs):
    shp = x_ref[ls, rs, :].shape
    return shp, shp[0] * shp[1]


def _shift_time(u, k, fill, time_major):
    lead, rows, n = u.shape
    if time_major:
        return jnp.concatenate([fill, u[:lead - k]], axis=0)
    out = pltpu.roll(u.reshape(rows, n), k, 0)
    row = lax.broadcasted_iota(jnp.int32, (rows, n), 0)
    for t in range(k):
        out = jnp.where(row == t, fill[t:t + 1, :], out)
    return out.reshape(1, rows, n)


class _Layout:
    def __init__(self, shape, time_major, rows_per_tile):
        self.time_major = time_major
        self.shape = shape
        lead, rows, d = shape
        if time_major:
            self.block = (lead, rows_per_tile, d)
            self.tiles_per_seq = 1
            self.n_tiles = rows // rows_per_tile
        else:
            self.block = (1, rows_per_tile, d)
            self.tiles_per_seq = rows // rows_per_tile
            self.n_tiles = lead * self.tiles_per_seq
        self.tile_rows = self.block[0] * self.block[1]

    def xmap(self, i, *_):
        if self.time_major:
            return (0, i, 0)
        return (i // self.tiles_per_seq, i % self.tiles_per_seq, 0)

    def x_spec(self, **kw):
        return pl.BlockSpec(self.block, self.xmap, **kw)

    def mod_spec(self, mod, l, sub):
        d = self.shape[2]
        if self.time_major:
            return pl.BlockSpec((None, 3, self.block[1], d), lambda i, *_: (l, sub, i, 0))
        tps = self.tiles_per_seq
        return pl.BlockSpec((None, 3, None, 1, d), lambda i, *_: (l, sub, i // tps, 0, 0))


def _gain_spec(gains, l, sub):
    return pl.BlockSpec((None, None, 1, gains.shape[3]), lambda i, *_: (l, sub, 0, 0))


def _mod_kernel(c_ref, w_ref, b_ref, o_ref):
    c = c_ref[...]
    o_ref[...] = _mm(c * _sigmoid(c), w_ref[...]) + b_ref[...]


def _mod_call(c_all, mod_w, mod_b):
    depth, d, n = mod_w.shape
    nbp = c_all.shape[0]
    tn = math.gcd(n, 1024)
    return pl.pallas_call(
        _mod_kernel,
        grid=(depth, n // tn),
        in_specs=[
            pl.BlockSpec((nbp, d), lambda l, j: (0, 0)),
            pl.BlockSpec((None, d, tn), lambda l, j: (l, 0, j)),
            pl.BlockSpec((None, 1, tn), lambda l, j: (l, 0, j)),
        ],
        out_specs=pl.BlockSpec((None, nbp, tn), lambda l, j: (l, 0, j)),
        out_shape=jax.ShapeDtypeStruct((depth, nbp, n), F32),
        compiler_params=_params(2),
    )(c_all, mod_w, mod_b.reshape(depth, 1, n))


def _out_proj_kernel(*refs, res_w, emit_next):
    if emit_next:
        x_ref, m_ref, gpost_ref, a_ref, w_ref, mn_ref, gn_ref, o_ref, h_ref = refs
    else:
        x_ref, m_ref, gpost_ref, a_ref, w_ref, o_ref = refs
    lead, rows, _ = x_ref.shape
    for ls, rs in _pieces(lead, rows):
        shp, n = _piece_rows(x_ref, ls, rs)
        a = a_ref[ls, rs, :]
        y = lax.dot_general(a.reshape(n, a.shape[2]), w_ref[...], NN,
                            preferred_element_type=F32).reshape(shp)
        o = _gated_post(x_ref[ls, rs, :], y, m_ref, gpost_ref[...], res_w)
        o_ref[ls, rs, :] = o
        if emit_next:
            h_ref[ls, rs, :] = _modulated_pre(o, mn_ref, gn_ref[...]).astype(BF16)


def _out_proj_call(x3, lay, mod, norm_post, act, w, l, widx, sub, res_w, nxt=None):
    k, d = w.shape[2], w.shape[3]
    in_specs = [
        lay.x_spec(),
        lay.mod_spec(mod, l, sub),
        _gain_spec(norm_post, l, sub),
        pl.BlockSpec((lay.block[0], lay.block[1], k), lay.xmap),
        pl.BlockSpec((None, None, k, d), lambda i: (widx[0], widx[1], 0, 0),
                     pipeline_mode=pl.Buffered(1)),
    ]
    args = [x3, mod, norm_post, act, w]
    out_specs = [lay.x_spec()]
    out_shape = [jax.ShapeDtypeStruct(x3.shape, F32)]
    if nxt is not None:
        norm_pre, ln, subn = nxt
        in_specs += [lay.mod_spec(mod, ln, subn), _gain_spec(norm_pre, ln, subn)]
        args += [mod, norm_pre]
        out_specs.append(lay.x_spec())
        out_shape.append(jax.ShapeDtypeStruct(x3.shape, BF16))
    out = pl.pallas_call(
        functools.partial(_out_proj_kernel, res_w=res_w, emit_next=nxt is not None),
        grid=(lay.n_tiles,),
        in_specs=in_specs,
        out_specs=out_specs,
        out_shape=out_shape,
        compiler_params=_params(1),
    )(*args)
    return (out[0], out[1]) if nxt is not None else (out[0], None)


def _pre_kernel(x_ref, m_ref, g_ref, h_ref):
    lead, rows, _ = x_ref.shape
    for ls, rs in _pieces(lead, rows):
        h_ref[ls, rs, :] = _modulated_pre(x_ref[ls, rs, :], m_ref, g_ref[...]).astype(BF16)


def _pre_call(x3, lay, mod, norm_pre, l, sub):
    return pl.pallas_call(
        _pre_kernel,
        grid=(lay.n_tiles,),
        in_specs=[lay.x_spec(), lay.mod_spec(mod, l, sub), _gain_spec(norm_pre, l, sub)],
        out_specs=lay.x_spec(),
        out_shape=jax.ShapeDtypeStruct(x3.shape, BF16),
        compiler_params=_params(1),
    )(x3, mod, norm_pre)


def _cast_weights_once(i, pairs):
    @pl.when(i == 0)
    def _():
        for src, dst in pairs:
            dst[...] = src[...].astype(BF16)


def _ffn_in_kernel(*refs, first_use, n_row_tiles):
    if first_use:
        h_ref, wg_ref, wu_ref, wo_ref, a_ref, wgb_ref, wub_ref, wob_ref = refs
        _cast_weights_once(pl.program_id(1), [(wg_ref, wgb_ref), (wu_ref, wub_ref)])
        share = wo_ref.shape[0] // n_row_tiles
        rows_i = pl.ds(pl.multiple_of(pl.program_id(1) * share, share), share)
        wob_ref[rows_i, :] = wo_ref[rows_i, :].astype(BF16)
    else:
        h_ref, wgb_ref, wub_ref, a_ref = refs
    lead, rows, d = h_ref.shape
    for ls, rs in _pieces(lead, rows):
        shp, n = _piece_rows(h_ref, ls, rs)
        h = h_ref[ls, rs, :].reshape(n, d)
        gt = lax.dot_general(h, wgb_ref[...], NN, preferred_element_type=F32)
        up = lax.dot_general(h, wub_ref[...], NN, preferred_element_type=F32)
        a_ref[ls, rs, :] = (gt * _sigmoid(gt) * up).astype(BF16).reshape(shp[0], shp[1], -1)


def _ffn_in_call(h3, lay, tf, f32_weights=None, bf16_weights=None):
    lead, rows, d = h3.shape
    first_use = f32_weights is not None
    h_spec = pl.BlockSpec(lay.block, lambda j, i: lay.xmap(i))
    blocked_spec = pl.BlockSpec((None, d, tf), lambda j, i: (j, 0, 0))
    if first_use:
        w_in, w_out, l, s = f32_weights
        f = w_out.shape[2]
        nj = f // tf
        assert tf % lay.n_tiles == 0
        in_specs = [h_spec,
                    pl.BlockSpec((None, None, d, tf), lambda j, i: (l, s, 0, j)),
                    pl.BlockSpec((None, None, d, tf), lambda j, i: (l, s, 0, nj + j)),
                    pl.BlockSpec((None, None, tf, d), lambda j, i: (l, s, j, 0))]
        args = [h3, w_in, w_in, w_out]
        extra_specs = [blocked_spec, blocked_spec, pl.BlockSpec((tf, d), lambda j, i: (j, 0))]
        extra_shape = [jax.ShapeDtypeStruct((nj, d, tf), BF16)] * 2 + [
            jax.ShapeDtypeStruct((f, d), BF16)]
    else:
        nj = bf16_weights[0].shape[0]
        f = nj * tf
        in_specs = [h_spec, blocked_spec, blocked_spec]
        args = [h3, *bf16_weights]
        extra_specs, extra_shape = [], []
    out = pl.pallas_call(
        functools.partial(_ffn_in_kernel, first_use=first_use, n_row_tiles=lay.n_tiles),
        grid=(nj, lay.n_tiles),
        in_specs=in_specs,
        out_specs=[pl.BlockSpec((lay.block[0], lay.block[1], tf),
                                lambda j, i: lay.xmap(i)[:2] + (j,))] + extra_specs,
        out_shape=[jax.ShapeDtypeStruct((lead, rows, f), BF16)] + extra_shape,
        compiler_params=_params(2),
    )(*args)
    return out[0], tuple(out[1:])


def _conv_kernel(*refs, time_major, tiles_per_seq):
    if time_major:
        (h_ref, wb_ref, wc_ref, wx_ref, cw_ref, st_ref,
         o_ref, so_ref, wbb_ref, wcb_ref, wxb_ref) = refs
    else:
        (h_ref, wb_ref, wc_ref, wx_ref, cw_ref,
         o_ref, so_ref, wbb_ref, wcb_ref, wxb_ref, carry_ref) = refs
    i = pl.program_id(1)
    _cast_weights_once(i, [(wb_ref, wbb_ref), (wc_ref, wcb_ref), (wx_ref, wxb_ref)])
    lead, rows, d = h_ref.shape
    tm = lead * rows

    h = h_ref[...].reshape(tm, d)

    def proj(w_ref):
        return lax.dot_general(h, w_ref[...], NN, preferred_element_type=F32)

    bg = proj(wbb_ref)
    u2 = proj(wcb_ref) * proj(wxb_ref)
    tn = u2.shape[1]
    u = u2.reshape(lead, rows, tn)
    if time_major:
        st = st_ref[...]
        fill1, fill2 = st[1:2], st
        so_ref[...] = u[lead - 2:lead]
    else:
        @pl.when(i % tiles_per_seq == 0)
        def _():
            carry_ref[...] = jnp.zeros((8, tn), F32)
        prev = carry_ref[...]
        fill1, fill2 = prev[7:8, :], prev[6:8, :]
        carry_ref[...] = u2[tm - 8:tm, :]
        so_ref[...] = u[:, rows - 2:rows, :]
    cw = cw_ref[...]
    z = (cw[0:1, :] * _shift_time(u, 2, fill2, time_major)
         + cw[1:2, :] * _shift_time(u, 1, fill1, time_major) + cw[2:3, :] * u)
    o_ref[...] = (bg.reshape(lead, rows, tn) * z).astype(BF16)


def _conv_call(h3, lay, w_in, cw, state, jl, tn):
    lead_n, rows_n, d = h3.shape
    dc = cw.shape[2]
    nj = dc // tn
    in_specs = [
        pl.BlockSpec(lay.block, lambda j, i: lay.xmap(i)),
        pl.BlockSpec((None, d, tn), lambda j, i: (jl, 0, j)),
        pl.BlockSpec((None, d, tn), lambda j, i: (jl, 0, nj + j)),
        pl.BlockSpec((None, d, tn), lambda j, i: (jl, 0, 2 * nj + j)),
        pl.BlockSpec((None, cw.shape[1], tn), lambda j, i: (jl, 0, j)),
    ]
    args = [h3, w_in, w_in, w_in, cw]
    scratch = [pltpu.VMEM((d, tn), BF16)] * 3
    if lay.time_major:
        nb = lay.block[1]
        in_specs.append(pl.BlockSpec((2, nb, tn), lambda j, i: (0, i, j)))
        args.append(state)
        so_spec = pl.BlockSpec((2, nb, tn), lambda j, i: (0, i, j))
        so_shape = jax.ShapeDtypeStruct((2, h3.shape[1], dc), F32)
    else:
        scratch.append(pltpu.VMEM((8, tn), F32))
        so_spec = pl.BlockSpec((1, 2, tn), lambda j, i: (i, 0, j))
        so_shape = jax.ShapeDtypeStruct((lay.n_tiles, 2, dc), F32)
    return pl.pallas_call(
        functools.partial(_conv_kernel, time_major=lay.time_major,
                          tiles_per_seq=lay.tiles_per_seq),
        grid=(nj, lay.n_tiles),
        in_specs=in_specs,
        out_specs=[pl.BlockSpec((lay.block[0], lay.block[1], tn),
                                lambda j, i: lay.xmap(i)[:2] + (j,)), so_spec],
        out_shape=[jax.ShapeDtypeStruct((lead_n, rows_n, dc), BF16), so_shape],
        scratch_shapes=scratch,
        compiler_params=_params(2),
    )(*args)


def _rw_prep_kernel(*refs, time_major, tiles_per_seq):
    if time_major:
        (x_ref, m_ref, gpre_ref, mix_ref, w1_ref, a1_ref, g1_ref, st_ref,
         xr_ref, xk_ref, xv_ref, tw_ref, ta_ref, tg_ref, so_ref) = refs
    else:
        (x_ref, m_ref, gpre_ref, mix_ref, w1_ref, a1_ref, g1_ref,
         xr_ref, xk_ref, xv_ref, tw_ref, ta_ref, tg_ref, so_ref, carry_ref) = refs
    i = pl.program_id(0)
    lead, rows, d = x_ref.shape
    n = lead * rows
    h = _modulated_pre(x_ref[...], m_ref, gpre_ref[...])
    if time_major:
        fill = st_ref[...][None]
        so_ref[...] = h[lead - 1]
    else:
        @pl.when(i % tiles_per_seq == 0)
        def _():
            carry_ref[...] = jnp.zeros((8, d), F32)
        fill = carry_ref[7:8, :]
        carry_ref[...] = h[0, rows - 8:rows, :]
        so_ref[...] = h[:, rows - 1:rows, :]
    xx = _shift_time(h, 1, fill, time_major) - h
    mix = mix_ref[...]

    def mixed(k):
        return h + xx * mix[k:k + 1, :]

    def low_rank(k, w_ref):
        return _mm(mixed(k).reshape(n, d), w_ref[...]).reshape(lead, rows, w_ref.shape[1])

    xr_ref[...] = mixed(0).astype(BF16)
    tw_ref[...] = jnp.tanh(low_rank(1, w1_ref))
    xk_ref[...] = mixed(2).astype(BF16)
    xv_ref[...] = mixed(3).astype(BF16)
    ta_ref[...] = low_rank(4, a1_ref)
    tg_ref[...] = _sigmoid(low_rank(5, g1_ref))


def _rw_prep_call(x3, lay, mod, norm_pre, mix, w1, a1, g1, state, l, jl, sub):
    lead, rows, d = x3.shape
    dl, dg = w1.shape[2], g1.shape[2]
    in_specs = [
        lay.x_spec(),
        lay.mod_spec(mod, l, sub),
        _gain_spec(norm_pre, l, sub),
        pl.BlockSpec((None, mix.shape[1], d), lambda i: (jl, 0, 0)),
        pl.BlockSpec((None, d, dl), lambda i: (jl, 0, 0)),
        pl.BlockSpec((None, d, dl), lambda i: (jl, 0, 0)),
        pl.BlockSpec((None, d, dg), lambda i: (jl, 0, 0)),
    ]
    args = [x3, mod, norm_pre, mix, w1, a1, g1]
    scratch = []
    blk = lay.block
    if lay.time_major:
        in_specs.append(pl.BlockSpec((blk[1], d), lambda i: (i, 0)))
        args.append(state)
        so_spec = pl.BlockSpec((blk[1], d), lambda i: (i, 0))
        so_shape = jax.ShapeDtypeStruct((rows, d), F32)
    else:
        scratch.append(pltpu.VMEM((8, d), F32))
        tps = lay.tiles_per_seq
        so_spec = pl.BlockSpec((1, 1, d), lambda i: (i // tps, 0, 0))
        so_shape = jax.ShapeDtypeStruct((lead, 1, d), F32)

    def ospec(width):
        return pl.BlockSpec((blk[0], blk[1], width), lay.xmap)

    return pl.pallas_call(
        functools.partial(_rw_prep_kernel, time_major=lay.time_major,
                          tiles_per_seq=lay.tiles_per_seq),
        grid=(lay.n_tiles,),
        in_specs=in_specs,
        out_specs=[ospec(d)] * 3 + [ospec(dl), ospec(dl), ospec(dg), so_spec],
        out_shape=[jax.ShapeDtypeStruct((lead, rows, d), BF16)] * 3
        + [jax.ShapeDtypeStruct((lead, rows, dl), F32)] * 2
        + [jax.ShapeDtypeStruct((lead, rows, dg), F32), so_shape],
        scratch_shapes=scratch,
        compiler_params=_params(1),
    )(*args)


def _rw_proj_kernel(xr_ref, xk_ref, xv_ref, tw_ref, ta_ref, tg_ref, wr_ref, wk_ref, wv_ref,
                    w2_ref, a2_ref, g2_ref, w0_ref, a0_ref,
                    r_ref, k_ref, v_ref, w_ref, a_ref, g_ref, wrb_ref, wkb_ref, wvb_ref,
                    *, cols_outer):
    pairs = [(wr_ref, wrb_ref), (wk_ref, wkb_ref), (wv_ref, wvb_ref)]
    if cols_outer:
        _cast_weights_once(pl.program_id(1), pairs)
    else:
        for src, dst in pairs:
            dst[...] = src[...].astype(BF16)

    def proj(x_ref, wb_ref):
        return lax.dot_general(x_ref[...], wb_ref[...], NN, preferred_element_type=F32)

    r_ref[...] = proj(xr_ref, wrb_ref)
    k_ref[...] = proj(xk_ref, wkb_ref)
    v_ref[...] = proj(xv_ref, wvb_ref)
    w_ref[...] = w0_ref[...] + _mm(tw_ref[...], w2_ref[...])
    a_ref[...] = _sigmoid(a0_ref[...] + _mm(ta_ref[...], a2_ref[...]))
    g_ref[...] = _mm(tg_ref[...], g2_ref[...])


def _rw_proj_call(xr, xk, xv, tw, ta, tg, wr, wk, wv, w2, a2, g2, w0, a0, jl, tm, tn):
    m, d = xr.shape
    dl, dg = tw.shape[1], tg.shape[1]
    nr = w0.shape[0]
    cols_outer = m // tm > 2
    grid = (d // tn, m // tm) if cols_outer else (m // tm, d // tn)

    def ij(f):
        return (lambda j, i: f(i, j)) if cols_outer else f

    xspec = pl.BlockSpec((tm, d), ij(lambda i, j: (i, 0)))
    wspec = pl.BlockSpec((None, d, tn), ij(lambda i, j: (jl, 0, j)))
    vspec = pl.BlockSpec((None, 1, tn), ij(lambda i, j: (jl, 0, j)))
    ospec = pl.BlockSpec((tm, tn), ij(lambda i, j: (i, j)))
    return pl.pallas_call(
        functools.partial(_rw_proj_kernel, cols_outer=cols_outer),
        grid=grid,
        in_specs=[xspec, xspec, xspec,
                  pl.BlockSpec((tm, dl), ij(lambda i, j: (i, 0))),
                  pl.BlockSpec((tm, dl), ij(lambda i, j: (i, 0))),
                  pl.BlockSpec((tm, dg), ij(lambda i, j: (i, 0))),
                  wspec, wspec, wspec,
                  pl.BlockSpec((None, dl, tn), ij(lambda i, j: (jl, 0, j))),
                  pl.BlockSpec((None, dl, tn), ij(lambda i, j: (jl, 0, j))),
                  pl.BlockSpec((None, dg, tn), ij(lambda i, j: (jl, 0, j))),
                  vspec, vspec],
        out_specs=[ospec] * 6,
        out_shape=[jax.ShapeDtypeStruct((m, d), F32)] * 6,
        scratch_shapes=[pltpu.VMEM((d, tn), BF16)] * 3,
        compiler_params=_params(2),
    )(xr, xk, xv, tw, ta, tg, wr, wk, wv, w2, a2, g2,
      w0.reshape(nr, 1, d), a0.reshape(nr, 1, d))


def _iota2(shape, axis):
    return lax.broadcasted_iota(jnp.int32, shape, axis)


def _same_block(shape, row_block, lane_block):
    r = lax.shift_right_logical(_iota2(shape, 0), int(math.log2(row_block)))
    c = lax.shift_right_logical(_iota2(shape, 1), int(math.log2(lane_block)))
    return r == c


def _bd_rows(x, mask):
    return jnp.where(mask, jnp.concatenate([x] * HEADS_PER_GROUP, axis=0), 0.0)


def _split_mm(x, ones):
    hi = x.astype(BF16)
    lo = (x - hi.astype(F32)).astype(BF16)
    return (lax.dot_general(hi, ones, NN, preferred_element_type=F32)
            + lax.dot_general(lo, ones, NN, preferred_element_type=F32))


def _each(fn, *lists):
    return [fn(*xs) for xs in zip(*lists)]


def _wkv_chunk(r, k, v, a_, b_, lw, cum, s_bd, c):
    hs = HEAD_SIZE
    tlanes = HEADS_PER_GROUP * c
    m_ch = _same_block((tlanes, GROUP_LANES), c, hs)
    m_tt = _same_block((tlanes, tlanes), c, c)
    m_ss = _same_block((GROUP_LANES, GROUP_LANES), hs, hs)
    t_row = _iota2((c, tlanes), 0)
    s_lane = jnp.bitwise_and(_iota2((c, tlanes), 1), c - 1)
    strict = s_lane < t_row
    incl = s_lane <= t_row

    def bd_ch(x):
        return _bd_rows(x, m_ch)

    def bd_tt(x):
        return _bd_rows(x, m_tt)

    def apply(a, x):
        return _mm(a, bd_ch(x))

    cl = _each(lambda x: x[c - 1:c, :], cum)
    at = _each(lambda a, x, l: a * jnp.exp(x - l), a_, cum, lw)
    rt = _each(lambda a, x: a * jnp.exp(x), r, cum)
    bt = _each(lambda a, x: a * jnp.exp(-x), b_, cum)
    kt = _each(lambda a, x: a * jnp.exp(-x), k, cum)
    bh = _each(lambda a, x, xl: a * jnp.exp(xl - x), b_, cum, cl)
    kh = _each(lambda a, x, xl: a * jnp.exp(xl - x), k, cum, cl)

    lhs = _each(lambda a, b: jnp.concatenate([a, b], axis=0), at, rt)
    ob = _each(lambda a, x: _mm(a, bd_ch(x), NT), lhs, bt)
    ok = _each(lambda a, x: _mm(a, bd_ch(x), NT), lhs, kt)
    a_ab = _each(lambda x: jnp.where(strict, x[:c], 0.0), ob)
    a_rb = _each(lambda x: jnp.where(incl, x[c:], 0.0), ob)
    a_ak = _each(lambda x: jnp.where(strict, x[:c], 0.0), ok)
    a_rk = _each(lambda x: jnp.where(incl, x[c:], 0.0), ok)
    akv = _each(lambda a, b, x: apply(jnp.concatenate([a, b], axis=0), x), a_ak, a_rk, v)
    av = _each(lambda x: x[:c], akv)
    y_kv = _each(lambda x: x[c:], akv)

    base = min(c, INV_BASE)
    eye = jnp.where(s_lane == t_row, 1.0, 0.0)
    in_base = jnp.bitwise_and(t_row, -base) == jnp.bitwise_and(s_lane, -base)
    tinv = _each(lambda x: eye + jnp.where(in_base, x, 0.0), a_ab)
    if base > 1:
        npow = _each(lambda x: _mm(x, bd_tt(x)), _each(lambda x: jnp.where(in_base, x, 0.0), a_ab))
        for _ in range(int(math.log2(base)) - 2):
            both = _each(lambda t, x: _mm(jnp.concatenate([t, x], axis=0), bd_tt(x)), tinv, npow)
            tinv = _each(lambda t, z: t + z[:c], tinv, both)
            npow = _each(lambda z: z[c:], both)
        if base > 2:
            tinv = _each(lambda t, x: t + _mm(t, bd_tt(x)), tinv, npow)
    size = base
    while size < c:
        lower_left = ((jnp.bitwise_and(t_row, -2 * size) == jnp.bitwise_and(s_lane, -2 * size))
                      & (jnp.bitwise_and(t_row, size) != 0) & (jnp.bitwise_and(s_lane, size) == 0))
        cross = _each(lambda x, t: _mm(jnp.where(lower_left, x, 0.0), bd_tt(t)), a_ab, tinv)
        tinv = _each(lambda t, x: t + _mm(t, bd_tt(x)), tinv, cross)
        size *= 2

    wt = _each(apply, tinv, at)
    w = _each(apply, tinv, av)
    qt = _each(lambda x, a, y: x + apply(a, y), rt, a_rb, wt)
    yi = _each(lambda a, x, y: apply(a, x) + y, a_rb, w, y_kv)
    g_off = _each(lambda x, y: jnp.where(m_ss, _mm(x, y, TN), 0.0), wt, bh)
    h_t = _each(lambda x, y, p, q: jnp.where(
        m_ss, _mm(jnp.concatenate([x, p], axis=0), jnp.concatenate([y, q], axis=0), TN), 0.0),
                w, bh, v, kh)

    y = _each(lambda q, s, x: _mm(q, s, NT) + x, qt, s_bd, yi)
    s_new = _each(lambda s, xl, g, h: s * jnp.exp(xl) + _mm(s, g) + h, s_bd, cl, g_off, h_t)
    return y, s_new


def _scan_kernel(*refs, c, d, nq, has_state, lane_packed):
    if has_state:
        (r_ref, k_ref, v_ref, w_ref, a_ref, g_ref, p_ref, s0_ref, y_ref, so_ref, sbd_ref) = refs
    else:
        (r_ref, k_ref, v_ref, w_ref, a_ref, g_ref, p_ref, y_ref, so_ref, sbd_ref) = refs
    ci = pl.program_id(1)
    hs = HEAD_SIZE
    ng = d // GROUP_LANES
    chains = [(q, g) for q in range(nq) for g in range(ng)]
    n_ch = len(chains)
    m_ss = _same_block((GROUP_LANES, GROUP_LANES), hs, hs)
    ones_bd = jnp.where(m_ss, 1.0, 0.0).astype(BF16)

    def lanes(g):
        return slice(g * GROUP_LANES, (g + 1) * GROUP_LANES)

    def rd(ref, q, g):
        if lane_packed:
            return ref[:, q * d + g * GROUP_LANES:q * d + (g + 1) * GROUP_LANES]
        return ref[q, :, lanes(g)]

    def par(row, g):
        return p_ref[row:row + 1, lanes(g)]

    @pl.when(ci == 0)
    def _():
        for i, (q, g) in enumerate(chains):
            if has_state:
                heads = [s0_ref[q, HEADS_PER_GROUP * g + h] for h in range(HEADS_PER_GROUP)]
                sbd_ref[i] = _bd_rows(jnp.concatenate(heads, axis=1), m_ss)
            else:
                sbd_ref[i] = jnp.zeros((GROUP_LANES, GROUP_LANES), F32)

    def log_decay(wq):
        z = -wq
        softplus = jnp.maximum(z, 0.0) + jnp.log(1.0 + jnp.exp(-jnp.abs(z)))
        lw = -jnp.exp(-softplus - 0.5)
        row = _iota2(lw.shape, 0)
        cum = lw
        step = 1
        while step < c:
            cum = cum + jnp.where(row >= step, pltpu.roll(cum, step, 0), 0.0)
            step *= 2
        return lw, cum

    if lane_packed:
        lw_all, cum_all = log_decay(w_ref[...])
        lw = [lw_all[:, q * d + g * GROUP_LANES:q * d + (g + 1) * GROUP_LANES] for q, g in chains]
        cum = [cum_all[:, q * d + g * GROUP_LANES:q * d + (g + 1) * GROUP_LANES] for q, g in chains]
    else:
        per_q = [log_decay(w_ref[q]) for q in range(nq)]
        lw = [per_q[q][0][:, lanes(g)] for q, g in chains]
        cum = [per_q[q][1][:, lanes(g)] for q, g in chains]

    def seg_sum(xs, split=True):
        x = jnp.concatenate(xs, axis=0)
        y = _split_mm(x, ones_bd) if split else _mm(x, ones_bd)
        return [y[i * c:(i + 1) * c] for i in range(len(xs))]

    r = [rd(r_ref, q, g) for q, g in chains]
    k = [rd(k_ref, q, g) for q, g in chains]
    v = [rd(v_ref, q, g) for q, g in chains]
    a_sig = [rd(a_ref, q, g) for q, g in chains]
    kk = [x * par(0, g) for x, (q, g) in zip(k, chains)]
    k = [x * (1.0 + (a - 1.0) * par(1, g)) for x, a, (q, g) in zip(k, a_sig, chains)]
    kk = _each(lambda x, n: x / jnp.maximum(jnp.sqrt(n), NORM_EPS), kk,
               seg_sum(_each(lambda x: x * x, kk)))
    bonus = _each(lambda s, z: s * z,
                  seg_sum([x * y * par(2, g) for x, y, (q, g) in zip(r, k, chains)], split=False),
                  v)
    y, s_new = _wkv_chunk(r, k, v, _each(lambda x: -x, kk), _each(lambda x, a: x * a, kk, a_sig),
                          lw, cum, [sbd_ref[i] for i in range(n_ch)], c)
    for i in range(n_ch):
        sbd_ref[i] = s_new[i]

    mu = _each(lambda x: x * (1.0 / hs), seg_sum(y))
    dy = _each(lambda x, m: x - m, y, mu)
    var = _each(lambda x: x * (1.0 / hs), seg_sum(_each(lambda x: x * x, dy), split=False))
    for i, (q, g) in enumerate(chains):
        yn = dy[i] * lax.rsqrt(var[i] + GN_EPS) * par(3, g) + par(4, g)
        out = ((yn + bonus[i]) * rd(g_ref, q, g)).astype(BF16)
        if lane_packed:
            y_ref[:, q * d + g * GROUP_LANES:q * d + (g + 1) * GROUP_LANES] = out
        else:
            y_ref[q, :, lanes(g)] = out

    @pl.when(ci == pl.num_programs(1) - 1)
    def _():
        for i, (q, g) in enumerate(chains):
            s = sbd_ref[i]
            s = s[0:hs] + s[hs:2 * hs] + s[2 * hs:3 * hs] + s[3 * hs:4 * hs]
            for h in range(HEADS_PER_GROUP):
                so_ref[q, HEADS_PER_GROUP * g + h] = s[:, h * hs:(h + 1) * hs]


def _scan_call(r, k, v, w, a, g, p, s0, nb, t, c, nq, time_major):
    m, d = r.shape
    nc = t // c
    ng = d // GROUP_LANES
    nh = d // HEAD_SIZE
    has_state = s0 is not None
    if time_major:
        r, k, v, w, a, g = (z.reshape(t, nb * d) for z in (r, k, v, w, a, g))
        xspec = pl.BlockSpec((c, nq * d), lambda b, ci: (ci, b))
    else:
        r, k, v, w, a, g = (z.reshape(nb, t, d) for z in (r, k, v, w, a, g))
        xspec = pl.BlockSpec((nq, c, d), lambda b, ci: (b, ci, 0))
    sspec = pl.BlockSpec((nq, nh, HEAD_SIZE, HEAD_SIZE), lambda b, ci: (b, 0, 0, 0))
    in_specs = [xspec] * 6 + [pl.BlockSpec(p.shape, lambda b, ci: (0, 0))]
    args = [r, k, v, w, a, g, p]
    if has_state:
        in_specs.append(sspec)
        args.append(s0)
    y, s_fin = pl.pallas_call(
        functools.partial(_scan_kernel, c=c, d=d, nq=nq, has_state=has_state,
                          lane_packed=time_major),
        grid=(nb // nq, nc),
        in_specs=in_specs,
        out_specs=[xspec, sspec],
        out_shape=[jax.ShapeDtypeStruct(r.shape, BF16),
                   jax.ShapeDtypeStruct((nb, nh, HEAD_SIZE, HEAD_SIZE), F32)],
        scratch_shapes=[pltpu.VMEM((nq * ng, GROUP_LANES, GROUP_LANES), F32)],
        compiler_params=_params(2),
    )(*args)
    return y.reshape(m, d), s_fin


class _Tiles(NamedTuple):
    ffn: int
    ffn_cols: int
    conv: int
    conv_cols: int
    out: int
    prep: int
    proj: tuple
    chunk: int
    scan_seqs: int


def _tiles(time_major, nb, t, d):
    if time_major:
        return _Tiles(ffn=nb, ffn_cols=512, conv=nb, conv_cols=min(d, 256), out=min(nb, 32),
                      prep=min(nb, 32), proj=(nb * t, min(d, 256)), chunk=t,
                      scan_seqs=math.gcd(nb, 4))
    return _Tiles(ffn=min(t, 2048), ffn_cols=512, conv=min(t, 2048), conv_cols=min(d, 256),
                  out=min(t, 512), prep=min(t, 256), proj=(min(nb * t, 512), min(d, 512)),
                  chunk=min(t, 64), scan_seqs=math.gcd(nb, 2))


def _trunk(x3, mod, conv_state, shift_state, wkv_state, wts, *, time_major):
    d = x3.shape[2]
    nb, t = (x3.shape[1], x3.shape[0]) if time_major else (x3.shape[0], x3.shape[1])
    depth = mod.shape[0]
    tiles = _tiles(time_major, nb, t, d)
    chunk, scan_seqs = tiles.chunk, tiles.scan_seqs
    ffn_lay = _Layout(x3.shape, time_major, tiles.ffn)
    lay = _Layout(x3.shape, time_major, tiles.conv)
    out_lay = _Layout(x3.shape, time_major, tiles.out)
    prep_lay = _Layout(x3.shape, time_major, tiles.prep)
    m = nb * t
    new_conv, new_shift, new_wkv = [], [], []

    subs = [(l, sub) for l in range(depth) for sub in range(3)]

    def takes_h(l, sub):
        return not (sub == 1 and l % 2 == 1)

    def nxt_of(l, sub):
        i = subs.index((l, sub)) + 1
        if i < len(subs) and takes_h(*subs[i]):
            return (wts["norm_pre"],) + subs[i]
        return None

    def ffn(x3, h3, l, s, sub):
        cached = wts["ffn_bf16"].get((l, s))
        if cached is None:
            act, cached = _ffn_in_call(h3, ffn_lay, tiles.ffn_cols, f32_weights=(
                wts["ffn_w_in"], wts["ffn_w_out"], l, s))
            wts["ffn_bf16"][(l, s)] = cached
        else:
            act, _ = _ffn_in_call(h3, ffn_lay, tiles.ffn_cols, bf16_weights=cached[:2])
        w_out = cached[2]
        return _out_proj_call(x3, out_lay, mod, wts["norm_post"], act, w_out[None, None],
                              l, (0, 0), sub, HALF_STEP, nxt_of(l, sub))

    h3 = _pre_call(x3, out_lay, mod, wts["norm_pre"], 0, 0)
    for l in range(depth):
        jl = l // 2
        x3, h3 = ffn(x3, h3, l, 0, 0)
        if l % 2 == 0:
            st = conv_state[jl].transpose(1, 0, 2) if time_major else None
            bz, cs = _conv_call(h3, lay, wts["conv_w_in"], wts["conv_w"], st, jl,
                                tiles.conv_cols)
            x3, h3 = _out_proj_call(x3, out_lay, mod, wts["norm_post"], bz, wts["conv_w_out"],
                                    l, (jl, 0), 1, 1.0, nxt_of(l, 1))
            if time_major:
                new_conv.append(cs.transpose(1, 0, 2))
            else:
                tps = lay.tiles_per_seq
                new_conv.append(cs[tps - 1::tps])
        else:
            st = shift_state[jl] if time_major else None
            xr, xk, xv, tw, ta, tg, ss = _rw_prep_call(
                x3, prep_lay, mod, wts["norm_pre"], wts["rw_mix"], wts["rw_w1"], wts["rw_a1"],
                wts["rw_g1"], st, l, jl, 1)
            flat = lambda z: z.reshape(m, z.shape[2])
            r, k, v, w, a, g = _rw_proj_call(
                flat(xr), flat(xk), flat(xv), flat(tw), flat(ta), flat(tg),
                wts["rw_wr"], wts["rw_wk"], wts["rw_wv"], wts["rw_w2"], wts["rw_a2"],
                wts["rw_g2"], wts["rw_w0"], wts["rw_a0"], jl, *tiles.proj)
            p = jnp.concatenate([
                wts["rw_kk"][jl][None], wts["rw_ka"][jl][None], wts["rw_rk"][jl].reshape(1, d),
                wts["rw_lnw"][jl][None], wts["rw_lnb"][jl][None],
                jnp.zeros((3, d), F32)], axis=0)
            s0 = wkv_state[jl] if time_major else None
            yg, s_fin = _scan_call(r, k, v, w, a, g, p, s0, nb, t, chunk, scan_seqs, time_major)
            x3, h3 = _out_proj_call(x3, out_lay, mod, wts["norm_post"], yg.reshape(x3.shape),
                                    wts["rw_wo"], l, (jl, 0), 1, 1.0, nxt_of(l, 1))
            new_shift.append(ss.reshape(nb, d))
            new_wkv.append(s_fin)
        x3, h3 = ffn(x3, h3, l, 1, 2)
    def stack(xs):
        return xs[0][None] if len(xs) == 1 else jnp.stack(xs)

    return x3, stack(new_conv), stack(new_shift), stack(new_wkv)


def kernel(x_prompt, x_sample, state_conv, state_shift, state_wkv, c_prompt, c_sample, mod_w, mod_b, norm_pre, norm_post, ffn_w_in, ffn_w_out, conv_w_in, conv_w, conv_w_out, rw_mix, rw_w0, rw_w1, rw_w2, rw_a0, rw_a1, rw_a2, rw_g1, rw_g2, rw_kk, rw_ka, rw_rk, rw_wr, rw_wk, rw_wv, rw_wo, rw_lnw, rw_lnb):
    b, t, d = x_prompt.shape
    sb, st, _ = x_sample.shape
    depth = mod_w.shape[0]
    n_sub = norm_pre.shape[1]
    wts = dict(norm_pre=norm_pre.reshape(depth, n_sub, 1, d),
               norm_post=norm_post.reshape(depth, n_sub, 1, d), ffn_w_in=ffn_w_in,
               ffn_w_out=ffn_w_out, ffn_bf16={},
               conv_w_out=conv_w_out.astype(BF16)[:, None],
               rw_wo=rw_wo.astype(BF16)[:, None],
               conv_w_in=conv_w_in, conv_w=conv_w, rw_mix=rw_mix,
               rw_w0=rw_w0, rw_w1=rw_w1, rw_w2=rw_w2, rw_a0=rw_a0, rw_a1=rw_a1, rw_a2=rw_a2,
               rw_g1=rw_g1, rw_g2=rw_g2, rw_kk=rw_kk, rw_ka=rw_ka, rw_rk=rw_rk, rw_wr=rw_wr,
               rw_wk=rw_wk, rw_wv=rw_wv, rw_lnw=rw_lnw, rw_lnb=rw_lnb)

    n_c = b + sb
    pad = (-n_c) % 8
    c_all = jnp.concatenate([c_prompt, c_sample, jnp.zeros((pad, d), F32)], axis=0)
    mod_all = _mod_call(c_all, mod_w, mod_b)
    mod_p = (mod_all[:, :b].reshape(depth, b, N_MOD, d).transpose(0, 2, 1, 3)
             .reshape(depth, N_MOD, b, 1, d))
    mod_s = mod_all[:, b:n_c].reshape(depth, sb, N_MOD, d).transpose(0, 2, 1, 3)

    y_p, conv_p, shift_p, wkv_p = _trunk(
        x_prompt, mod_p, None, None, None, wts, time_major=False)
    y_s, conv_s, shift_s, wkv_s = _trunk(
        x_sample.transpose(1, 0, 2), mod_s, state_conv, state_shift, state_wkv, wts,
        time_major=True)
    return (y_p, y_s.transpose(1, 0, 2), conv_p, shift_p, wkv_p, conv_s, shift_s, wkv_s)
```

```python
import functools
import math
from typing import NamedTuple

import jax
import jax.numpy as jnp
from jax import lax
from jax.experimental import pallas as pl
from jax.experimental.pallas import tpu as pltpu

F32 = jnp.float32
BF16 = jnp.bfloat16

RMS_EPS = 1e-6
GN_EPS = 64e-5
NORM_EPS = 1e-12
HALF_STEP = 0.5
HEAD_SIZE = 64
N_MOD = 9

SUBLANES_V7X = 8
MXU_WIDTH_V7X = 256
GROUP_LANES = MXU_WIDTH_V7X
HEADS_PER_GROUP = GROUP_LANES // HEAD_SIZE
VMEM_LIMIT_V7X = 60 * 2**20
ROW_PIECE = 256
INV_BASE = 16
MOD_COLS = 2048

NN = (((1,), (0,)), ((), ()))
NT = (((1,), (1,)), ((), ()))
TN = (((0,), (0,)), ((), ()))


def _mm(a, b, dims=NN):
    return lax.dot_general(a.astype(BF16), b.astype(BF16), dims, preferred_element_type=F32)


def _sigmoid(x):
    return 1.0 / (1.0 + jnp.exp(-x))


def _params(n_axes):
    return pltpu.CompilerParams(dimension_semantics=("arbitrary",) * n_axes,
                                vmem_limit_bytes=VMEM_LIMIT_V7X)


def _rms(x, g):
    return x * lax.rsqrt(jnp.mean(x * x, axis=-1, keepdims=True) + RMS_EPS) * g


def _modulated_pre(x, m_ref, g):
    return _rms(x, g) * (1.0 + m_ref[1]) + m_ref[0]


def _gated_post(x, y, m_ref, g, res_w):
    return x + (res_w * m_ref[2]) * _rms(y, g)


def _pieces(lead, rows):
    if lead == 1:
        n = min(ROW_PIECE, rows)
        return [(slice(0, 1), slice(r0, r0 + n)) for r0 in range(0, rows, n)]
    n = min(lead, max(1, ROW_PIECE // rows))
    return [(slice(a0, a0 + n), slice(0, rows)) for a0 in range(0, lead, n)]


def _piece_rows(x_ref, ls, rs):
    shp = x_ref[ls, rs, :].shape
    return shp, shp[0] * shp[1]


def _shift_time(u, k, fill, time_major):
    lead, rows, n = u.shape
    if time_major:
        return jnp.concatenate([fill, u[:lead - k]], axis=0)
    out = pltpu.roll(u.reshape(rows, n), k, 0)
    row = lax.broadcasted_iota(jnp.int32, (rows, n), 0)
    for t in range(k):
        out = jnp.where(row == t, fill[t:t + 1, :], out)
    return out.reshape(1, rows, n)


class _Layout:
    def __init__(self, shape, time_major, rows_per_tile):
        self.time_major = time_major
        self.shape = shape
        lead, rows, d = shape
        if time_major:
            self.block = (lead, rows_per_tile, d)
            self.tiles_per_seq = 1
            self.n_tiles = rows // rows_per_tile
        else:
            self.block = (1, rows_per_tile, d)
            self.tiles_per_seq = rows // rows_per_tile
            self.n_tiles = lead * self.tiles_per_seq
        self.tile_rows = self.block[0] * self.block[1]

    def xmap(self, i, *_):
        if self.time_major:
            return (0, i, 0)
        return (i // self.tiles_per_seq, i % self.tiles_per_seq, 0)

    def x_spec(self, **kw):
        return pl.BlockSpec(self.block, self.xmap, **kw)

    def mod_spec(self, mod, l, sub):
        d = self.shape[2]
        if self.time_major:
            return pl.BlockSpec((None, 3, self.block[1], d), lambda i, *_: (l, sub, i, 0))
        tps = self.tiles_per_seq
        return pl.BlockSpec((None, 3, None, 1, d), lambda i, *_: (l, sub, i // tps, 0, 0))


def _gain_spec(gains, l, sub):
    return pl.BlockSpec((None, None, 1, gains.shape[3]), lambda i, *_: (l, sub, 0, 0))


def _mod_kernel(c_ref, w_ref, b_ref, o_ref):
    c = c_ref[...]
    o_ref[...] = _mm(c * _sigmoid(c), w_ref[...]) + b_ref[...]


def _mod_call(c_all, mod_w, mod_b):
    depth, d, n = mod_w.shape
    nbp = c_all.shape[0]
    tn = math.gcd(n, MOD_COLS)
    return pl.pallas_call(
        _mod_kernel,
        grid=(depth, n // tn),
        in_specs=[
            pl.BlockSpec((nbp, d), lambda l, j: (0, 0)),
            pl.BlockSpec((None, d, tn), lambda l, j: (l, 0, j)),
            pl.BlockSpec((None, 1, tn), lambda l, j: (l, 0, j)),
        ],
        out_specs=pl.BlockSpec((None, nbp, tn), lambda l, j: (l, 0, j)),
        out_shape=jax.ShapeDtypeStruct((depth, nbp, n), F32),
        compiler_params=_params(2),
    )(c_all, mod_w, mod_b.reshape(depth, 1, n))


def _out_proj_kernel(*refs, res_w, emit_next):
    if emit_next:
        x_ref, m_ref, gpost_ref, a_ref, w_ref, mn_ref, gn_ref, o_ref, h_ref = refs
    else:
        x_ref, m_ref, gpost_ref, a_ref, w_ref, o_ref = refs
    lead, rows, _ = x_ref.shape
    for ls, rs in _pieces(lead, rows):
        shp, n = _piece_rows(x_ref, ls, rs)
        a = a_ref[ls, rs, :]
        y = lax.dot_general(a.reshape(n, a.shape[2]), w_ref[...], NN,
                            preferred_element_type=F32).reshape(shp)
        o = _gated_post(x_ref[ls, rs, :], y, m_ref, gpost_ref[...], res_w)
        o_ref[ls, rs, :] = o
        if emit_next:
            h_ref[ls, rs, :] = _modulated_pre(o, mn_ref, gn_ref[...]).astype(BF16)


def _out_proj_call(x3, lay, mod, norm_post, act, w, l, widx, sub, res_w, nxt=None):
    k, d = w.shape[2], w.shape[3]
    in_specs = [
        lay.x_spec(),
        lay.mod_spec(mod, l, sub),
        _gain_spec(norm_post, l, sub),
        pl.BlockSpec((lay.block[0], lay.block[1], k), lay.xmap),
        pl.BlockSpec((None, None, k, d), lambda i: (widx[0], widx[1], 0, 0),
                     pipeline_mode=pl.Buffered(1)),
    ]
    args = [x3, mod, norm_post, act, w]
    out_specs = [lay.x_spec()]
    out_shape = [jax.ShapeDtypeStruct(x3.shape, F32)]
    if nxt is not None:
        norm_pre, ln, subn = nxt
        in_specs += [lay.mod_spec(mod, ln, subn), _gain_spec(norm_pre, ln, subn)]
        args += [mod, norm_pre]
        out_specs.append(lay.x_spec())
        out_shape.append(jax.ShapeDtypeStruct(x3.shape, BF16))
    out = pl.pallas_call(
        functools.partial(_out_proj_kernel, res_w=res_w, emit_next=nxt is not None),
        grid=(lay.n_tiles,),
        in_specs=in_specs,
        out_specs=out_specs,
        out_shape=out_shape,
        compiler_params=_params(1),
    )(*args)
    return (out[0], out[1]) if nxt is not None else (out[0], None)


def _pre_kernel(x_ref, m_ref, g_ref, h_ref):
    lead, rows, _ = x_ref.shape
    for ls, rs in _pieces(lead, rows):
        h_ref[ls, rs, :] = _modulated_pre(x_ref[ls, rs, :], m_ref, g_ref[...]).astype(BF16)


def _pre_call(x3, lay, mod, norm_pre, l, sub):
    return pl.pallas_call(
        _pre_kernel,
        grid=(lay.n_tiles,),
        in_specs=[lay.x_spec(), lay.mod_spec(mod, l, sub), _gain_spec(norm_pre, l, sub)],
        out_specs=lay.x_spec(),
        out_shape=jax.ShapeDtypeStruct(x3.shape, BF16),
        compiler_params=_params(1),
    )(x3, mod, norm_pre)


def _cast_weights_once(i, pairs):
    @pl.when(i == 0)
    def _():
        for src, dst in pairs:
            dst[...] = src[...].astype(BF16)


def _ffn_in_kernel(*refs, cast_w_out, n_row_tiles):
    if cast_w_out:
        h_ref, wg_ref, wu_ref, wo_ref, a_ref, wob_ref, wgb_ref, wub_ref = refs
        share = wo_ref.shape[0] // n_row_tiles
        rows_i = pl.ds(pl.multiple_of(pl.program_id(1) * share, share), share)
        wob_ref[rows_i, :] = wo_ref[rows_i, :].astype(BF16)
    else:
        h_ref, wg_ref, wu_ref, a_ref, wgb_ref, wub_ref = refs
    _cast_weights_once(pl.program_id(1), [(wg_ref, wgb_ref), (wu_ref, wub_ref)])
    lead, rows, d = h_ref.shape
    for ls, rs in _pieces(lead, rows):
        shp, n = _piece_rows(h_ref, ls, rs)
        h = h_ref[ls, rs, :].reshape(n, d)
        gt = lax.dot_general(h, wgb_ref[...], NN, preferred_element_type=F32)
        up = lax.dot_general(h, wub_ref[...], NN, preferred_element_type=F32)
        a_ref[ls, rs, :] = (gt * _sigmoid(gt) * up).astype(BF16).reshape(shp[0], shp[1], -1)


def _ffn_in_call(h3, lay, w_in, l, s, tf, w_out=None):
    lead, rows, d = h3.shape
    f = w_in.shape[3] // 2
    nj = f // tf
    in_specs = [
        pl.BlockSpec(lay.block, lambda j, i: lay.xmap(i)),
        pl.BlockSpec((None, None, d, tf), lambda j, i: (l, s, 0, j)),
        pl.BlockSpec((None, None, d, tf), lambda j, i: (l, s, 0, nj + j)),
    ]
    args = [h3, w_in, w_in]
    out_specs = [pl.BlockSpec((lay.block[0], lay.block[1], tf),
                              lambda j, i: lay.xmap(i)[:2] + (j,))]
    out_shape = [jax.ShapeDtypeStruct((lead, rows, f), BF16)]
    if w_out is not None:
        assert tf % lay.n_tiles == 0
        in_specs.append(pl.BlockSpec((None, None, tf, d), lambda j, i: (l, s, j, 0)))
        args.append(w_out)
        out_specs.append(pl.BlockSpec((tf, d), lambda j, i: (j, 0)))
        out_shape.append(jax.ShapeDtypeStruct((f, d), BF16))
    out = pl.pallas_call(
        functools.partial(_ffn_in_kernel, cast_w_out=w_out is not None,
                          n_row_tiles=lay.n_tiles),
        grid=(nj, lay.n_tiles),
        in_specs=in_specs,
        out_specs=out_specs,
        out_shape=out_shape,
        scratch_shapes=[pltpu.VMEM((d, tf), BF16), pltpu.VMEM((d, tf), BF16)],
        compiler_params=_params(2),
    )(*args)
    return (out[0], out[1]) if w_out is not None else (out[0], None)


def _conv_kernel(*refs, time_major, tiles_per_seq):
    if time_major:
        (h_ref, wb_ref, wc_ref, wx_ref, cw_ref, st_ref,
         o_ref, so_ref, wbb_ref, wcb_ref, wxb_ref) = refs
    else:
        (h_ref, wb_ref, wc_ref, wx_ref, cw_ref,
         o_ref, so_ref, wbb_ref, wcb_ref, wxb_ref, carry_ref) = refs
    i = pl.program_id(1)
    _cast_weights_once(i, [(wb_ref, wbb_ref), (wc_ref, wcb_ref), (wx_ref, wxb_ref)])
    lead, rows, d = h_ref.shape
    tm = lead * rows

    h = h_ref[...].reshape(tm, d)

    def proj(w_ref):
        return lax.dot_general(h, w_ref[...], NN, preferred_element_type=F32)

    bg = proj(wbb_ref)
    u2 = proj(wcb_ref) * proj(wxb_ref)
    tn = u2.shape[1]
    u = u2.reshape(lead, rows, tn)
    if time_major:
        st = st_ref[...]
        fill1, fill2 = st[1:2], st
        so_ref[...] = u[lead - 2:lead]
    else:
        @pl.when(i % tiles_per_seq == 0)
        def _():
            carry_ref[...] = jnp.zeros((SUBLANES_V7X, tn), F32)
        prev = carry_ref[...]
        fill1, fill2 = prev[SUBLANES_V7X - 1:, :], prev[SUBLANES_V7X - 2:, :]
        carry_ref[...] = u2[tm - SUBLANES_V7X:tm, :]
        so_ref[...] = u[:, rows - 2:rows, :]
    cw = cw_ref[...]
    z = (cw[0:1, :] * _shift_time(u, 2, fill2, time_major)
         + cw[1:2, :] * _shift_time(u, 1, fill1, time_major) + cw[2:3, :] * u)
    o_ref[...] = (bg.reshape(lead, rows, tn) * z).astype(BF16)


def _conv_call(h3, lay, w_in, cw, state, jl, tn):
    lead_n, rows_n, d = h3.shape
    dc = cw.shape[2]
    nj = dc // tn
    in_specs = [
        pl.BlockSpec(lay.block, lambda j, i: lay.xmap(i)),
        pl.BlockSpec((None, d, tn), lambda j, i: (jl, 0, j)),
        pl.BlockSpec((None, d, tn), lambda j, i: (jl, 0, nj + j)),
        pl.BlockSpec((None, d, tn), lambda j, i: (jl, 0, 2 * nj + j)),
        pl.BlockSpec((None, cw.shape[1], tn), lambda j, i: (jl, 0, j)),
    ]
    args = [h3, w_in, w_in, w_in, cw]
    scratch = [pltpu.VMEM((d, tn), BF16)] * 3
    if lay.time_major:
        nb = lay.block[1]
        in_specs.append(pl.BlockSpec((2, nb, tn), lambda j, i: (0, i, j)))
        args.append(state)
        so_spec = pl.BlockSpec((2, nb, tn), lambda j, i: (0, i, j))
        so_shape = jax.ShapeDtypeStruct((2, h3.shape[1], dc), F32)
    else:
        scratch.append(pltpu.VMEM((SUBLANES_V7X, tn), F32))
        so_spec = pl.BlockSpec((1, 2, tn), lambda j, i: (i, 0, j))
        so_shape = jax.ShapeDtypeStruct((lay.n_tiles, 2, dc), F32)
    return pl.pallas_call(
        functools.partial(_conv_kernel, time_major=lay.time_major,
                          tiles_per_seq=lay.tiles_per_seq),
        grid=(nj, lay.n_tiles),
        in_specs=in_specs,
        out_specs=[pl.BlockSpec((lay.block[0], lay.block[1], tn),
                                lambda j, i: lay.xmap(i)[:2] + (j,)), so_spec],
        out_shape=[jax.ShapeDtypeStruct((lead_n, rows_n, dc), BF16), so_shape],
        scratch_shapes=scratch,
        compiler_params=_params(2),
    )(*args)


def _rw_prep_kernel(*refs, time_major, tiles_per_seq):
    if time_major:
        (x_ref, m_ref, gpre_ref, mix_ref, w1_ref, a1_ref, g1_ref, st_ref,
         xr_ref, xk_ref, xv_ref, tw_ref, ta_ref, tg_ref, so_ref) = refs
    else:
        (x_ref, m_ref, gpre_ref, mix_ref, w1_ref, a1_ref, g1_ref,
         xr_ref, xk_ref, xv_ref, tw_ref, ta_ref, tg_ref, so_ref, carry_ref) = refs
    i = pl.program_id(0)
    lead, rows, d = x_ref.shape
    n = lead * rows
    h = _modulated_pre(x_ref[...], m_ref, gpre_ref[...])
    if time_major:
        fill = st_ref[...][None]
        so_ref[...] = h[lead - 1]
    else:
        @pl.when(i % tiles_per_seq == 0)
        def _():
            carry_ref[...] = jnp.zeros((SUBLANES_V7X, d), F32)
        fill = carry_ref[SUBLANES_V7X - 1:, :]
        carry_ref[...] = h[0, rows - SUBLANES_V7X:rows, :]
        so_ref[...] = h[:, rows - 1:rows, :]
    xx = _shift_time(h, 1, fill, time_major) - h
    mix = mix_ref[...]

    def mixed(k):
        return h + xx * mix[k:k + 1, :]

    def low_rank(k, w_ref):
        return _mm(mixed(k).reshape(n, d), w_ref[...]).reshape(lead, rows, w_ref.shape[1])

    xr_ref[...] = mixed(0).astype(BF16)
    tw_ref[...] = jnp.tanh(low_rank(1, w1_ref))
    xk_ref[...] = mixed(2).astype(BF16)
    xv_ref[...] = mixed(3).astype(BF16)
    ta_ref[...] = low_rank(4, a1_ref)
    tg_ref[...] = _sigmoid(low_rank(5, g1_ref))


def _rw_prep_call(x3, lay, mod, norm_pre, mix, w1, a1, g1, state, l, jl, sub):
    lead, rows, d = x3.shape
    dl, dg = w1.shape[2], g1.shape[2]
    in_specs = [
        lay.x_spec(),
        lay.mod_spec(mod, l, sub),
        _gain_spec(norm_pre, l, sub),
        pl.BlockSpec((None, mix.shape[1], d), lambda i: (jl, 0, 0)),
        pl.BlockSpec((None, d, dl), lambda i: (jl, 0, 0)),
        pl.BlockSpec((None, d, dl), lambda i: (jl, 0, 0)),
        pl.BlockSpec((None, d, dg), lambda i: (jl, 0, 0)),
    ]
    args = [x3, mod, norm_pre, mix, w1, a1, g1]
    scratch = []
    blk = lay.block
    if lay.time_major:
        in_specs.append(pl.BlockSpec((blk[1], d), lambda i: (i, 0)))
        args.append(state)
        so_spec = pl.BlockSpec((blk[1], d), lambda i: (i, 0))
        so_shape = jax.ShapeDtypeStruct((rows, d), F32)
    else:
        scratch.append(pltpu.VMEM((SUBLANES_V7X, d), F32))
        tps = lay.tiles_per_seq
        so_spec = pl.BlockSpec((1, 1, d), lambda i: (i // tps, 0, 0))
        so_shape = jax.ShapeDtypeStruct((lead, 1, d), F32)

    def ospec(width):
        return pl.BlockSpec((blk[0], blk[1], width), lay.xmap)

    return pl.pallas_call(
        functools.partial(_rw_prep_kernel, time_major=lay.time_major,
                          tiles_per_seq=lay.tiles_per_seq),
        grid=(lay.n_tiles,),
        in_specs=in_specs,
        out_specs=[ospec(d)] * 3 + [ospec(dl), ospec(dl), ospec(dg), so_spec],
        out_shape=[jax.ShapeDtypeStruct((lead, rows, d), BF16)] * 3
        + [jax.ShapeDtypeStruct((lead, rows, dl), F32)] * 2
        + [jax.ShapeDtypeStruct((lead, rows, dg), F32), so_shape],
        scratch_shapes=scratch,
        compiler_params=_params(1),
    )(*args)


def _rw_proj_kernel(xr_ref, xk_ref, xv_ref, tw_ref, ta_ref, tg_ref, wr_ref, wk_ref, wv_ref,
                    w2_ref, a2_ref, g2_ref, w0_ref, a0_ref,
                    r_ref, k_ref, v_ref, w_ref, a_ref, g_ref, wrb_ref, wkb_ref, wvb_ref,
                    *, cols_outer):
    pairs = [(wr_ref, wrb_ref), (wk_ref, wkb_ref), (wv_ref, wvb_ref)]
    if cols_outer:
        _cast_weights_once(pl.program_id(1), pairs)
    else:
        for src, dst in pairs:
            dst[...] = src[...].astype(BF16)

    def proj(x_ref, wb_ref):
        return lax.dot_general(x_ref[...], wb_ref[...], NN, preferred_element_type=F32)

    r_ref[...] = proj(xr_ref, wrb_ref)
    k_ref[...] = proj(xk_ref, wkb_ref)
    v_ref[...] = proj(xv_ref, wvb_ref)
    w_ref[...] = w0_ref[...] + _mm(tw_ref[...], w2_ref[...])
    a_ref[...] = _sigmoid(a0_ref[...] + _mm(ta_ref[...], a2_ref[...]))
    g_ref[...] = _mm(tg_ref[...], g2_ref[...])


def _rw_proj_call(xr, xk, xv, tw, ta, tg, wr, wk, wv, w2, a2, g2, w0, a0, jl, tm, tn):
    m, d = xr.shape
    dl, dg = tw.shape[1], tg.shape[1]
    nr = w0.shape[0]
    cols_outer = m // tm > 2
    grid = (d // tn, m // tm) if cols_outer else (m // tm, d // tn)

    def ij(f):
        return (lambda j, i: f(i, j)) if cols_outer else f

    xspec = pl.BlockSpec((tm, d), ij(lambda i, j: (i, 0)))
    wspec = pl.BlockSpec((None, d, tn), ij(lambda i, j: (jl, 0, j)))
    vspec = pl.BlockSpec((None, 1, tn), ij(lambda i, j: (jl, 0, j)))
    ospec = pl.BlockSpec((tm, tn), ij(lambda i, j: (i, j)))
    return pl.pallas_call(
        functools.partial(_rw_proj_kernel, cols_outer=cols_outer),
        grid=grid,
        in_specs=[xspec, xspec, xspec,
                  pl.BlockSpec((tm, dl), ij(lambda i, j: (i, 0))),
                  pl.BlockSpec((tm, dl), ij(lambda i, j: (i, 0))),
                  pl.BlockSpec((tm, dg), ij(lambda i, j: (i, 0))),
                  wspec, wspec, wspec,
                  pl.BlockSpec((None, dl, tn), ij(lambda i, j: (jl, 0, j))),
                  pl.BlockSpec((None, dl, tn), ij(lambda i, j: (jl, 0, j))),
                  pl.BlockSpec((None, dg, tn), ij(lambda i, j: (jl, 0, j))),
                  vspec, vspec],
        out_specs=[ospec] * 6,
        out_shape=[jax.ShapeDtypeStruct((m, d), F32)] * 6,
        scratch_shapes=[pltpu.VMEM((d, tn), BF16)] * 3,
        compiler_params=_params(2),
    )(xr, xk, xv, tw, ta, tg, wr, wk, wv, w2, a2, g2,
      w0.reshape(nr, 1, d), a0.reshape(nr, 1, d))


def _iota2(shape, axis):
    return lax.broadcasted_iota(jnp.int32, shape, axis)


def _same_block(shape, row_block, lane_block):
    r = lax.shift_right_logical(_iota2(shape, 0), int(math.log2(row_block)))
    c = lax.shift_right_logical(_iota2(shape, 1), int(math.log2(lane_block)))
    return r == c


def _bd_rows(x, mask):
    return jnp.where(mask, jnp.concatenate([x] * HEADS_PER_GROUP, axis=0), 0.0)


def _split_mm(x, ones):
    hi = x.astype(BF16)
    lo = (x - hi.astype(F32)).astype(BF16)
    return (lax.dot_general(hi, ones, NN, preferred_element_type=F32)
            + lax.dot_general(lo, ones, NN, preferred_element_type=F32))


def _each(fn, *lists):
    return [fn(*xs) for xs in zip(*lists)]


def _wkv_chunk(r, k, v, a_, b_, lw, cum, s_bd, c):
    hs = HEAD_SIZE
    tlanes = HEADS_PER_GROUP * c
    m_ch = _same_block((tlanes, GROUP_LANES), c, hs)
    m_tt = _same_block((tlanes, tlanes), c, c)
    m_ss = _same_block((GROUP_LANES, GROUP_LANES), hs, hs)
    t_row = _iota2((c, tlanes), 0)
    s_lane = jnp.bitwise_and(_iota2((c, tlanes), 1), c - 1)
    strict = s_lane < t_row
    incl = s_lane <= t_row

    def bd_ch(x):
        return _bd_rows(x, m_ch)

    def bd_tt(x):
        return _bd_rows(x, m_tt)

    def apply(a, x):
        return _mm(a, bd_ch(x))

    cl = _each(lambda x: x[c - 1:c, :], cum)
    at = _each(lambda a, x, l: a * jnp.exp(x - l), a_, cum, lw)
    rt = _each(lambda a, x: a * jnp.exp(x), r, cum)
    bt = _each(lambda a, x: a * jnp.exp(-x), b_, cum)
    kt = _each(lambda a, x: a * jnp.exp(-x), k, cum)
    bh = _each(lambda a, x, xl: a * jnp.exp(xl - x), b_, cum, cl)
    kh = _each(lambda a, x, xl: a * jnp.exp(xl - x), k, cum, cl)

    lhs = _each(lambda a, b: jnp.concatenate([a, b], axis=0), at, rt)
    ob = _each(lambda a, x: _mm(a, bd_ch(x), NT), lhs, bt)
    ok = _each(lambda a, x: _mm(a, bd_ch(x), NT), lhs, kt)
    a_ab = _each(lambda x: jnp.where(strict, x[:c], 0.0), ob)
    a_rb = _each(lambda x: jnp.where(incl, x[c:], 0.0), ob)
    a_ak = _each(lambda x: jnp.where(strict, x[:c], 0.0), ok)
    a_rk = _each(lambda x: jnp.where(incl, x[c:], 0.0), ok)
    akv = _each(lambda a, b, x: apply(jnp.concatenate([a, b], axis=0), x), a_ak, a_rk, v)
    av = _each(lambda x: x[:c], akv)
    y_kv = _each(lambda x: x[c:], akv)

    base = min(c, INV_BASE)
    eye = jnp.where(s_lane == t_row, 1.0, 0.0)
    in_base = jnp.bitwise_and(t_row, -base) == jnp.bitwise_and(s_lane, -base)
    tinv = _each(lambda x: eye + jnp.where(in_base, x, 0.0), a_ab)
    if base > 1:
        npow = _each(lambda x: _mm(x, bd_tt(x)), _each(lambda x: jnp.where(in_base, x, 0.0), a_ab))
        for _ in range(int(math.log2(base)) - 2):
            both = _each(lambda t, x: _mm(jnp.concatenate([t, x], axis=0), bd_tt(x)), tinv, npow)
            tinv = _each(lambda t, z: t + z[:c], tinv, both)
            npow = _each(lambda z: z[c:], both)
        if base > 2:
            tinv = _each(lambda t, x: t + _mm(t, bd_tt(x)), tinv, npow)
    size = base
    while size < c:
        lower_left = ((jnp.bitwise_and(t_row, -2 * size) == jnp.bitwise_and(s_lane, -2 * size))
                      & (jnp.bitwise_and(t_row, size) != 0) & (jnp.bitwise_and(s_lane, size) == 0))
        cross = _each(lambda x, t: _mm(jnp.where(lower_left, x, 0.0), bd_tt(t)), a_ab, tinv)
        tinv = _each(lambda t, x: t + _mm(t, bd_tt(x)), tinv, cross)
        size *= 2

    wt = _each(apply, tinv, at)
    w = _each(apply, tinv, av)
    qt = _each(lambda x, a, y: x + apply(a, y), rt, a_rb, wt)
    yi = _each(lambda a, x, y: apply(a, x) + y, a_rb, w, y_kv)
    g_off = _each(lambda x, y: jnp.where(m_ss, _mm(x, y, TN), 0.0), wt, bh)
    h_t = _each(lambda x, y, p, q: jnp.where(
        m_ss, _mm(jnp.concatenate([x, p], axis=0), jnp.concatenate([y, q], axis=0), TN), 0.0),
                w, bh, v, kh)

    y = _each(lambda q, s, x: _mm(q, s, NT) + x, qt, s_bd, yi)
    s_new = _each(lambda s, xl, g, h: s * jnp.exp(xl) + _mm(s, g) + h, s_bd, cl, g_off, h_t)
    return y, s_new


def _scan_kernel(*refs, c, d, nq, has_state, lane_packed):
    if has_state:
        (r_ref, k_ref, v_ref, w_ref, a_ref, g_ref, p_ref, s0_ref, y_ref, so_ref, sbd_ref) = refs
    else:
        (r_ref, k_ref, v_ref, w_ref, a_ref, g_ref, p_ref, y_ref, so_ref, sbd_ref) = refs
    ci = pl.program_id(1)
    hs = HEAD_SIZE
    ng = d // GROUP_LANES
    chains = [(q, g) for q in range(nq) for g in range(ng)]
    n_ch = len(chains)
    m_ss = _same_block((GROUP_LANES, GROUP_LANES), hs, hs)
    ones_bd = jnp.where(m_ss, 1.0, 0.0).astype(BF16)

    def lanes(g):
        return slice(g * GROUP_LANES, (g + 1) * GROUP_LANES)

    def rd(ref, q, g):
        if lane_packed:
            return ref[:, q * d + g * GROUP_LANES:q * d + (g + 1) * GROUP_LANES]
        return ref[q, :, lanes(g)]

    def par(row, g):
        return p_ref[row:row + 1, lanes(g)]

    @pl.when(ci == 0)
    def _():
        for i, (q, g) in enumerate(chains):
            if has_state:
                heads = [s0_ref[q, HEADS_PER_GROUP * g + h] for h in range(HEADS_PER_GROUP)]
                sbd_ref[i] = _bd_rows(jnp.concatenate(heads, axis=1), m_ss)
            else:
                sbd_ref[i] = jnp.zeros((GROUP_LANES, GROUP_LANES), F32)

    def log_decay(wq):
        z = -wq
        softplus = jnp.maximum(z, 0.0) + jnp.log(1.0 + jnp.exp(-jnp.abs(z)))
        lw = -jnp.exp(-softplus - 0.5)
        row = _iota2(lw.shape, 0)
        cum = lw
        step = 1
        while step < c:
            cum = cum + jnp.where(row >= step, pltpu.roll(cum, step, 0), 0.0)
            step *= 2
        return lw, cum

    if lane_packed:
        lw_all, cum_all = log_decay(w_ref[...])
        lw = [lw_all[:, q * d + g * GROUP_LANES:q * d + (g + 1) * GROUP_LANES] for q, g in chains]
        cum = [cum_all[:, q * d + g * GROUP_LANES:q * d + (g + 1) * GROUP_LANES] for q, g in chains]
    else:
        per_q = [log_decay(w_ref[q]) for q in range(nq)]
        lw = [per_q[q][0][:, lanes(g)] for q, g in chains]
        cum = [per_q[q][1][:, lanes(g)] for q, g in chains]

    def seg_sum(xs, split=True):
        x = jnp.concatenate(xs, axis=0)
        y = _split_mm(x, ones_bd) if split else _mm(x, ones_bd)
        return [y[i * c:(i + 1) * c] for i in range(len(xs))]

    r = [rd(r_ref, q, g) for q, g in chains]
    k = [rd(k_ref, q, g) for q, g in chains]
    v = [rd(v_ref, q, g) for q, g in chains]
    a_sig = [rd(a_ref, q, g) for q, g in chains]
    kk = [x * par(0, g) for x, (q, g) in zip(k, chains)]
    k = [x * (1.0 + (a - 1.0) * par(1, g)) for x, a, (q, g) in zip(k, a_sig, chains)]
    kk = _each(lambda x, n: x / jnp.maximum(jnp.sqrt(n), NORM_EPS), kk,
               seg_sum(_each(lambda x: x * x, kk)))
    bonus = _each(lambda s, z: s * z,
                  seg_sum([x * y * par(2, g) for x, y, (q, g) in zip(r, k, chains)], split=False),
                  v)
    y, s_new = _wkv_chunk(r, k, v, _each(lambda x: -x, kk), _each(lambda x, a: x * a, kk, a_sig),
                          lw, cum, [sbd_ref[i] for i in range(n_ch)], c)
    for i in range(n_ch):
        sbd_ref[i] = s_new[i]

    mu = _each(lambda x: x * (1.0 / hs), seg_sum(y))
    dy = _each(lambda x, m: x - m, y, mu)
    var = _each(lambda x: x * (1.0 / hs), seg_sum(_each(lambda x: x * x, dy), split=False))
    for i, (q, g) in enumerate(chains):
        yn = dy[i] * lax.rsqrt(var[i] + GN_EPS) * par(3, g) + par(4, g)
        out = ((yn + bonus[i]) * rd(g_ref, q, g)).astype(BF16)
        if lane_packed:
            y_ref[:, q * d + g * GROUP_LANES:q * d + (g + 1) * GROUP_LANES] = out
        else:
            y_ref[q, :, lanes(g)] = out

    @pl.when(ci == pl.num_programs(1) - 1)
    def _():
        for i, (q, g) in enumerate(chains):
            s = sbd_ref[i]
            s = s[0:hs] + s[hs:2 * hs] + s[2 * hs:3 * hs] + s[3 * hs:4 * hs]
            for h in range(HEADS_PER_GROUP):
                so_ref[q, HEADS_PER_GROUP * g + h] = s[:, h * hs:(h + 1) * hs]


def _scan_call(r, k, v, w, a, g, p, s0, nb, t, c, nq, time_major):
    m, d = r.shape
    nc = t // c
    ng = d // GROUP_LANES
    nh = d // HEAD_SIZE
    has_state = s0 is not None
    if time_major:
        r, k, v, w, a, g = (z.reshape(t, nb * d) for z in (r, k, v, w, a, g))
        xspec = pl.BlockSpec((c, nq * d), lambda b, ci: (ci, b))
    else:
        r, k, v, w, a, g = (z.reshape(nb, t, d) for z in (r, k, v, w, a, g))
        xspec = pl.BlockSpec((nq, c, d), lambda b, ci: (b, ci, 0))
    sspec = pl.BlockSpec((nq, nh, HEAD_SIZE, HEAD_SIZE), lambda b, ci: (b, 0, 0, 0))
    in_specs = [xspec] * 6 + [pl.BlockSpec(p.shape, lambda b, ci: (0, 0))]
    args = [r, k, v, w, a, g, p]
    if has_state:
        in_specs.append(sspec)
        args.append(s0)
    y, s_fin = pl.pallas_call(
        functools.partial(_scan_kernel, c=c, d=d, nq=nq, has_state=has_state,
                          lane_packed=time_major),
        grid=(nb // nq, nc),
        in_specs=in_specs,
        out_specs=[xspec, sspec],
        out_shape=[jax.ShapeDtypeStruct(r.shape, BF16),
                   jax.ShapeDtypeStruct((nb, nh, HEAD_SIZE, HEAD_SIZE), F32)],
        scratch_shapes=[pltpu.VMEM((nq * ng, GROUP_LANES, GROUP_LANES), F32)],
        compiler_params=_params(2),
    )(*args)
    return y.reshape(m, d), s_fin


class _Tiles(NamedTuple):
    ffn: int
    ffn_cols: int
    conv: int
    conv_cols: int
    out: int
    prep: int
    proj: tuple
    chunk: int
    scan_seqs: int


def _tiles(time_major, nb, t, d):
    if time_major:
        return _Tiles(ffn=nb, ffn_cols=512, conv=nb, conv_cols=min(d, 256), out=min(nb, 32),
                      prep=min(nb, 32), proj=(nb * t, min(d, 256)), chunk=t,
                      scan_seqs=math.gcd(nb, 4))
    return _Tiles(ffn=min(t, 2048), ffn_cols=512, conv=min(t, 2048), conv_cols=min(d, 256),
                  out=min(t, 512), prep=min(t, 256), proj=(min(nb * t, 512), min(d, 512)),
                  chunk=min(t, 64), scan_seqs=math.gcd(nb, 2))


def _trunk(x3, mod, conv_state, shift_state, wkv_state, wts, *, time_major):
    d = x3.shape[2]
    nb, t = (x3.shape[1], x3.shape[0]) if time_major else (x3.shape[0], x3.shape[1])
    depth = mod.shape[0]
    tiles = _tiles(time_major, nb, t, d)
    chunk, scan_seqs = tiles.chunk, tiles.scan_seqs
    ffn_lay = _Layout(x3.shape, time_major, tiles.ffn)
    lay = _Layout(x3.shape, time_major, tiles.conv)
    out_lay = _Layout(x3.shape, time_major, tiles.out)
    prep_lay = _Layout(x3.shape, time_major, tiles.prep)
    m = nb * t
    new_conv, new_shift, new_wkv = [], [], []

    subs = [(l, sub) for l in range(depth) for sub in range(3)]

    def takes_h(l, sub):
        return not (sub == 1 and l % 2 == 1)

    def nxt_of(l, sub):
        i = subs.index((l, sub)) + 1
        if i < len(subs) and takes_h(*subs[i]):
            return (wts["norm_pre"],) + subs[i]
        return None

    def ffn(x3, h3, l, s, sub):
        w_out = wts["ffn_w_out_bf16"].get((l, s))
        act, cast = _ffn_in_call(h3, ffn_lay, wts["ffn_w_in"], l, s, tiles.ffn_cols,
                                 wts["ffn_w_out"] if w_out is None else None)
        if w_out is None:
            w_out = wts["ffn_w_out_bf16"][(l, s)] = cast
        return _out_proj_call(x3, out_lay, mod, wts["norm_post"], act, w_out[None, None],
                              l, (0, 0), sub, HALF_STEP, nxt_of(l, sub))

    h3 = _pre_call(x3, out_lay, mod, wts["norm_pre"], 0, 0)
    for l in range(depth):
        jl = l // 2
        x3, h3 = ffn(x3, h3, l, 0, 0)
        if l % 2 == 0:
            st = conv_state[jl].transpose(1, 0, 2) if time_major else None
            bz, cs = _conv_call(h3, lay, wts["conv_w_in"], wts["conv_w"], st, jl,
                                tiles.conv_cols)
            x3, h3 = _out_proj_call(x3, out_lay, mod, wts["norm_post"], bz, wts["conv_w_out"],
                                    l, (jl, 0), 1, 1.0, nxt_of(l, 1))
            if time_major:
                new_conv.append(cs.transpose(1, 0, 2))
            else:
                tps = lay.tiles_per_seq
                new_conv.append(cs[tps - 1::tps])
        else:
            st = shift_state[jl] if time_major else None
            xr, xk, xv, tw, ta, tg, ss = _rw_prep_call(
                x3, prep_lay, mod, wts["norm_pre"], wts["rw_mix"], wts["rw_w1"], wts["rw_a1"],
                wts["rw_g1"], st, l, jl, 1)
            flat = lambda z: z.reshape(m, z.shape[2])
            r, k, v, w, a, g = _rw_proj_call(
                flat(xr), flat(xk), flat(xv), flat(tw), flat(ta), flat(tg),
                wts["rw_wr"], wts["rw_wk"], wts["rw_wv"], wts["rw_w2"], wts["rw_a2"],
                wts["rw_g2"], wts["rw_w0"], wts["rw_a0"], jl, *tiles.proj)
            p = jnp.concatenate([
                wts["rw_kk"][jl][None], wts["rw_ka"][jl][None], wts["rw_rk"][jl].reshape(1, d),
                wts["rw_lnw"][jl][None], wts["rw_lnb"][jl][None],
                jnp.zeros((3, d), F32)], axis=0)
            s0 = wkv_state[jl] if time_major else None
            yg, s_fin = _scan_call(r, k, v, w, a, g, p, s0, nb, t, chunk, scan_seqs, time_major)
            x3, h3 = _out_proj_call(x3, out_lay, mod, wts["norm_post"], yg.reshape(x3.shape),
                                    wts["rw_wo"], l, (jl, 0), 1, 1.0, nxt_of(l, 1))
            new_shift.append(ss.reshape(nb, d))
            new_wkv.append(s_fin)
        x3, h3 = ffn(x3, h3, l, 1, 2)
    def stack(xs):
        return xs[0][None] if len(xs) == 1 else jnp.stack(xs)

    return x3, stack(new_conv), stack(new_shift), stack(new_wkv)


def kernel(x_prompt, x_sample, state_conv, state_shift, state_wkv, c_prompt, c_sample, mod_w, mod_b, norm_pre, norm_post, ffn_w_in, ffn_w_out, conv_w_in, conv_w, conv_w_out, rw_mix, rw_w0, rw_w1, rw_w2, rw_a0, rw_a1, rw_a2, rw_g1, rw_g2, rw_kk, rw_ka, rw_rk, rw_wr, rw_wk, rw_wv, rw_wo, rw_lnw, rw_lnb):
    b, t, d = x_prompt.shape
    sb, st, _ = x_sample.shape
    depth = mod_w.shape[0]
    n_sub = norm_pre.shape[1]
    wts = dict(norm_pre=norm_pre.reshape(depth, n_sub, 1, d),
               norm_post=norm_post.reshape(depth, n_sub, 1, d), ffn_w_in=ffn_w_in,
               ffn_w_out=ffn_w_out, ffn_w_out_bf16={},
               conv_w_out=conv_w_out.astype(BF16)[:, None],
               rw_wo=rw_wo.astype(BF16)[:, None],
               conv_w_in=conv_w_in, conv_w=conv_w, rw_mix=rw_mix,
               rw_w0=rw_w0, rw_w1=rw_w1, rw_w2=rw_w2, rw_a0=rw_a0, rw_a1=rw_a1, rw_a2=rw_a2,
               rw_g1=rw_g1, rw_g2=rw_g2, rw_kk=rw_kk, rw_ka=rw_ka, rw_rk=rw_rk, rw_wr=rw_wr,
               rw_wk=rw_wk, rw_wv=rw_wv, rw_lnw=rw_lnw, rw_lnb=rw_lnb)

    n_c = b + sb
    pad = (-n_c) % SUBLANES_V7X
    c_all = jnp.concatenate([c_prompt, c_sample, jnp.zeros((pad, d), F32)], axis=0)
    mod_all = _mod_call(c_all, mod_w, mod_b)
    mod_p = (mod_all[:, :b].reshape(depth, b, N_MOD, d).transpose(0, 2, 1, 3)
             .reshape(depth, N_MOD, b, 1, d))
    mod_s = mod_all[:, b:n_c].reshape(depth, sb, N_MOD, d).transpose(0, 2, 1, 3)

    y_p, conv_p, shift_p, wkv_p = _trunk(
        x_prompt, mod_p, None, None, None, wts, time_major=False)
    y_s, conv_s, shift_s, wkv_s = _trunk(
        x_sample.transpose(1, 0, 2), mod_s, state_conv, state_shift, state_wkv, wts,
        time_major=True)
    return (y_p, y_s.transpose(1, 0, 2), conv_p, shift_p, wkv_p, conv_s, shift_s, wkv_s)
```

```python
import functools
import math
from typing import NamedTuple

import jax
import jax.numpy as jnp
from jax import lax
from jax.experimental import pallas as pl
from jax.experimental.pallas import tpu as pltpu

F32 = jnp.float32
BF16 = jnp.bfloat16

RMS_EPS = 1e-6
GN_EPS = 64e-5
NORM_EPS = 1e-12
HALF_STEP = 0.5
HEAD_SIZE = 64
N_MOD = 9

SUBLANES_V7X = 8
MXU_WIDTH_V7X = 256
GROUP_LANES = MXU_WIDTH_V7X
HEADS_PER_GROUP = GROUP_LANES // HEAD_SIZE
VMEM_LIMIT_V7X = 60 * 2**20
ROW_PIECE = 256
INV_BASE = 16
MOD_COLS = 2048

NN = (((1,), (0,)), ((), ()))
NT = (((1,), (1,)), ((), ()))
TN = (((0,), (0,)), ((), ()))


def _mm(a, b, dims=NN):
    return lax.dot_general(a.astype(BF16), b.astype(BF16), dims, preferred_element_type=F32)


def _sigmoid(x):
    return 1.0 / (1.0 + jnp.exp(-x))


def _params(n_axes):
    return pltpu.CompilerParams(dimension_semantics=("arbitrary",) * n_axes,
                                vmem_limit_bytes=VMEM_LIMIT_V7X)


def _rms(x, g):
    return x * lax.rsqrt(jnp.mean(x * x, axis=-1, keepdims=True) + RMS_EPS) * g


def _modulated_pre(x, m_ref, g):
    return _rms(x, g) * (1.0 + m_ref[1]) + m_ref[0]


def _gated_post(x, y, m_ref, g, res_w):
    return x + (res_w * m_ref[2]) * _rms(y, g)


def _pieces(lead, rows):
    if lead == 1:
        n = min(ROW_PIECE, rows)
        return [(slice(0, 1), slice(r0, r0 + n)) for r0 in range(0, rows, n)]
    n = min(lead, max(1, ROW_PIECE // rows))
    return [(slice(a0, a0 + n), slice(0, rows)) for a0 in range(0, lead, n)]


def _piece_rows(x_ref, ls, rs):
    shp = x_ref[ls, rs, :].shape
    return shp, shp[0] * shp[1]


def _shift_time(u, k, fill, time_major):
    lead, rows, n = u.shape
    if time_major:
        return jnp.concatenate([fill, u[:lead - k]], axis=0)
    out = pltpu.roll(u.reshape(rows, n), k, 0)
    row = lax.broadcasted_iota(jnp.int32, (rows, n), 0)
    for t in range(k):
        out = jnp.where(row == t, fill[t:t + 1, :], out)
    return out.reshape(1, rows, n)


class _Layout:
    def __init__(self, shape, time_major, rows_per_tile):
        self.time_major = time_major
        self.shape = shape
        lead, rows, d = shape
        if time_major:
            self.block = (lead, rows_per_tile, d)
            self.tiles_per_seq = 1
            self.n_tiles = rows // rows_per_tile
        else:
            self.block = (1, rows_per_tile, d)
            self.tiles_per_seq = rows // rows_per_tile
            self.n_tiles = lead * self.tiles_per_seq
        self.tile_rows = self.block[0] * self.block[1]

    def xmap(self, i, *_):
        if self.time_major:
            return (0, i, 0)
        return (i // self.tiles_per_seq, i % self.tiles_per_seq, 0)

    def x_spec(self, **kw):
        return pl.BlockSpec(self.block, self.xmap, **kw)

    def mod_spec(self, mod, l, sub):
        d = self.shape[2]
        if self.time_major:
            return pl.BlockSpec((None, 3, self.block[1], d), lambda i, *_: (l, sub, i, 0))
        tps = self.tiles_per_seq
        return pl.BlockSpec((None, 3, None, 1, d), lambda i, *_: (l, sub, i // tps, 0, 0))


def _gain_spec(gains, l, sub):
    return pl.BlockSpec((None, None, 1, gains.shape[3]), lambda i, *_: (l, sub, 0, 0))


def _mod_kernel(c_ref, w_ref, b_ref, o_ref):
    c = c_ref[...]
    o_ref[...] = _mm(c * _sigmoid(c), w_ref[...]) + b_ref[...]


def _mod_call(c_all, mod_w, mod_b):
    depth, d, n = mod_w.shape
    nbp = c_all.shape[0]
    tn = math.gcd(n, MOD_COLS)
    return pl.pallas_call(
        _mod_kernel,
        grid=(depth, n // tn),
        in_specs=[
            pl.BlockSpec((nbp, d), lambda l, j: (0, 0)),
            pl.BlockSpec((None, d, tn), lambda l, j: (l, 0, j)),
            pl.BlockSpec((None, 1, tn), lambda l, j: (l, 0, j)),
        ],
        out_specs=pl.BlockSpec((None, nbp, tn), lambda l, j: (l, 0, j)),
        out_shape=jax.ShapeDtypeStruct((depth, nbp, n), F32),
        compiler_params=_params(2),
    )(c_all, mod_w, mod_b.reshape(depth, 1, n))


def _out_proj_kernel(*refs, res_w, emit_next):
    if emit_next:
        x_ref, m_ref, gpost_ref, a_ref, w_ref, mn_ref, gn_ref, o_ref, h_ref = refs
    else:
        x_ref, m_ref, gpost_ref, a_ref, w_ref, o_ref = refs
    lead, rows, _ = x_ref.shape
    for ls, rs in _pieces(lead, rows):
        shp, n = _piece_rows(x_ref, ls, rs)
        a = a_ref[ls, rs, :]
        y = lax.dot_general(a.reshape(n, a.shape[2]), w_ref[...], NN,
                            preferred_element_type=F32).reshape(shp)
        o = _gated_post(x_ref[ls, rs, :], y, m_ref, gpost_ref[...], res_w)
        o_ref[ls, rs, :] = o
        if emit_next:
            h_ref[ls, rs, :] = _modulated_pre(o, mn_ref, gn_ref[...]).astype(BF16)


def _out_proj_call(x3, lay, mod, norm_post, act, w, l, widx, sub, res_w, nxt=None):
    k, d = w.shape[2], w.shape[3]
    in_specs = [
        lay.x_spec(),
        lay.mod_spec(mod, l, sub),
        _gain_spec(norm_post, l, sub),
        pl.BlockSpec((lay.block[0], lay.block[1], k), lay.xmap),
        pl.BlockSpec((None, None, k, d), lambda i: (widx[0], widx[1], 0, 0),
                     pipeline_mode=pl.Buffered(1)),
    ]
    args = [x3, mod, norm_post, act, w]
    out_specs = [lay.x_spec()]
    out_shape = [jax.ShapeDtypeStruct(x3.shape, F32)]
    if nxt is not None:
        norm_pre, ln, subn = nxt
        in_specs += [lay.mod_spec(mod, ln, subn), _gain_spec(norm_pre, ln, subn)]
        args += [mod, norm_pre]
        out_specs.append(lay.x_spec())
        out_shape.append(jax.ShapeDtypeStruct(x3.shape, BF16))
    out = pl.pallas_call(
        functools.partial(_out_proj_kernel, res_w=res_w, emit_next=nxt is not None),
        grid=(lay.n_tiles,),
        in_specs=in_specs,
        out_specs=out_specs,
        out_shape=out_shape,
        compiler_params=_params(1),
    )(*args)
    return (out[0], out[1]) if nxt is not None else (out[0], None)


def _pre_kernel(x_ref, m_ref, g_ref, h_ref):
    lead, rows, _ = x_ref.shape
    for ls, rs in _pieces(lead, rows):
        h_ref[ls, rs, :] = _modulated_pre(x_ref[ls, rs, :], m_ref, g_ref[...]).astype(BF16)


def _pre_call(x3, lay, mod, norm_pre, l, sub):
    return pl.pallas_call(
        _pre_kernel,
        grid=(lay.n_tiles,),
        in_specs=[lay.x_spec(), lay.mod_spec(mod, l, sub), _gain_spec(norm_pre, l, sub)],
        out_specs=lay.x_spec(),
        out_shape=jax.ShapeDtypeStruct(x3.shape, BF16),
        compiler_params=_params(1),
    )(x3, mod, norm_pre)


def _cast_weights_once(i, pairs):
    @pl.when(i == 0)
    def _():
        for src, dst in pairs:
            dst[...] = src[...].astype(BF16)


def _ffn_in_kernel(*refs, cast_w_out, n_row_tiles):
    if cast_w_out:
        h_ref, wg_ref, wu_ref, wo_ref, a_ref, wob_ref, wgb_ref, wub_ref = refs
        share = wo_ref.shape[0] // n_row_tiles
        rows_i = pl.ds(pl.multiple_of(pl.program_id(1) * share, share), share)
        wob_ref[rows_i, :] = wo_ref[rows_i, :].astype(BF16)
    else:
        h_ref, wg_ref, wu_ref, a_ref, wgb_ref, wub_ref = refs
    _cast_weights_once(pl.program_id(1), [(wg_ref, wgb_ref), (wu_ref, wub_ref)])
    lead, rows, d = h_ref.shape
    for ls, rs in _pieces(lead, rows):
        shp, n = _piece_rows(h_ref, ls, rs)
        h = h_ref[ls, rs, :].reshape(n, d)
        gt = lax.dot_general(h, wgb_ref[...], NN, preferred_element_type=F32)
        up = lax.dot_general(h, wub_ref[...], NN, preferred_element_type=F32)
        a_ref[ls, rs, :] = (gt * _sigmoid(gt) * up).astype(BF16).reshape(shp[0], shp[1], -1)


def _ffn_in_call(h3, lay, w_in, l, s, tf, w_out=None):
    lead, rows, d = h3.shape
    f = w_in.shape[3] // 2
    nj = f // tf
    in_specs = [
        pl.BlockSpec(lay.block, lambda j, i: lay.xmap(i)),
        pl.BlockSpec((None, None, d, tf), lambda j, i: (l, s, 0, j)),
        pl.BlockSpec((None, None, d, tf), lambda j, i: (l, s, 0, nj + j)),
    ]
    args = [h3, w_in, w_in]
    out_specs = [pl.BlockSpec((lay.block[0], lay.block[1], tf),
                              lambda j, i: lay.xmap(i)[:2] + (j,))]
    out_shape = [jax.ShapeDtypeStruct((lead, rows, f), BF16)]
    if w_out is not None:
        assert tf % lay.n_tiles == 0
        in_specs.append(pl.BlockSpec((None, None, tf, d), lambda j, i: (l, s, j, 0)))
        args.append(w_out)
        out_specs.append(pl.BlockSpec((tf, d), lambda j, i: (j, 0)))
        out_shape.append(jax.ShapeDtypeStruct((f, d), BF16))
    out = pl.pallas_call(
        functools.partial(_ffn_in_kernel, cast_w_out=w_out is not None,
                          n_row_tiles=lay.n_tiles),
        grid=(nj, lay.n_tiles),
        in_specs=in_specs,
        out_specs=out_specs,
        out_shape=out_shape,
        scratch_shapes=[pltpu.VMEM((d, tf), BF16), pltpu.VMEM((d, tf), BF16)],
        compiler_params=_params(2),
    )(*args)
    return (out[0], out[1]) if w_out is not None else (out[0], None)


def _conv_kernel(*refs, time_major, tiles_per_seq):
    if time_major:
        (h_ref, wb_ref, wc_ref, wx_ref, cw_ref, st_ref,
         o_ref, so_ref, wbb_ref, wcb_ref, wxb_ref) = refs
    else:
        (h_ref, wb_ref, wc_ref, wx_ref, cw_ref,
         o_ref, so_ref, wbb_ref, wcb_ref, wxb_ref, carry_ref) = refs
    i = pl.program_id(1)
    _cast_weights_once(i, [(wb_ref, wbb_ref), (wc_ref, wcb_ref), (wx_ref, wxb_ref)])
    lead, rows, d = h_ref.shape
    tm = lead * rows

    h = h_ref[...].reshape(tm, d)

    def proj(w_ref):
        return lax.dot_general(h, w_ref[...], NN, preferred_element_type=F32)

    bg = proj(wbb_ref)
    u2 = proj(wcb_ref) * proj(wxb_ref)
    tn = u2.shape[1]
    u = u2.reshape(lead, rows, tn)
    if time_major:
        st = st_ref[...]
        fill1, fill2 = st[1:2], st
        so_ref[...] = u[lead - 2:lead]
    else:
        @pl.when(i % tiles_per_seq == 0)
        def _():
            carry_ref[...] = jnp.zeros((SUBLANES_V7X, tn), F32)
        prev = carry_ref[...]
        fill1, fill2 = prev[SUBLANES_V7X - 1:, :], prev[SUBLANES_V7X - 2:, :]
        carry_ref[...] = u2[tm - SUBLANES_V7X:tm, :]
        so_ref[...] = u[:, rows - 2:rows, :]
    cw = cw_ref[...]
    z = (cw[0:1, :] * _shift_time(u, 2, fill2, time_major)
         + cw[1:2, :] * _shift_time(u, 1, fill1, time_major) + cw[2:3, :] * u)
    o_ref[...] = (bg.reshape(lead, rows, tn) * z).astype(BF16)


def _conv_call(h3, lay, w_in, cw, state, jl, tn):
    lead_n, rows_n, d = h3.shape
    dc = cw.shape[2]
    nj = dc // tn
    in_specs = [
        pl.BlockSpec(lay.block, lambda j, i: lay.xmap(i)),
        pl.BlockSpec((None, d, tn), lambda j, i: (jl, 0, j)),
        pl.BlockSpec((None, d, tn), lambda j, i: (jl, 0, nj + j)),
        pl.BlockSpec((None, d, tn), lambda j, i: (jl, 0, 2 * nj + j)),
        pl.BlockSpec((None, cw.shape[1], tn), lambda j, i: (jl, 0, j)),
    ]
    args = [h3, w_in, w_in, w_in, cw]
    scratch = [pltpu.VMEM((d, tn), BF16)] * 3
    if lay.time_major:
        nb = lay.block[1]
        in_specs.append(pl.BlockSpec((2, nb, tn), lambda j, i: (0, i, j)))
        args.append(state)
        so_spec = pl.BlockSpec((2, nb, tn), lambda j, i: (0, i, j))
        so_shape = jax.ShapeDtypeStruct((2, h3.shape[1], dc), F32)
    else:
        scratch.append(pltpu.VMEM((SUBLANES_V7X, tn), F32))
        so_spec = pl.BlockSpec((1, 2, tn), lambda j, i: (i, 0, j))
        so_shape = jax.ShapeDtypeStruct((lay.n_tiles, 2, dc), F32)
    return pl.pallas_call(
        functools.partial(_conv_kernel, time_major=lay.time_major,
                          tiles_per_seq=lay.tiles_per_seq),
        grid=(nj, lay.n_tiles),
        in_specs=in_specs,
        out_specs=[pl.BlockSpec((lay.block[0], lay.block[1], tn),
                                lambda j, i: lay.xmap(i)[:2] + (j,)), so_spec],
        out_shape=[jax.ShapeDtypeStruct((lead_n, rows_n, dc), BF16), so_shape],
        scratch_shapes=scratch,
        compiler_params=_params(2),
    )(*args)


def _rw_prep_kernel(*refs, time_major, tiles_per_seq):
    if time_major:
        (x_ref, m_ref, gpre_ref, mix_ref, w1_ref, a1_ref, g1_ref, st_ref,
         xr_ref, xk_ref, xv_ref, tw_ref, ta_ref, tg_ref, so_ref) = refs
    else:
        (x_ref, m_ref, gpre_ref, mix_ref, w1_ref, a1_ref, g1_ref,
         xr_ref, xk_ref, xv_ref, tw_ref, ta_ref, tg_ref, so_ref, carry_ref) = refs
    i = pl.program_id(0)
    lead, rows, d = x_ref.shape
    n = lead * rows
    h = _modulated_pre(x_ref[...], m_ref, gpre_ref[...])
    if time_major:
        fill = st_ref[...][None]
        so_ref[...] = h[lead - 1]
    else:
        @pl.when(i % tiles_per_seq == 0)
        def _():
            carry_ref[...] = jnp.zeros((SUBLANES_V7X, d), F32)
        fill = carry_ref[SUBLANES_V7X - 1:, :]
        carry_ref[...] = h[0, rows - SUBLANES_V7X:rows, :]
        so_ref[...] = h[:, rows - 1:rows, :]
    xx = _shift_time(h, 1, fill, time_major) - h
    mix = mix_ref[...]

    def mixed(k):
        return h + xx * mix[k:k + 1, :]

    def low_rank(k, w_ref):
        return _mm(mixed(k).reshape(n, d), w_ref[...]).reshape(lead, rows, w_ref.shape[1])

    xr_ref[...] = mixed(0).astype(BF16)
    tw_ref[...] = jnp.tanh(low_rank(1, w1_ref))
    xk_ref[...] = mixed(2).astype(BF16)
    xv_ref[...] = mixed(3).astype(BF16)
    ta_ref[...] = low_rank(4, a1_ref)
    tg_ref[...] = _sigmoid(low_rank(5, g1_ref))


def _rw_prep_call(x3, lay, mod, norm_pre, mix, w1, a1, g1, state, l, jl, sub):
    lead, rows, d = x3.shape
    dl, dg = w1.shape[2], g1.shape[2]
    in_specs = [
        lay.x_spec(),
        lay.mod_spec(mod, l, sub),
        _gain_spec(norm_pre, l, sub),
        pl.BlockSpec((None, mix.shape[1], d), lambda i: (jl, 0, 0)),
        pl.BlockSpec((None, d, dl), lambda i: (jl, 0, 0)),
        pl.BlockSpec((None, d, dl), lambda i: (jl, 0, 0)),
        pl.BlockSpec((None, d, dg), lambda i: (jl, 0, 0)),
    ]
    args = [x3, mod, norm_pre, mix, w1, a1, g1]
    scratch = []
    blk = lay.block
    if lay.time_major:
        in_specs.append(pl.BlockSpec((blk[1], d), lambda i: (i, 0)))
        args.append(state)
        so_spec = pl.BlockSpec((blk[1], d), lambda i: (i, 0))
        so_shape = jax.ShapeDtypeStruct((rows, d), F32)
    else:
        scratch.append(pltpu.VMEM((SUBLANES_V7X, d), F32))
        tps = lay.tiles_per_seq
        so_spec = pl.BlockSpec((1, 1, d), lambda i: (i // tps, 0, 0))
        so_shape = jax.ShapeDtypeStruct((lead, 1, d), F32)

    def ospec(width):
        return pl.BlockSpec((blk[0], blk[1], width), lay.xmap)

    return pl.pallas_call(
        functools.partial(_rw_prep_kernel, time_major=lay.time_major,
                          tiles_per_seq=lay.tiles_per_seq),
        grid=(lay.n_tiles,),
        in_specs=in_specs,
        out_specs=[ospec(d)] * 3 + [ospec(dl), ospec(dl), ospec(dg), so_spec],
        out_shape=[jax.ShapeDtypeStruct((lead, rows, d), BF16)] * 3
        + [jax.ShapeDtypeStruct((lead, rows, dl), F32)] * 2
        + [jax.ShapeDtypeStruct((lead, rows, dg), F32), so_shape],
        scratch_shapes=scratch,
        compiler_params=_params(1),
    )(*args)


def _rw_proj_kernel(xr_ref, xk_ref, xv_ref, tw_ref, ta_ref, tg_ref, wr_ref, wk_ref, wv_ref,
                    w2_ref, a2_ref, g2_ref, w0_ref, a0_ref,
                    r_ref, k_ref, v_ref, w_ref, a_ref, g_ref, wrb_ref, wkb_ref, wvb_ref,
                    *, cols_outer, swap_tb):
    pairs = [(wr_ref, wrb_ref), (wk_ref, wkb_ref), (wv_ref, wvb_ref)]

    def put(o_ref, val):
        if swap_tb is None:
            o_ref[...] = val
        else:
            o_ref[...] = jnp.swapaxes(val.reshape(*swap_tb, val.shape[1]), 0, 1)
    if cols_outer:
        _cast_weights_once(pl.program_id(1), pairs)
    else:
        for src, dst in pairs:
            dst[...] = src[...].astype(BF16)

    def proj(x_ref, wb_ref):
        return lax.dot_general(x_ref[...], wb_ref[...], NN, preferred_element_type=F32)

    put(r_ref, proj(xr_ref, wrb_ref))
    put(k_ref, proj(xk_ref, wkb_ref))
    put(v_ref, proj(xv_ref, wvb_ref))
    put(w_ref, w0_ref[...] + _mm(tw_ref[...], w2_ref[...]))
    put(a_ref, _sigmoid(a0_ref[...] + _mm(ta_ref[...], a2_ref[...])))
    put(g_ref, _mm(tg_ref[...], g2_ref[...]))


def _rw_proj_call(xr, xk, xv, tw, ta, tg, wr, wk, wv, w2, a2, g2, w0, a0, jl, tm, tn,
                  swap_tb=None):
    m, d = xr.shape
    dl, dg = tw.shape[1], tg.shape[1]
    nr = w0.shape[0]
    cols_outer = m // tm > 2
    grid = (d // tn, m // tm) if cols_outer else (m // tm, d // tn)

    def ij(f):
        return (lambda j, i: f(i, j)) if cols_outer else f

    xspec = pl.BlockSpec((tm, d), ij(lambda i, j: (i, 0)))
    wspec = pl.BlockSpec((None, d, tn), ij(lambda i, j: (jl, 0, j)))
    vspec = pl.BlockSpec((None, 1, tn), ij(lambda i, j: (jl, 0, j)))
    ospec = pl.BlockSpec((tm, tn), ij(lambda i, j: (i, j)))
    if swap_tb is not None:
        assert tm == m and swap_tb[0] * swap_tb[1] == m
        ospec = pl.BlockSpec((swap_tb[1], swap_tb[0], tn), ij(lambda i, j: (0, 0, j)))
    out = pl.pallas_call(
        functools.partial(_rw_proj_kernel, cols_outer=cols_outer, swap_tb=swap_tb),
        grid=grid,
        in_specs=[xspec, xspec, xspec,
                  pl.BlockSpec((tm, dl), ij(lambda i, j: (i, 0))),
                  pl.BlockSpec((tm, dl), ij(lambda i, j: (i, 0))),
                  pl.BlockSpec((tm, dg), ij(lambda i, j: (i, 0))),
                  wspec, wspec, wspec,
                  pl.BlockSpec((None, dl, tn), ij(lambda i, j: (jl, 0, j))),
                  pl.BlockSpec((None, dl, tn), ij(lambda i, j: (jl, 0, j))),
                  pl.BlockSpec((None, dg, tn), ij(lambda i, j: (jl, 0, j))),
                  vspec, vspec],
        out_specs=[ospec] * 6,
        out_shape=[jax.ShapeDtypeStruct(
            (m, d) if swap_tb is None else (swap_tb[1], swap_tb[0], d), F32)] * 6,
        scratch_shapes=[pltpu.VMEM((d, tn), BF16)] * 3,
        compiler_params=_params(2),
    )(xr, xk, xv, tw, ta, tg, wr, wk, wv, w2, a2, g2,
      w0.reshape(nr, 1, d), a0.reshape(nr, 1, d))
    return [z.reshape(m, d) for z in out]


def _iota2(shape, axis):
    return lax.broadcasted_iota(jnp.int32, shape, axis)


def _same_block(shape, row_block, lane_block):
    r = lax.shift_right_logical(_iota2(shape, 0), int(math.log2(row_block)))
    c = lax.shift_right_logical(_iota2(shape, 1), int(math.log2(lane_block)))
    return r == c


def _bd_rows(x, mask):
    return jnp.where(mask, jnp.concatenate([x] * HEADS_PER_GROUP, axis=0), 0.0)


def _split_mm(x, ones):
    hi = x.astype(BF16)
    lo = (x - hi.astype(F32)).astype(BF16)
    return (lax.dot_general(hi, ones, NN, preferred_element_type=F32)
            + lax.dot_general(lo, ones, NN, preferred_element_type=F32))


def _each(fn, *lists):
    return [fn(*xs) for xs in zip(*lists)]


def _wkv_chunk(r, k, v, a_, b_, lw, cum, s_bd, c):
    hs = HEAD_SIZE
    tlanes = HEADS_PER_GROUP * c
    m_ch = _same_block((tlanes, GROUP_LANES), c, hs)
    m_tt = _same_block((tlanes, tlanes), c, c)
    m_ss = _same_block((GROUP_LANES, GROUP_LANES), hs, hs)
    t_row = _iota2((c, tlanes), 0)
    s_lane = jnp.bitwise_and(_iota2((c, tlanes), 1), c - 1)
    strict = s_lane < t_row
    incl = s_lane <= t_row

    def bd_ch(x):
        return _bd_rows(x, m_ch)

    def bd_tt(x):
        return _bd_rows(x, m_tt)

    def apply(a, x):
        return _mm(a, bd_ch(x))

    cl = _each(lambda x: x[c - 1:c, :], cum)
    at = _each(lambda a, x, l: a * jnp.exp(x - l), a_, cum, lw)
    rt = _each(lambda a, x: a * jnp.exp(x), r, cum)
    bt = _each(lambda a, x: a * jnp.exp(-x), b_, cum)
    kt = _each(lambda a, x: a * jnp.exp(-x), k, cum)
    bh = _each(lambda a, x, xl: a * jnp.exp(xl - x), b_, cum, cl)
    kh = _each(lambda a, x, xl: a * jnp.exp(xl - x), k, cum, cl)

    lhs = _each(lambda a, b: jnp.concatenate([a, b], axis=0), at, rt)
    ob = _each(lambda a, x: _mm(a, bd_ch(x), NT), lhs, bt)
    ok = _each(lambda a, x: _mm(a, bd_ch(x), NT), lhs, kt)
    a_ab = _each(lambda x: jnp.where(strict, x[:c], 0.0), ob)
    a_rb = _each(lambda x: jnp.where(incl, x[c:], 0.0), ob)
    a_ak = _each(lambda x: jnp.where(strict, x[:c], 0.0), ok)
    a_rk = _each(lambda x: jnp.where(incl, x[c:], 0.0), ok)
    akv = _each(lambda a, b, x: apply(jnp.concatenate([a, b], axis=0), x), a_ak, a_rk, v)
    av = _each(lambda x: x[:c], akv)
    y_kv = _each(lambda x: x[c:], akv)

    base = min(c, INV_BASE)
    eye = jnp.where(s_lane == t_row, 1.0, 0.0)
    in_base = jnp.bitwise_and(t_row, -base) == jnp.bitwise_and(s_lane, -base)
    tinv = _each(lambda x: eye + jnp.where(in_base, x, 0.0), a_ab)
    if base > 1:
        npow = _each(lambda x: _mm(x, bd_tt(x)), _each(lambda x: jnp.where(in_base, x, 0.0), a_ab))
        for _ in range(int(math.log2(base)) - 2):
            both = _each(lambda t, x: _mm(jnp.concatenate([t, x], axis=0), bd_tt(x)), tinv, npow)
            tinv = _each(lambda t, z: t + z[:c], tinv, both)
            npow = _each(lambda z: z[c:], both)
        if base > 2:
            tinv = _each(lambda t, x: t + _mm(t, bd_tt(x)), tinv, npow)
    size = base
    while size < c:
        lower_left = ((jnp.bitwise_and(t_row, -2 * size) == jnp.bitwise_and(s_lane, -2 * size))
                      & (jnp.bitwise_and(t_row, size) != 0) & (jnp.bitwise_and(s_lane, size) == 0))
        cross = _each(lambda x, t: _mm(jnp.where(lower_left, x, 0.0), bd_tt(t)), a_ab, tinv)
        tinv = _each(lambda t, x: t + _mm(t, bd_tt(x)), tinv, cross)
        size *= 2

    wt = _each(apply, tinv, at)
    w = _each(apply, tinv, av)
    qt = _each(lambda x, a, y: x + apply(a, y), rt, a_rb, wt)
    yi = _each(lambda a, x, y: apply(a, x) + y, a_rb, w, y_kv)
    g_off = _each(lambda x, y: jnp.where(m_ss, _mm(x, y, TN), 0.0), wt, bh)
    h_t = _each(lambda x, y, p, q: jnp.where(
        m_ss, _mm(jnp.concatenate([x, p], axis=0), jnp.concatenate([y, q], axis=0), TN), 0.0),
                w, bh, v, kh)

    y = _each(lambda q, s, x: _mm(q, s, NT) + x, qt, s_bd, yi)
    s_new = _each(lambda s, xl, g, h: s * jnp.exp(xl) + _mm(s, g) + h, s_bd, cl, g_off, h_t)
    return y, s_new


def _scan_kernel(*refs, c, d, nq, has_state, lane_packed):
    if has_state:
        (r_ref, k_ref, v_ref, w_ref, a_ref, g_ref, p_ref, s0_ref, y_ref, so_ref, sbd_ref) = refs
    else:
        (r_ref, k_ref, v_ref, w_ref, a_ref, g_ref, p_ref, y_ref, so_ref, sbd_ref) = refs
    ci = pl.program_id(1)
    hs = HEAD_SIZE
    ng = d // GROUP_LANES
    chains = [(q, g) for q in range(nq) for g in range(ng)]
    n_ch = len(chains)
    m_ss = _same_block((GROUP_LANES, GROUP_LANES), hs, hs)
    ones_bd = jnp.where(m_ss, 1.0, 0.0).astype(BF16)

    def lanes(g):
        return slice(g * GROUP_LANES, (g + 1) * GROUP_LANES)

    def rd(ref, q, g):
        if lane_packed:
            return ref[:, q * d + g * GROUP_LANES:q * d + (g + 1) * GROUP_LANES]
        return ref[q, :, lanes(g)]

    def par(row, g):
        return p_ref[row:row + 1, lanes(g)]

    @pl.when(ci == 0)
    def _():
        for i, (q, g) in enumerate(chains):
            if has_state:
                heads = [s0_ref[q, HEADS_PER_GROUP * g + h] for h in range(HEADS_PER_GROUP)]
                sbd_ref[i] = _bd_rows(jnp.concatenate(heads, axis=1), m_ss)
            else:
                sbd_ref[i] = jnp.zeros((GROUP_LANES, GROUP_LANES), F32)

    def log_decay(wq):
        z = -wq
        softplus = jnp.maximum(z, 0.0) + jnp.log(1.0 + jnp.exp(-jnp.abs(z)))
        lw = -jnp.exp(-softplus - 0.5)
        row = _iota2(lw.shape, 0)
        cum = lw
        step = 1
        while step < c:
            cum = cum + jnp.where(row >= step, pltpu.roll(cum, step, 0), 0.0)
            step *= 2
        return lw, cum

    if lane_packed:
        lw_all, cum_all = log_decay(w_ref[...])
        lw = [lw_all[:, q * d + g * GROUP_LANES:q * d + (g + 1) * GROUP_LANES] for q, g in chains]
        cum = [cum_all[:, q * d + g * GROUP_LANES:q * d + (g + 1) * GROUP_LANES] for q, g in chains]
    else:
        per_q = [log_decay(w_ref[q]) for q in range(nq)]
        lw = [per_q[q][0][:, lanes(g)] for q, g in chains]
        cum = [per_q[q][1][:, lanes(g)] for q, g in chains]

    def seg_sum(xs, split=True):
        x = jnp.concatenate(xs, axis=0)
        y = _split_mm(x, ones_bd) if split else _mm(x, ones_bd)
        return [y[i * c:(i + 1) * c] for i in range(len(xs))]

    r = [rd(r_ref, q, g) for q, g in chains]
    k = [rd(k_ref, q, g) for q, g in chains]
    v = [rd(v_ref, q, g) for q, g in chains]
    a_sig = [rd(a_ref, q, g) for q, g in chains]
    kk = [x * par(0, g) for x, (q, g) in zip(k, chains)]
    k = [x * (1.0 + (a - 1.0) * par(1, g)) for x, a, (q, g) in zip(k, a_sig, chains)]
    kk = _each(lambda x, n: x / jnp.maximum(jnp.sqrt(n), NORM_EPS), kk,
               seg_sum(_each(lambda x: x * x, kk)))
    bonus = _each(lambda s, z: s * z,
                  seg_sum([x * y * par(2, g) for x, y, (q, g) in zip(r, k, chains)], split=False),
                  v)
    y, s_new = _wkv_chunk(r, k, v, _each(lambda x: -x, kk), _each(lambda x, a: x * a, kk, a_sig),
                          lw, cum, [sbd_ref[i] for i in range(n_ch)], c)
    for i in range(n_ch):
        sbd_ref[i] = s_new[i]

    mu = _each(lambda x: x * (1.0 / hs), seg_sum(y))
    dy = _each(lambda x, m: x - m, y, mu)
    var = _each(lambda x: x * (1.0 / hs), seg_sum(_each(lambda x: x * x, dy), split=False))
    for i, (q, g) in enumerate(chains):
        yn = dy[i] * lax.rsqrt(var[i] + GN_EPS) * par(3, g) + par(4, g)
        out = ((yn + bonus[i]) * rd(g_ref, q, g)).astype(BF16)
        if lane_packed:
            y_ref[:, q * d + g * GROUP_LANES:q * d + (g + 1) * GROUP_LANES] = out
        else:
            y_ref[q, :, lanes(g)] = out

    @pl.when(ci == pl.num_programs(1) - 1)
    def _():
        for i, (q, g) in enumerate(chains):
            s = sbd_ref[i]
            s = s[0:hs] + s[hs:2 * hs] + s[2 * hs:3 * hs] + s[3 * hs:4 * hs]
            for h in range(HEADS_PER_GROUP):
                so_ref[q, HEADS_PER_GROUP * g + h] = s[:, h * hs:(h + 1) * hs]


def _scan_call(r, k, v, w, a, g, p, s0, nb, t, c, nq, time_major):
    m, d = r.shape
    nc = t // c
    ng = d // GROUP_LANES
    nh = d // HEAD_SIZE
    has_state = s0 is not None
    if time_major:
        r, k, v, w, a, g = (z.reshape(t, nb * d) for z in (r, k, v, w, a, g))
        xspec = pl.BlockSpec((c, nq * d), lambda b, ci: (ci, b))
    else:
        r, k, v, w, a, g = (z.reshape(nb, t, d) for z in (r, k, v, w, a, g))
        xspec = pl.BlockSpec((nq, c, d), lambda b, ci: (b, ci, 0))
    sspec = pl.BlockSpec((nq, nh, HEAD_SIZE, HEAD_SIZE), lambda b, ci: (b, 0, 0, 0))
    in_specs = [xspec] * 6 + [pl.BlockSpec(p.shape, lambda b, ci: (0, 0))]
    args = [r, k, v, w, a, g, p]
    if has_state:
        in_specs.append(sspec)
        args.append(s0)
    y, s_fin = pl.pallas_call(
        functools.partial(_scan_kernel, c=c, d=d, nq=nq, has_state=has_state,
                          lane_packed=time_major),
        grid=(nb // nq, nc),
        in_specs=in_specs,
        out_specs=[xspec, sspec],
        out_shape=[jax.ShapeDtypeStruct(r.shape, BF16),
                   jax.ShapeDtypeStruct((nb, nh, HEAD_SIZE, HEAD_SIZE), F32)],
        scratch_shapes=[pltpu.VMEM((nq * ng, GROUP_LANES, GROUP_LANES), F32)],
        compiler_params=_params(2),
    )(*args)
    return y.reshape(m, d), s_fin


class _Tiles(NamedTuple):
    ffn: int
    ffn_cols: int
    conv: int
    conv_cols: int
    out: int
    prep: int
    proj: tuple
    chunk: int
    scan_seqs: int


def _tiles(time_major, nb, t, d):
    if time_major:
        return _Tiles(ffn=nb, ffn_cols=512, conv=nb, conv_cols=min(d, 256), out=min(nb, 32),
                      prep=min(nb, 32), proj=(nb * t, min(d, 256)), chunk=t,
                      scan_seqs=math.gcd(nb, 4))
    return _Tiles(ffn=min(t, 2048), ffn_cols=512, conv=min(t, 2048), conv_cols=min(d, 256),
                  out=min(t, 512), prep=min(t, 256), proj=(min(nb * t, 512), min(d, 512)),
                  chunk=min(t, 64), scan_seqs=math.gcd(nb, 2))


def _trunk(x3, mod, conv_state, shift_state, wkv_state, wts, *, time_major):
    d = x3.shape[2]
    nb, t = (x3.shape[1], x3.shape[0]) if time_major else (x3.shape[0], x3.shape[1])
    depth = mod.shape[0]
    tiles = _tiles(time_major, nb, t, d)
    chunk, scan_seqs = tiles.chunk, tiles.scan_seqs
    ffn_lay = _Layout(x3.shape, time_major, tiles.ffn)
    lay = _Layout(x3.shape, time_major, tiles.conv)
    out_lay = _Layout(x3.shape, time_major, tiles.out)
    prep_lay = _Layout(x3.shape, time_major, tiles.prep)
    m = nb * t
    new_conv, new_shift, new_wkv = [], [], []

    subs = [(l, sub) for l in range(depth) for sub in range(3)]

    def takes_h(l, sub):
        return not (sub == 1 and l % 2 == 1)

    def nxt_of(l, sub):
        i = subs.index((l, sub)) + 1
        if i < len(subs) and takes_h(*subs[i]):
            return (wts["norm_pre"],) + subs[i]
        return None

    def ffn(x3, h3, l, s, sub):
        w_out = wts["ffn_w_out_bf16"].get((l, s))
        act, cast = _ffn_in_call(h3, ffn_lay, wts["ffn_w_in"], l, s, tiles.ffn_cols,
                                 wts["ffn_w_out"] if w_out is None else None)
        if w_out is None:
            w_out = wts["ffn_w_out_bf16"][(l, s)] = cast
        return _out_proj_call(x3, out_lay, mod, wts["norm_post"], act, w_out[None, None],
                              l, (0, 0), sub, HALF_STEP, nxt_of(l, sub))

    h3 = _pre_call(x3, out_lay, mod, wts["norm_pre"], 0, 0)
    for l in range(depth):
        jl = l // 2
        x3, h3 = ffn(x3, h3, l, 0, 0)
        if l % 2 == 0:
            st = conv_state[jl].transpose(1, 0, 2) if time_major else None
            bz, cs = _conv_call(h3, lay, wts["conv_w_in"], wts["conv_w"], st, jl,
                                tiles.conv_cols)
            x3, h3 = _out_proj_call(x3, out_lay, mod, wts["norm_post"], bz, wts["conv_w_out"],
                                    l, (jl, 0), 1, 1.0, nxt_of(l, 1))
            if time_major:
                new_conv.append(cs.transpose(1, 0, 2))
            else:
                tps = lay.tiles_per_seq
                new_conv.append(cs[tps - 1::tps])
        else:
            st = shift_state[jl] if time_major else None
            xr, xk, xv, tw, ta, tg, ss = _rw_prep_call(
                x3, prep_lay, mod, wts["norm_pre"], wts["rw_mix"], wts["rw_w1"], wts["rw_a1"],
                wts["rw_g1"], st, l, jl, 1)
            flat = lambda z: z.reshape(m, z.shape[2])
            r, k, v, w, a, g = _rw_proj_call(
                flat(xr), flat(xk), flat(xv), flat(tw), flat(ta), flat(tg),
                wts["rw_wr"], wts["rw_wk"], wts["rw_wv"], wts["rw_w2"], wts["rw_a2"],
                wts["rw_g2"], wts["rw_w0"], wts["rw_a0"], jl, *tiles.proj,
                swap_tb=(t, nb) if time_major else None)
            p = jnp.concatenate([
                wts["rw_kk"][jl][None], wts["rw_ka"][jl][None], wts["rw_rk"][jl].reshape(1, d),
                wts["rw_lnw"][jl][None], wts["rw_lnb"][jl][None],
                jnp.zeros((3, d), F32)], axis=0)
            s0 = wkv_state[jl] if time_major else None
            yg, s_fin = _scan_call(r, k, v, w, a, g, p, s0, nb, t, chunk, scan_seqs, False)
            if time_major:
                yg = yg.reshape(nb, t, d).transpose(1, 0, 2)
            x3, h3 = _out_proj_call(x3, out_lay, mod, wts["norm_post"], yg.reshape(x3.shape),
                                    wts["rw_wo"], l, (jl, 0), 1, 1.0, nxt_of(l, 1))
            new_shift.append(ss.reshape(nb, d))
            new_wkv.append(s_fin)
        x3, h3 = ffn(x3, h3, l, 1, 2)
    def stack(xs):
        return xs[0][None] if len(xs) == 1 else jnp.stack(xs)

    return x3, stack(new_conv), stack(new_shift), stack(new_wkv)


def kernel(x_prompt, x_sample, state_conv, state_shift, state_wkv, c_prompt, c_sample, mod_w, mod_b, norm_pre, norm_post, ffn_w_in, ffn_w_out, conv_w_in, conv_w, conv_w_out, rw_mix, rw_w0, rw_w1, rw_w2, rw_a0, rw_a1, rw_a2, rw_g1, rw_g2, rw_kk, rw_ka, rw_rk, rw_wr, rw_wk, rw_wv, rw_wo, rw_lnw, rw_lnb):
    b, t, d = x_prompt.shape
    sb, st, _ = x_sample.shape
    depth = mod_w.shape[0]
    n_sub = norm_pre.shape[1]
    wts = dict(norm_pre=norm_pre.reshape(depth, n_sub, 1, d),
               norm_post=norm_post.reshape(depth, n_sub, 1, d), ffn_w_in=ffn_w_in,
               ffn_w_out=ffn_w_out, ffn_w_out_bf16={},
               conv_w_out=conv_w_out.astype(BF16)[:, None],
               rw_wo=rw_wo.astype(BF16)[:, None],
               conv_w_in=conv_w_in, conv_w=conv_w, rw_mix=rw_mix,
               rw_w0=rw_w0, rw_w1=rw_w1, rw_w2=rw_w2, rw_a0=rw_a0, rw_a1=rw_a1, rw_a2=rw_a2,
               rw_g1=rw_g1, rw_g2=rw_g2, rw_kk=rw_kk, rw_ka=rw_ka, rw_rk=rw_rk, rw_wr=rw_wr,
               rw_wk=rw_wk, rw_wv=rw_wv, rw_lnw=rw_lnw, rw_lnb=rw_lnb)

    n_c = b + sb
    pad = (-n_c) % SUBLANES_V7X
    c_all = jnp.concatenate([c_prompt, c_sample, jnp.zeros((pad, d), F32)], axis=0)
    mod_all = _mod_call(c_all, mod_w, mod_b)
    mod_p = (mod_all[:, :b].reshape(depth, b, N_MOD, d).transpose(0, 2, 1, 3)
             .reshape(depth, N_MOD, b, 1, d))
    mod_s = mod_all[:, b:n_c].reshape(depth, sb, N_MOD, d).transpose(0, 2, 1, 3)

    y_p, conv_p, shift_p, wkv_p = _trunk(
        x_prompt, mod_p, None, None, None, wts, time_major=False)
    y_s, conv_s, shift_s, wkv_s = _trunk(
        x_sample.transpose(1, 0, 2), mod_s, state_conv, state_shift, state_wkv, wts,
        time_major=True)
    return (y_p, y_s.transpose(1, 0, 2), conv_p, shift_p, wkv_p, conv_s, shift_s, wkv_s)
```
